```python
import jax, jax.numpy as jnp
from jax import lax
import numpy as np

D_MODEL = 1024
BATCH = 8
SEQ = 8192
DEPTH = 1

HEAD_DIM = 64
ROT_DIM = HEAD_DIM // 4
ROPE_THETA = 500000.0
DIL_GROUPS = ((128, 1), (512, 4), (2048, 16))
HEADS_PER_DIL_GROUP = 2
N_DIL_HEADS = HEADS_PER_DIL_GROUP * len(DIL_GROUPS)
DSA_Q_HEADS = 6
DSA_KV_HEADS = 2
DSA_TOPK_MAX = 256
IDX_HEADS = 8
IDX_DIM = 64
MEM_LEN = 256
MEM_HEADS = 4
N_BRANCHES = 3
D_FF = 4 * D_MODEL
Q_BLOCK = 128
EPS = 1e-6

DIL_W = N_DIL_HEADS * HEAD_DIM
DIL_OUT_W = HEADS_PER_DIL_GROUP * HEAD_DIM
DSA_Q_W = DSA_Q_HEADS * HEAD_DIM
DSA_KV_W = DSA_KV_HEADS * HEAD_DIM
IDX_Q_W = IDX_HEADS * IDX_DIM
MEM_Q_W = MEM_HEADS * HEAD_DIM
GATE_W = N_BRANCHES * D_MODEL
IN_SPLITS = (DIL_W, DIL_W, DIL_W, DSA_Q_W, DSA_KV_W, DSA_KV_W, IDX_Q_W, IDX_DIM, IDX_HEADS, MEM_Q_W, GATE_W)
IN_COLS = DIL_W * 3 + DSA_Q_W + DSA_KV_W * 2 + IDX_Q_W + IDX_DIM + IDX_HEADS + MEM_Q_W + GATE_W

kernel_name = "hybrid_gated_dilated_dsa_memory_block"


def rms_norm(x, g):
    xf = x.astype(jnp.float32)
    y = xf * lax.rsqrt(jnp.mean(xf * xf, axis=-1, keepdims=True) + EPS)
    return (y * g.astype(jnp.float32)).astype(x.dtype)


def rotary(x, pos):
    half = ROT_DIM // 2
    inv = jnp.power(jnp.float32(ROPE_THETA), -jnp.arange(half, dtype=jnp.float32) / half)
    ang = pos.astype(jnp.float32)[..., None] * inv
    cos = jnp.cos(ang)[:, :, None, :]
    sin = jnp.sin(ang)[:, :, None, :]
    xr = x[..., :ROT_DIM].astype(jnp.float32)
    x1, x2 = xr[..., :half], xr[..., half:]
    rot = jnp.concatenate([x1 * cos - x2 * sin, x2 * cos + x1 * sin], axis=-1).astype(x.dtype)
    return jnp.concatenate([rot, x[..., ROT_DIM:]], axis=-1)


def split_heads(t, n, dh=HEAD_DIM):
    return t.reshape(t.shape[:-1] + (n, dh))


def dilated_window_attention(q, k, v, window, dilation):
    b, s, h, dh = q.shape
    span = window // dilation
    m = s // dilation
    nblk = -(-m // span)
    mp = nblk * span

    def to_sub(t):
        t = t.reshape(b, m, dilation, h, dh).transpose(0, 2, 1, 3, 4)
        return jnp.pad(t, ((0, 0), (0, 0), (0, mp - m), (0, 0), (0, 0)))

    def band(t):
        tp = jnp.pad(to_sub(t), ((0, 0), (0, 0), (span, 0), (0, 0), (0, 0)))
        tp = tp.reshape(b, dilation, nblk + 1, span, h, dh)
        return jnp.concatenate([tp[:, :, :-1], tp[:, :, 1:]], axis=3)

    qs = to_sub(q).reshape(b, dilation, nblk, span, h, dh)
    kb, vb = band(k), band(v)
    sc = jnp.einsum('brnqhd,brnkhd->brnhqk', qs, kb).astype(jnp.float32) * (dh ** -0.5)
    qi = jnp.arange(span)[:, None]
    kj = jnp.arange(2 * span)[None, :]
    dist = span + qi - kj
    in_band = (dist >= 0) & (dist <= span)
    blk = jnp.arange(nblk)[:, None, None]
    mask = in_band[None] & ((blk > 0) | (kj >= span)[None])
    sc = jnp.where(mask[None, None, :, None], sc, -jnp.inf)
    lse = jax.nn.logsumexp(sc, axis=-1)
    p = jnp.exp(sc - lse[..., None]).astype(v.dtype)
    o = jnp.einsum('brnhqk,brnkhd->brnqhd', p, vb)
    o = o.reshape(b, dilation, mp, h, dh)[:, :, :m].transpose(0, 2, 1, 3, 4).reshape(b, s, h, dh)
    lse = lse.transpose(0, 1, 2, 4, 3).reshape(b, dilation, mp, h)[:, :, :m]
    lse = lse.transpose(0, 2, 1, 3).reshape(b, s, h)
    return o, lse


def dsa_attention(q, k, v, qi, ki, wi):
    b, s, hq, dh = q.shape
    hkv = k.shape[2]
    grp = hq // hkv
    topk = min(DSA_TOPK_MAX, s // 4)
    nblk = s // Q_BLOCK
    key_pos = jnp.arange(s)
    gather = jax.vmap(lambda a, i: a[i])

    def blocks(t):
        return t.reshape((b, nblk, Q_BLOCK) + t.shape[2:]).swapaxes(0, 1)

    def one_block(args):
        n, qb, qib, wib = args
        t = n * Q_BLOCK + jnp.arange(Q_BLOCK)
        isc = jax.nn.relu(jnp.einsum('bqhd,bsd->bqhs', qib, ki))
        iscore = jnp.einsum('bqhs,bqh->bqs', isc, wib).astype(jnp.float32)
        causal = key_pos[None, :] <= t[:, None]
        iscore = jnp.where(causal[None], iscore, -jnp.inf)
        _, sel = lax.top_k(iscore, topk)
        valid = sel <= t[None, :, None]
        ks = gather(k, sel)
        vs = gather(v, sel)
        qg = qb.reshape(b, Q_BLOCK, hkv, grp, dh)
        sc = jnp.einsum('bqcgd,bqkcd->bqcgk', qg, ks).astype(jnp.float32) * (dh ** -0.5)
        sc = jnp.where(valid[:, :, None, None, :], sc, -jnp.inf)
        p = jax.nn.softmax(sc, axis=-1).astype(v.dtype)
        o = jnp.einsum('bqcgk,bqkcd->bqcgd', p, vs)
        return o.reshape(b, Q_BLOCK, hq * dh)

    out = lax.map(one_block, (jnp.arange(nblk), blocks(q), blocks(qi), blocks(wi)))
    return out.swapaxes(0, 1).reshape(b, s, hq * dh)


def memory_attention(qc, mem, g_mem, w_mem_kv, g_qc, g_kc):
    b, s = qc.shape[0], qc.shape[1]
    q = rms_norm(split_heads(qc, MEM_HEADS), g_qc)
    kv = (rms_norm(mem, g_mem) @ w_mem_kv).reshape(b, mem.shape[1], 2, MEM_HEADS, HEAD_DIM)
    km = rms_norm(kv[:, :, 0], g_kc)
    vm = kv[:, :, 1]
    sc = jnp.einsum('bshd,bmhd->bhsm', q, km).astype(jnp.float32) * (HEAD_DIM ** -0.5)
    p = jax.nn.softmax(sc, axis=-1).astype(vm.dtype)
    return jnp.einsum('bhsm,bmhd->bshd', p, vm).reshape(b, s, MEM_Q_W)


def hybrid_layer(x, mem, positions, g_mix, g_mem, w_in, g_qa, g_ka, g_qb, g_kb, g_qc, g_kc,
                 w_mem_kv, w_a, w_b, w_c, w_o, g_mlp, w_1, w_2):
    b, s, d = x.shape
    h = rms_norm(x, g_mix)
    proj = h @ w_in
    offs = []
    acc = 0
    for w in IN_SPLITS[:-1]:
        acc += w
        offs.append(acc)
    qa, ka, va, qb, kb, vb, qi, ki, wi, qc, gl = jnp.split(proj, offs, axis=-1)

    qa = rotary(rms_norm(split_heads(qa, N_DIL_HEADS), g_qa), positions)
    ka = rotary(rms_norm(split_heads(ka, N_DIL_HEADS), g_ka), positions)
    va = split_heads(va, N_DIL_HEADS)
    n_g = len(DIL_GROUPS)
    qa = qa.reshape(b, s, n_g, HEADS_PER_DIL_GROUP, HEAD_DIM)
    ka = ka.reshape(b, s, n_g, HEADS_PER_DIL_GROUP, HEAD_DIM)
    va = va.reshape(b, s, n_g, HEADS_PER_DIL_GROUP, HEAD_DIM)
    outs, lses = [], []
    for gi, (win, dil) in enumerate(DIL_GROUPS):
        o_g, l_g = dilated_window_attention(qa[:, :, gi], ka[:, :, gi], va[:, :, gi], win, dil)
        outs.append(o_g)
        lses.append(l_g)
    o_a = jnp.stack(outs, axis=2)
    alpha = jax.nn.softmax(jnp.stack(lses, axis=2), axis=2)
    y_a = (alpha[..., None].astype(o_a.dtype) * o_a).sum(axis=2).reshape(b, s, DIL_OUT_W) @ w_a

    qb = rotary(rms_norm(split_heads(qb, DSA_Q_HEADS), g_qb), positions)
    kb = rotary(rms_norm(split_heads(kb, DSA_KV_HEADS), g_kb), positions)
    vb = split_heads(vb, DSA_KV_HEADS)
    qi = rotary(split_heads(qi, IDX_HEADS, IDX_DIM), positions)
    ki = rotary(ki[:, :, None, :], positions)[:, :, 0]
    wi = wi * ((IDX_HEADS ** -0.5) * (IDX_DIM ** -0.5))
    y_b = dsa_attention(qb, kb, vb, qi, ki, wi) @ w_b

    y_c = memory_attention(qc, mem, g_mem, w_mem_kv, g_qc, g_kc) @ w_c

    gates = jax.nn.sigmoid(gl.astype(jnp.float32)).astype(x.dtype).reshape(b, s, N_BRANCHES, d)
    merged = gates[:, :, 0] * y_a + gates[:, :, 1] * y_b + gates[:, :, 2] * y_c
    x = x + merged @ w_o

    u = rms_norm(x, g_mlp) @ w_1
    return x + jnp.square(jax.nn.relu(u)) @ w_2


def setup_inputs(seed: int = 0) -> dict:
    key = jax.random.key(seed)
    ks = jax.random.split(key, 24)
    f32 = jnp.float32

    def nrm(k, shape, fan_in):
        return jax.random.normal(k, shape, f32) * (fan_in ** -0.5)

    def gain(k, shape):
        return 1.0 + 0.02 * jax.random.normal(k, shape, f32)

    x = jax.random.normal(ks[0], (BATCH, SEQ, D_MODEL), f32)
    mem = jax.random.normal(ks[1], (BATCH, MEM_LEN, D_MODEL), f32)
    offset = jax.random.randint(ks[2], (BATCH, 1), 0, 4096, dtype=jnp.int32)
    positions = offset + jnp.arange(SEQ, dtype=jnp.int32)[None, :]
    L = DEPTH
    return {
        "x": x,
        "mem": mem,
        "positions": positions,
        "g_mix": gain(ks[3], (L, D_MODEL)),
        "g_mem": gain(ks[4], (L, D_MODEL)),
        "w_in": nrm(ks[5], (L, D_MODEL, IN_COLS), D_MODEL),
        "g_qa": gain(ks[6], (L, HEAD_DIM)),
        "g_ka": gain(ks[7], (L, HEAD_DIM)),
        "g_qb": gain(ks[8], (L, HEAD_DIM)),
        "g_kb": gain(ks[9], (L, HEAD_DIM)),
        "g_qc": gain(ks[10], (L, HEAD_DIM)),
        "g_kc": gain(ks[11], (L, HEAD_DIM)),
        "w_mem_kv": nrm(ks[12], (L, D_MODEL, 2 * MEM_HEADS * HEAD_DIM), D_MODEL),
        "w_a": nrm(ks[13], (L, DIL_OUT_W, D_MODEL), DIL_OUT_W),
        "w_b": nrm(ks[14], (L, DSA_Q_W, D_MODEL), DSA_Q_W),
        "w_c": nrm(ks[15], (L, MEM_Q_W, D_MODEL), MEM_Q_W),
        "w_o": nrm(ks[16], (L, D_MODEL, D_MODEL), D_MODEL),
        "g_mlp": gain(ks[17], (L, D_MODEL)),
        "w_1": nrm(ks[18], (L, D_MODEL, D_FF), D_MODEL),
        "w_2": nrm(ks[19], (L, D_FF, D_MODEL), D_FF),
    }


def reference(x, mem, positions, g_mix, g_mem, w_in, g_qa, g_ka, g_qb, g_kb, g_qc, g_kc,
              w_mem_kv, w_a, w_b, w_c, w_o, g_mlp, w_1, w_2):
    for i in range(DEPTH):
        x = hybrid_layer(x, mem, positions, g_mix[i], g_mem[i], w_in[i], g_qa[i], g_ka[i],
                         g_qb[i], g_kb[i], g_qc[i], g_kc[i], w_mem_kv[i], w_a[i], w_b[i],
                         w_c[i], w_o[i], g_mlp[i], w_1[i], w_2[i])
    return x
```

```python
import functools

import jax
import jax.numpy as jnp
from jax import lax
from jax.experimental import pallas as pl
from jax.experimental.pallas import tpu as pltpu

F32 = jnp.float32
BF16 = jnp.bfloat16
I32 = jnp.int32

LANES = 128
HEAD_DIM = 64
ROT_HALF = 8
ROPE_THETA = 500000.0
EPS = 1e-6
DIL_GROUPS = ((128, 1), (512, 4), (2048, 16))
DIL_SPAN = 128
N_DIL_HEADS = 6
DSA_Q_HEADS = 6
DSA_KV_HEADS = 2
DSA_TOPK_MAX = 256
IDX_HEADS = 8
MEM_HEADS = 4
D_QA = N_DIL_HEADS * HEAD_DIM
D_QB = DSA_Q_HEADS * HEAD_DIM
D_KVB = DSA_KV_HEADS * HEAD_DIM
D_QI = IDX_HEADS * HEAD_DIM
D_QC = MEM_HEADS * HEAD_DIM
QB_HEAD_ORDER = (0, 3, 1, 4, 2, 5)
NEG_BIG = -1e30
INT_MIN = -2147483648
KEY_NEG_INF = -2139095041
VMEM_LIMIT = 56 * 1024 * 1024

_CONTRACT_LAST = (((1,), (1,)), ((), ()))


def _dot(a, b):
    return jnp.dot(a, b, preferred_element_type=F32)


def _dot_nt(a, b):
    return lax.dot_general(a, b, _CONTRACT_LAST, preferred_element_type=F32)


def _tile_lanes(a, reps):
    return a if reps == 1 else jnp.concatenate([a] * reps, axis=1)


def _low_half(shape):
    return (lax.broadcasted_iota(I32, shape, 1) % LANES) < HEAD_DIM


def _rms_rows(x, g):
    ms = jnp.mean(x * x, axis=-1, keepdims=True)
    return x * lax.rsqrt(ms + EPS) * g


def _norm_heads(p, bd, gain):
    sq = p * p
    hi = sq.astype(BF16)
    lo = (sq - hi.astype(F32)).astype(BF16)
    ms = _dot(hi, bd) + _dot(lo, bd)
    return p * lax.rsqrt(ms + EPS) * gain


def _rotary(y, cos, sin_lo, sin_hi):
    w = y.shape[1]
    reps = w // LANES
    c = _tile_lanes(cos, reps)
    a = _tile_lanes(sin_lo, reps)
    b = _tile_lanes(sin_hi, reps)
    return y * c + pltpu.roll(y, w - ROT_HALF, 1) * a + pltpu.roll(y, ROT_HALF, 1) * b


def _memkv_kernel(mem_ref, g_ref, w_ref, bd_ref, gk_ref, k_ref, v_ref):
    h = _rms_rows(mem_ref[...], g_ref[...]).astype(BF16)
    kv = _dot(h, w_ref[...])
    k_ref[...] = _norm_heads(kv[:, :D_QC], bd_ref[...], gk_ref[...]).astype(BF16)
    v_ref[...] = kv[:, D_QC:].astype(BF16)


def _memkv(mem, g_mem, w_mem_kv, bd, gk):
    b, m, d = mem.shape
    full = lambda *shape: pl.BlockSpec(shape, lambda i: (0,) * len(shape))
    return pl.pallas_call(
        _memkv_kernel,
        grid=(b,),
        in_specs=[pl.BlockSpec((None, m, d), lambda i: (i, 0, 0)), full(1, d), full(d, 2 * D_QC),
                  full(D_QC, D_QC), full(1, D_QC)],
        out_specs=[pl.BlockSpec((None, m, D_QC), lambda i: (i, 0, 0))] * 2,
        out_shape=[jax.ShapeDtypeStruct((b, m, D_QC), BF16)] * 2,
        name="memkv",
    )(mem, g_mem, w_mem_kv, bd, gk)


_C_QA, _C_KA, _C_VA = 0, 384, 768
_C_QB, _C_KB, _C_VB = 1152, 1536, 1664
_C_QI, _C_KI = 1792, 2304
_C_QC, _C_WI, _C_END = 2432, 2688, 2816


def _inproj_kernel(x_ref, g_ref, w_ref, cos_ref, slo_ref, shi_ref, bd_ref, hg_ref,
                   qa_ref, ka_ref, va_ref, qb_ref, kb_ref, vb_ref, qi_ref, ki_ref, qc_ref, wi_ref):
    h = _rms_rows(x_ref[...], g_ref[...]).astype(BF16)
    cos, slo, shi = cos_ref[...], slo_ref[...], shi_ref[...]
    bd = bd_ref[...]

    def proj(a, b):
        return _dot(h, w_ref[:, a:b])

    def norm_rot(a, b, gain_row):
        w = b - a
        y = _norm_heads(proj(a, b), bd[:w, :w], hg_ref[gain_row:gain_row + 1, :w])
        return _rotary(y, cos, slo, shi).astype(BF16)

    qa_ref[...] = norm_rot(_C_QA, _C_KA, 0)
    ka_ref[...] = norm_rot(_C_KA, _C_VA, 1)
    va_ref[...] = proj(_C_VA, _C_QB).astype(BF16)
    qb_ref[...] = norm_rot(_C_QB, _C_KB, 2)
    kb_ref[...] = norm_rot(_C_KB, _C_VB, 3)
    vb_ref[...] = proj(_C_VB, _C_QI).astype(BF16)
    qi_ref[...] = _rotary(proj(_C_QI, _C_KI), cos, slo, shi).astype(BF16)
    ki_ref[...] = _rotary(proj(_C_KI, _C_QC), cos, slo, shi).astype(BF16)
    qc_ref[...] = _norm_heads(proj(_C_QC, _C_WI), bd[:D_QC, :D_QC], hg_ref[4:5, :D_QC]).astype(BF16)
    wi_ref[...] = proj(_C_WI, _C_END) * ((IDX_HEADS ** -0.5) * (HEAD_DIM ** -0.5))


def _inproj(x, g_mix, w_pack, cos, slo, shi, bd, hg, tm):
    b, s, d = x.shape
    tok = lambda w: pl.BlockSpec((None, tm, w), lambda i, j: (i, j, 0))
    full = lambda *shape: pl.BlockSpec(shape, lambda i, j: (0,) * len(shape))
    widths = (D_QA, D_QA, D_QA, D_QB, D_KVB, D_KVB, D_QI, LANES, D_QC, LANES)
    dtypes = (BF16,) * 9 + (F32,)
    return pl.pallas_call(
        _inproj_kernel,
        grid=(b, s // tm),
        in_specs=[tok(d), full(1, d), full(d, _C_END), tok(LANES), tok(LANES), tok(LANES),
                  full(D_QA, D_QA), full(8, D_QA)],
        out_specs=[tok(w) for w in widths],
        out_shape=[jax.ShapeDtypeStruct((b, s, w), t) for w, t in zip(widths, dtypes)],
        compiler_params=pltpu.CompilerParams(dimension_semantics=("arbitrary", "arbitrary"),
                                             vmem_limit_bytes=VMEM_LIMIT),
        name="inproj",
    )(x, g_mix, w_pack, cos, slo, shi, bd, hg)


def _dilated_kernel(q_ref, k_ref, v_ref, kp_ref, vp_ref, o_ref, l_ref, *, nsub):
    n = pl.program_id(2)
    sp = DIL_SPAN
    low = _low_half((sp, LANES))
    rq = lax.broadcasted_iota(I32, (2 * sp, 2 * sp), 0) % sp
    kj = lax.broadcasted_iota(I32, (2 * sp, 2 * sp), 1)
    dist = sp + rq - kj
    band = (dist >= 0) & (dist <= sp)
    kmin = jnp.where(n > 0, 0, sp)
    for i in range(nsub):
        q = q_ref[i * sp:(i + 1) * sp, :]
        if i == 0:
            kprev, vprev = kp_ref[...], vp_ref[...]
            mask = band & (kj >= kmin)
        else:
            kprev, vprev = k_ref[(i - 1) * sp:i * sp, :], v_ref[(i - 1) * sp:i * sp, :]
            mask = band
        kk = jnp.concatenate([kprev, k_ref[i * sp:(i + 1) * sp, :]], axis=0)
        vv = jnp.concatenate([vprev, v_ref[i * sp:(i + 1) * sp, :]], axis=0)
        zero = jnp.zeros_like(q)
        qs = jnp.concatenate([jnp.where(low, q, zero), jnp.where(low, zero, q)], axis=0)
        s = jnp.where(mask, _dot_nt(qs, kk), -jnp.inf)
        m = jnp.max(s, axis=-1, keepdims=True)
        e = jnp.exp(s - m)
        den = jnp.sum(e, axis=-1, keepdims=True)
        o2 = _dot((e / den).astype(BF16), vv)
        lse = jnp.broadcast_to(m + jnp.log(den), (2 * sp, LANES))
        o_ref[i * sp:(i + 1) * sp, :] = jnp.where(low, o2[:sp], o2[sp:])
        l_ref[i * sp:(i + 1) * sp, :] = jnp.where(low, lse[:sp], lse[sp:])


def _dilated(qa, ka, va, group, dilation):
    b, s, _ = qa.shape
    m = s // dilation
    tb = min(512, m)
    nsub = tb // DIL_SPAN
    ngrp = len(DIL_GROUPS)
    view = lambda t: t.reshape(b, m, dilation * D_QA)
    cur = pl.BlockSpec((None, tb, LANES), lambda i, r, n: (i, n, r * ngrp + group))
    prev = pl.BlockSpec((None, DIL_SPAN, LANES),
                        lambda i, r, n: (i, jnp.maximum(n * nsub - 1, 0), r * ngrp + group))
    out = pl.BlockSpec((None, tb, LANES), lambda i, r, n: (i, n, r))
    o, lse = pl.pallas_call(
        functools.partial(_dilated_kernel, nsub=nsub),
        grid=(b, dilation, m // tb),
        in_specs=[cur, cur, cur, prev, prev],
        out_specs=[out, out],
        out_shape=[jax.ShapeDtypeStruct((b, m, dilation * LANES), F32)] * 2,
        compiler_params=pltpu.CompilerParams(dimension_semantics=("arbitrary",) * 3),
        name=f"dilated_g{group}",
    )(view(qa), view(ka), view(va), view(ka), view(va))
    return o.reshape(b, s, LANES), lse.reshape(b, s, LANES)


def _dsa_kernel(qi_ref, wi_ref, ki_ref, qb_ref, kb_ref, vb_ref, o_ref,
                keys_ref, wcol_ref, qim_ref, qbm_ref, thr_ref, m_ref, l_ref, acc_ref,
                *, tq, ck, rs, topk, idx_bits):
    i = pl.program_id(1)
    nkc = lax.div((i + 1) * tq + (ck - 1), ck)
    cw = ck // LANES
    low = _low_half((tq, LANES))

    for mcol in range(IDX_HEADS // 2):
        q = qi_ref[:, mcol * LANES:(mcol + 1) * LANES]
        zero = jnp.zeros_like(q)
        qim_ref[2 * mcol] = jnp.where(low, q, zero)
        qim_ref[2 * mcol + 1] = jnp.where(low, zero, q)
    for mcol in range(DSA_Q_HEADS // 2):
        q = qb_ref[:, mcol * LANES:(mcol + 1) * LANES]
        zero = jnp.zeros_like(q)
        qbm_ref[2 * mcol] = jnp.where(low, q, zero)
        qbm_ref[2 * mcol + 1] = jnp.where(low, zero, q)
    for j in range(IDX_HEADS):
        wcol_ref[j] = jnp.broadcast_to(wi_ref[:, j:j + 1], (tq, LANES))

    row_t = i * tq + lax.broadcasted_iota(I32, (tq, ck), 0)
    col_k = lax.broadcasted_iota(I32, (tq, ck), 1)

    def index_chunk(c, carry):
        off = pl.multiple_of(c * ck, ck)
        kch = ki_ref[pl.ds(off, ck), :]
        acc = jnp.zeros((tq, ck), F32)
        for j in range(IDX_HEADS):
            sc = _dot_nt(qim_ref[j], kch)
            acc = acc + jnp.maximum(sc, 0.0) * _tile_lanes(wcol_ref[j], cw)
        acc = acc + 0.0
        acc = jnp.where(col_k + off <= row_t, acc, -jnp.inf)
        bits = lax.bitcast_convert_type(acc, I32)
        keys_ref[:, pl.ds(off, ck)] = bits ^ (lax.shift_right_arithmetic(bits, 31) & 0x7FFFFFFF)
        return carry

    lax.fori_loop(0, nkc, index_chunk, 0)

    def slab(sl, carry):
        r0 = pl.multiple_of(sl * rs, rs)

        def sweep(fn, init):
            def body(c, acc):
                off = pl.multiple_of(c * ck, ck)
                for u in range(cw):
                    o2 = pl.multiple_of(off + u * LANES, LANES)
                    acc = fn(acc, keys_ref[pl.ds(r0, rs), pl.ds(o2, LANES)], o2)
                return acc
            return lax.fori_loop(0, nkc, body, init)

        def count(pred):
            acc = sweep(lambda a, kk, o2: a + jnp.where(pred(kk, o2), 1.0, 0.0), jnp.zeros((rs, LANES), F32))
            return jnp.sum(acc, axis=-1, keepdims=True)

        def bit_step(it, u):
            uc = u | lax.shift_left(jnp.int32(1), 31 - it)
            tc = jnp.broadcast_to(uc ^ INT_MIN, (rs, LANES))
            return jnp.where(count(lambda kk, o2: kk >= tc) >= float(topk), uc, u)

        u = lax.fori_loop(0, 32, bit_step, jnp.zeros((rs, 1), I32))
        thr = jnp.broadcast_to(jnp.maximum(u ^ INT_MIN, KEY_NEG_INF + 1), (rs, LANES))
        thr_ref[pl.ds(r0, rs), :] = thr

        n_ge = count(lambda kk, o2: kk >= thr)
        n_gt = count(lambda kk, o2: kk > thr)
        need = (n_ge > float(topk)) & (n_gt < float(topk))

        @pl.when(jnp.max(jnp.where(need, 1.0, 0.0)) > 0.0)
        def _():
            lane = lax.broadcasted_iota(I32, (rs, LANES), 1)
            want = jnp.where(need, float(topk) - n_gt, 3.0e38)

            def pos_step(it, x):
                xc = x | lax.shift_left(jnp.int32(1), idx_bits - 1 - it)
                xcb = jnp.broadcast_to(xc, (rs, LANES))
                below = count(lambda kk, o2: (kk == thr) & (lane + o2 < xcb))
                return jnp.where(below < want, xc, x)

            x = lax.fori_loop(0, idx_bits, pos_step, jnp.zeros((rs, 1), I32))
            xb = jnp.broadcast_to(x, (rs, LANES))

            def demote(c, carry2):
                off = pl.multiple_of(c * ck, ck)
                for uu in range(cw):
                    o2 = pl.multiple_of(off + uu * LANES, LANES)
                    kk = keys_ref[pl.ds(r0, rs), pl.ds(o2, LANES)]
                    drop = (kk == thr) & (lane + o2 > xb)
                    keys_ref[pl.ds(r0, rs), pl.ds(o2, LANES)] = jnp.where(drop, kk - 1, kk)
                return carry2

            lax.fori_loop(0, nkc, demote, 0)

        return carry

    lax.fori_loop(0, tq // rs, slab, 0)

    m_ref[...] = jnp.full(m_ref.shape, NEG_BIG, F32)
    l_ref[...] = jnp.zeros(l_ref.shape, F32)
    acc_ref[...] = jnp.zeros(acc_ref.shape, F32)

    def attend_chunk(c, carry):
        off = pl.multiple_of(c * ck, ck)
        kch = kb_ref[pl.ds(off, ck), :]
        vch = vb_ref[pl.ds(off, ck), :]
        sel = keys_ref[:, pl.ds(off, ck)] >= _tile_lanes(thr_ref[...], cw)
        bias = jnp.where(sel, 0.0, NEG_BIG)
        for j in range(DSA_Q_HEADS):
            sc = _dot_nt(qbm_ref[j], kch) + bias
            m_prev = m_ref[j]
            m_new = jnp.maximum(m_prev, jnp.max(sc, axis=-1, keepdims=True))
            alpha = jnp.exp(m_prev - m_new)
            p = jnp.exp(sc - _tile_lanes(m_new, cw))
            l_ref[j] = alpha * l_ref[j] + jnp.sum(p, axis=-1, keepdims=True)
            acc_ref[j] = alpha * acc_ref[j] + _dot(p.astype(BF16), vch)
            m_ref[j] = m_new
        return carry

    lax.fori_loop(0, nkc, attend_chunk, 0)

    for mcol in range(DSA_Q_HEADS // 2):
        o_lo = acc_ref[2 * mcol] / l_ref[2 * mcol]
        o_hi = acc_ref[2 * mcol + 1] / l_ref[2 * mcol + 1]
        o_ref[:, mcol * LANES:(mcol + 1) * LANES] = jnp.where(low, o_lo, o_hi).astype(BF16)


def _dsa(qi, wi, ki, qb, kb, vb):
    b, s, _ = qb.shape
    tq = min(256, s)
    ck = min(512, s)
    rs = 64
    topk = min(DSA_TOPK_MAX, s // 4)
    idx_bits = max(1, (s - 1).bit_length()) + 1
    qblk = lambda w: pl.BlockSpec((None, tq, w), lambda i, j: (i, j, 0))
    seq = lambda w: pl.BlockSpec((None, s, w), lambda i, j: (i, 0, 0))
    return pl.pallas_call(
        functools.partial(_dsa_kernel, tq=tq, ck=ck, rs=rs, topk=topk, idx_bits=idx_bits),
        grid=(b, s // tq),
        in_specs=[qblk(D_QI), qblk(LANES), seq(LANES), qblk(D_QB), seq(D_KVB), seq(D_KVB)],
        out_specs=qblk(D_QB),
        out_shape=jax.ShapeDtypeStruct((b, s, D_QB), BF16),
        scratch_shapes=[
            pltpu.VMEM((tq, s), I32),
            pltpu.VMEM((IDX_HEADS, tq, LANES), F32),
            pltpu.VMEM((IDX_HEADS, tq, LANES), BF16),
            pltpu.VMEM((DSA_Q_HEADS, tq, LANES), BF16),
            pltpu.VMEM((tq, LANES), I32),
            pltpu.VMEM((DSA_Q_HEADS, tq, LANES), F32),
            pltpu.VMEM((DSA_Q_HEADS, tq, LANES), F32),
            pltpu.VMEM((DSA_Q_HEADS, tq, LANES), F32),
        ],
        compiler_params=pltpu.CompilerParams(dimension_semantics=("arbitrary", "arbitrary"),
                                             vmem_limit_bytes=VMEM_LIMIT),
        name="dsa",
    )(qi, wi, ki, qb, kb, vb)


def _merge_kernel(x_ref, g_ref, wg_ref, o0_ref, l0_ref, o1_ref, l1_ref, o2_ref, l2_ref, ob_ref, qc_ref,
                  km_ref, vm_ref, wa_ref, wb_ref, wc_ref, wo_ref, out_ref):
    x = x_ref[...]
    d = x.shape[1]
    tm = x.shape[0]
    h = _rms_rows(x, g_ref[...]).astype(BF16)

    def gate(k):
        z = _dot(h, wg_ref[:, k * d:(k + 1) * d])
        return 1.0 / (1.0 + jnp.exp(-z))

    l0, l1, l2 = l0_ref[...], l1_ref[...], l2_ref[...]
    mx = jnp.maximum(jnp.maximum(l0, l1), l2)
    e0, e1, e2 = jnp.exp(l0 - mx), jnp.exp(l1 - mx), jnp.exp(l2 - mx)
    oa = (e0 * o0_ref[...] + e1 * o1_ref[...] + e2 * o2_ref[...]) / (e0 + e1 + e2)
    merged = gate(0) * _dot(oa.astype(BF16), wa_ref[...])

    merged = merged + gate(1) * _dot(ob_ref[...], wb_ref[...])

    low = _low_half((tm, LANES))
    cols = []
    for mcol in range(MEM_HEADS // 2):
        q = qc_ref[:, mcol * LANES:(mcol + 1) * LANES]
        km = km_ref[:, mcol * LANES:(mcol + 1) * LANES]
        vm = vm_ref[:, mcol * LANES:(mcol + 1) * LANES]
        zero = jnp.zeros_like(q)
        outs = []
        for qh in (jnp.where(low, q, zero), jnp.where(low, zero, q)):
            s = _dot_nt(qh, km)
            e = jnp.exp(s - jnp.max(s, axis=-1, keepdims=True))
            p = e / jnp.sum(e, axis=-1, keepdims=True)
            outs.append(_dot(p.astype(BF16), vm))
        cols.append(jnp.where(low, outs[0], outs[1]))
    oc = jnp.concatenate(cols, axis=1)
    merged = merged + gate(2) * _dot(oc.astype(BF16), wc_ref[...])

    out_ref[...] = x + _dot(merged.astype(BF16), wo_ref[...])


def _merge(x, g_mix, w_gate, dil, ob, qc, km, vm, w_a, w_b, w_c, w_o, tm):
    b, s, d = x.shape
    mlen = km.shape[1]
    tok = lambda w: pl.BlockSpec((None, tm, w), lambda i, j: (i, j, 0))
    full = lambda *shape: pl.BlockSpec(shape, lambda i, j: (0,) * len(shape))
    memb = pl.BlockSpec((None, mlen, D_QC), lambda i, j: (i, 0, 0))
    dil_args = [t for pair in dil for t in pair]
    return pl.pallas_call(
        _merge_kernel,
        grid=(b, s // tm),
        in_specs=[tok(d), full(1, d), full(d, 3 * d)] + [tok(LANES)] * 6 + [tok(D_QB), tok(D_QC), memb, memb,
                  full(LANES, d), full(D_QB, d), full(D_QC, d), full(d, d)],
        out_specs=tok(d),
        out_shape=jax.ShapeDtypeStruct((b, s, d), F32),
        compiler_params=pltpu.CompilerParams(dimension_semantics=("arbitrary", "arbitrary"),
                                             vmem_limit_bytes=VMEM_LIMIT),
        name="merge",
    )(x, g_mix, w_gate, *dil_args, ob, qc, km, vm, w_a, w_b, w_c, w_o)


def _mlp_kernel(x_ref, g_ref, w1_ref, w2_ref, out_ref, *, fchunk):
    x = x_ref[...]
    h = _rms_rows(x, g_ref[...]).astype(BF16)
    acc = x
    for c in range(w1_ref.shape[1] // fchunk):
        u = jnp.maximum(_dot(h, w1_ref[:, c * fchunk:(c + 1) * fchunk]), 0.0)
        acc = acc + _dot((u * u).astype(BF16), w2_ref[c * fchunk:(c + 1) * fchunk, :])
    out_ref[...] = acc


def _mlp(x, g_mlp, w_1, w_2, tm):
    b, s, d = x.shape
    f = w_1.shape[1]
    tok = pl.BlockSpec((None, tm, d), lambda i, j: (i, j, 0))
    full = lambda *shape: pl.BlockSpec(shape, lambda i, j: (0,) * len(shape))
    return pl.pallas_call(
        functools.partial(_mlp_kernel, fchunk=min(1024, f)),
        grid=(b, s // tm),
        in_specs=[tok, full(1, d), full(d, f), full(f, d)],
        out_specs=tok,
        out_shape=jax.ShapeDtypeStruct((b, s, d), F32),
        compiler_params=pltpu.CompilerParams(dimension_semantics=("arbitrary", "arbitrary"),
                                             vmem_limit_bytes=VMEM_LIMIT),
        name="mlp",
    )(x, g_mlp, w_1, w_2)


def _rotary_tables(positions):
    inv = jnp.power(jnp.float32(ROPE_THETA), -jnp.arange(ROT_HALF, dtype=F32) / ROT_HALF)
    ang = positions.astype(F32)[..., None] * inv
    cos, sin = jnp.cos(ang), jnp.sin(ang)
    rest = HEAD_DIM - 2 * ROT_HALF
    ones = jnp.ones(cos.shape[:-1] + (rest,), F32)
    zeros = jnp.zeros(cos.shape[:-1] + (rest,), F32)
    z8 = jnp.zeros_like(sin)
    head = lambda parts: jnp.tile(jnp.concatenate(parts, axis=-1), (1, 1, LANES // HEAD_DIM))
    return head([cos, cos, ones]), head([-sin, z8, zeros]), head([z8, sin, zeros])


def _block_diag_mean(width):
    r = jnp.arange(width) // HEAD_DIM
    return jnp.where(r[:, None] == r[None, :], 1.0 / HEAD_DIM, 0.0).astype(BF16)


def _layer(x, mem, tables, g_mix, g_mem, w_in, g_qa, g_ka, g_qb, g_kb, g_qc, g_kc,
           w_mem_kv, w_a, w_b, w_c, w_o, g_mlp, w_1, w_2):
    b, s, d = x.shape
    tm = min(256, s)
    cos, slo, shi = tables
    bd = _block_diag_mean(D_QA)
    scale = HEAD_DIM ** -0.5

    offs, acc = [], 0
    for w in (D_QA, D_QA, D_QA, D_QB, D_KVB, D_KVB, D_QI, HEAD_DIM, IDX_HEADS, D_QC):
        offs.append((acc, acc + w))
        acc += w
    seg = lambda k: w_in[:, offs[k][0]:offs[k][1]]
    perm = jnp.array(QB_HEAD_ORDER)
    w_qb = seg(3).reshape(d, DSA_Q_HEADS, HEAD_DIM)[:, perm].reshape(d, D_QB)
    w_pack = jnp.concatenate(
        [seg(0), seg(1), seg(2), w_qb, seg(4), seg(5), seg(6), seg(7), seg(7), seg(9), seg(8),
         jnp.zeros((d, LANES - IDX_HEADS), w_in.dtype)], axis=1).astype(BF16)
    w_gate = w_in[:, acc:].astype(BF16)
    w_b_perm = w_b.reshape(DSA_Q_HEADS, HEAD_DIM, d)[perm].reshape(D_QB, d)

    tile6 = lambda g: jnp.tile(g, D_QA // HEAD_DIM)
    hg = jnp.stack([tile6(g_qa) * scale, tile6(g_ka), tile6(g_qb) * scale, tile6(g_kb), tile6(g_qc) * scale,
                    jnp.zeros(D_QA), jnp.zeros(D_QA), jnp.zeros(D_QA)]).astype(F32)

    km, vm = _memkv(mem, g_mem[None, :], w_mem_kv.astype(BF16), bd[:D_QC, :D_QC],
                    jnp.tile(g_kc, MEM_HEADS)[None, :])
    qa, ka, va, qb, kb, vb, qi, ki, qc, wi = _inproj(x, g_mix[None, :], w_pack, cos, slo, shi, bd, hg, tm)
    dil = [_dilated(qa, ka, va, g, dl) for g, (_, dl) in enumerate(DIL_GROUPS)]
    ob = _dsa(qi, wi, ki, qb, kb, vb)
    x = _merge(x, g_mix[None, :], w_gate, dil, ob, qc, km, vm, w_a.astype(BF16), w_b_perm.astype(BF16),
               w_c.astype(BF16), w_o.astype(BF16), tm)
    return _mlp(x, g_mlp[None, :], w_1.astype(BF16), w_2.astype(BF16), tm)


def kernel(x, mem, positions, g_mix, g_mem, w_in, g_qa, g_ka, g_qb, g_kb, g_qc, g_kc, w_mem_kv, w_a, w_b, w_c, w_o, g_mlp, w_1, w_2):
    tables = _rotary_tables(positions)
    for i in range(g_mix.shape[0]):
        x = _layer(x, mem, tables, g_mix[i], g_mem[i], w_in[i], g_qa[i], g_ka[i], g_qb[i], g_kb[i], g_qc[i],
                   g_kc[i], w_mem_kv[i], w_a[i], w_b[i], w_c[i], w_o[i], g_mlp[i], w_1[i], w_2[i])
    return x
```

```python
import functools
import math

import jax
import jax.numpy as jnp
from jax import lax
from jax.experimental import pallas as pl
from jax.experimental.pallas import tpu as pltpu

F32 = jnp.float32
BF16 = jnp.bfloat16
I32 = jnp.int32
I16 = jnp.int16

LANES = 128
HEAD_DIM = 64
ROT_HALF = 8
ROPE_THETA = 500000.0
EPS = 1e-6
DIL_GROUPS = ((128, 1), (512, 4), (2048, 16))
DIL_SPAN = 128
N_DIL_HEADS = 6
DSA_Q_HEADS = 6
DSA_KV_HEADS = 2
DSA_GROUP = DSA_Q_HEADS // DSA_KV_HEADS
DSA_TOPK_MAX = 256
IDX_HEADS = 8
MEM_HEADS = 4
D_QA = N_DIL_HEADS * HEAD_DIM
D_QB = DSA_Q_HEADS * HEAD_DIM
D_KVB = DSA_KV_HEADS * HEAD_DIM
D_QI = IDX_HEADS * HEAD_DIM
D_QC = MEM_HEADS * HEAD_DIM
V_AUG = 80
LOG2E = math.log2(math.e)
NEG_BIG = -1e30
INT_MIN = -2147483648
I16_MIN = -32768
BF16_ONE_BITS = 0x3F80
HI_NEG_INF = -32640
SAFE_SHIFT_LOG2 = 55.0
N_ACC = 4
VMEM_LIMIT = 56 * 1024 * 1024

_CONTRACT_LAST = (((1,), (1,)), ((), ()))


def _dot(a, b):
    return jnp.dot(a, b, preferred_element_type=F32)


def _dot_nt(a, b):
    return lax.dot_general(a, b, _CONTRACT_LAST, preferred_element_type=F32)


def _tile_lanes(a, reps):
    return a if reps == 1 else jnp.concatenate([a] * reps, axis=1)


def _low_half(shape):
    return (lax.broadcasted_iota(I32, shape, 1) % LANES) < HEAD_DIM


def _rms_rows(x, g):
    ms = jnp.mean(x * x, axis=-1, keepdims=True)
    return x * lax.rsqrt(ms + EPS) * g


def _norm_heads(p, bd, gain):
    sq = p * p
    hi = sq.astype(BF16)
    lo = (sq - hi.astype(F32)).astype(BF16)
    ms = _dot(hi, bd) + _dot(lo, bd)
    return p * lax.rsqrt(ms + EPS) * gain


def _rotary(y, cos, sin_lo, sin_hi):
    w = y.shape[1]
    reps = w // LANES
    c = _tile_lanes(cos, reps)
    a = _tile_lanes(sin_lo, reps)
    b = _tile_lanes(sin_hi, reps)
    return y * c + pltpu.roll(y, w - ROT_HALF, 1) * a + pltpu.roll(y, ROT_HALF, 1) * b


def _rotary_t(blk, cos_t, sin_t):
    x1, x2 = blk[:ROT_HALF], blk[ROT_HALF:2 * ROT_HALF]
    return jnp.concatenate([x1 * cos_t - x2 * sin_t, x2 * cos_t + x1 * sin_t, blk[2 * ROT_HALF:]], axis=0)


def _memkv_kernel(mem_ref, g_ref, w_ref, bd_ref, gk_ref, k_ref, v_ref):
    h = _rms_rows(mem_ref[...], g_ref[...]).astype(BF16)
    kv = _dot(h, w_ref[...])
    k_ref[...] = _norm_heads(kv[:, :D_QC], bd_ref[...], gk_ref[...]).astype(BF16)
    v_ref[...] = kv[:, D_QC:].astype(BF16)


def _memkv(mem, g_mem, w_mem_kv, bd, gk):
    b, m, d = mem.shape
    full = lambda *shape: pl.BlockSpec(shape, lambda i: (0,) * len(shape))
    return pl.pallas_call(
        _memkv_kernel,
        grid=(b,),
        in_specs=[pl.BlockSpec((None, m, d), lambda i: (i, 0, 0)), full(1, d), full(d, 2 * D_QC),
                  full(D_QC, D_QC), full(1, D_QC)],
        out_specs=[pl.BlockSpec((None, m, D_QC), lambda i: (i, 0, 0))] * 2,
        out_shape=[jax.ShapeDtypeStruct((b, m, D_QC), BF16)] * 2,
        name="memkv",
    )(mem, g_mem, w_mem_kv, bd, gk)


_C_QA, _C_KA, _C_VA, _C_KB, _C_KI, _C_QC, _C_END = 0, 384, 768, 1152, 1280, 1408, 1664
_R_QB, _R_QI, _R_VB, _R_WI, _R_END = 0, 384, 896, 1024, 1040


def _inproj_kernel(x_ref, g_ref, w_ref, wt_ref, cos_ref, slo_ref, shi_ref, cost_ref, sint_ref, bd_ref, hg_ref, gqt_ref,
                   qa_ref, ka_ref, va_ref, kb_ref, ki_ref, qc_ref, qbt_ref, qit_ref, vta_ref, wit_ref):
    tm = x_ref.shape[0]
    h = _rms_rows(x_ref[...], g_ref[...]).astype(BF16)
    cos, slo, shi = cos_ref[...], slo_ref[...], shi_ref[...]
    bd = bd_ref[...]

    def proj(a, b):
        return _dot(h, w_ref[:, a:b])

    def norm_rot(a, b, gain_row):
        w = b - a
        y = _norm_heads(proj(a, b), bd[:w, :w], hg_ref[gain_row:gain_row + 1, :w])
        return _rotary(y, cos, slo, shi).astype(BF16)

    qa_ref[...] = norm_rot(_C_QA, _C_KA, 0)
    ka_ref[...] = norm_rot(_C_KA, _C_VA, 1)
    va_ref[...] = proj(_C_VA, _C_KB).astype(BF16)
    kb_ref[...] = norm_rot(_C_KB, _C_KI, 2)
    ki_ref[...] = _rotary(proj(_C_KI, _C_QC), cos, slo, shi)[:, :HEAD_DIM].astype(BF16)
    qc_ref[...] = _norm_heads(proj(_C_QC, _C_END), bd[:D_QC, :D_QC], hg_ref[3:4, :D_QC]).astype(BF16)

    cos_t, sin_t = cost_ref[...], sint_ref[...]
    gq = _tile_lanes(gqt_ref[...], tm // LANES)
    for hd in range(DSA_Q_HEADS):
        blk = _dot_nt(wt_ref[_R_QB + hd * HEAD_DIM:_R_QB + (hd + 1) * HEAD_DIM, :], h)
        ms = jnp.mean(blk * blk, axis=0, keepdims=True)
        qbt_ref[hd * HEAD_DIM:(hd + 1) * HEAD_DIM, :] = _rotary_t(blk * lax.rsqrt(ms + EPS) * gq, cos_t, sin_t).astype(BF16)
    for hd in range(IDX_HEADS):
        blk = _dot_nt(wt_ref[_R_QI + hd * HEAD_DIM:_R_QI + (hd + 1) * HEAD_DIM, :], h)
        qit_ref[hd * HEAD_DIM:(hd + 1) * HEAD_DIM, :] = _rotary_t(blk, cos_t, sin_t).astype(BF16)
    vt = _dot_nt(wt_ref[_R_VB:_R_WI, :], h).astype(BF16)
    pad = jnp.where(lax.broadcasted_iota(I32, (V_AUG - HEAD_DIM, tm), 0) == 0, 1.0, 0.0).astype(BF16)
    vta_ref[...] = jnp.concatenate([vt[:HEAD_DIM], pad, vt[HEAD_DIM:], pad], axis=0)
    wit_ref[...] = _dot_nt(wt_ref[_R_WI:_R_END, :], h)[:IDX_HEADS] * ((IDX_HEADS ** -0.5) * (HEAD_DIM ** -0.5))


def _inproj(x, g_mix, w_std, w_t, cos, slo, shi, cos_t, sin_t, bd, hg, gqt, tm):
    b, s, d = x.shape
    tok = lambda w: pl.BlockSpec((None, tm, w), lambda i, j: (i, j, 0))
    tok_t = lambda r: pl.BlockSpec((None, r, tm), lambda i, j: (i, 0, j))
    full = lambda *shape: pl.BlockSpec(shape, lambda i, j: (0,) * len(shape))
    std = ((D_QA, BF16), (D_QA, BF16), (D_QA, BF16), (D_KVB, BF16), (HEAD_DIM, BF16), (D_QC, BF16))
    tr = ((D_QB, BF16), (D_QI, BF16), (DSA_KV_HEADS * V_AUG, BF16), (IDX_HEADS, F32))
    return pl.pallas_call(
        _inproj_kernel,
        grid=(b, s // tm),
        in_specs=[tok(d), full(1, d), full(d, _C_END), full(_R_END, d), tok(LANES), tok(LANES), tok(LANES),
                  tok_t(ROT_HALF), tok_t(ROT_HALF), full(D_QA, D_QA), full(8, D_QA), full(HEAD_DIM, LANES)],
        out_specs=[tok(w) for w, _ in std] + [tok_t(r) for r, _ in tr],
        out_shape=[jax.ShapeDtypeStruct((b, s, w), t) for w, t in std]
        + [jax.ShapeDtypeStruct((b, r, s), t) for r, t in tr],
        compiler_params=pltpu.CompilerParams(dimension_semantics=("arbitrary", "arbitrary"),
                                             vmem_limit_bytes=VMEM_LIMIT),
        name="inproj",
    )(x, g_mix, w_std, w_t, cos, slo, shi, cos_t, sin_t, bd, hg, gqt)


def _dilated_kernel(q_ref, k_ref, v_ref, kp_ref, vp_ref, o_ref, l_ref, *, nsub):
    n = pl.program_id(2)
    sp = DIL_SPAN
    low = _low_half((sp, LANES))
    rq = lax.broadcasted_iota(I32, (2 * sp, 2 * sp), 0) % sp
    kj = lax.broadcasted_iota(I32, (2 * sp, 2 * sp), 1)
    dist = sp + rq - kj
    band = (dist >= 0) & (dist <= sp)
    kmin = jnp.where(n > 0, 0, sp)
    for i in range(nsub):
        q = q_ref[i * sp:(i + 1) * sp, :]
        if i == 0:
            kprev, vprev = kp_ref[...], vp_ref[...]
            mask = band & (kj >= kmin)
        else:
            kprev, vprev = k_ref[(i - 1) * sp:i * sp, :], v_ref[(i - 1) * sp:i * sp, :]
            mask = band
        kk = jnp.concatenate([kprev, k_ref[i * sp:(i + 1) * sp, :]], axis=0)
        vv = jnp.concatenate([vprev, v_ref[i * sp:(i + 1) * sp, :]], axis=0)
        zero = jnp.zeros_like(q)
        qs = jnp.concatenate([jnp.where(low, q, zero), jnp.where(low, zero, q)], axis=0)
        s = jnp.where(mask, _dot_nt(qs, kk), -jnp.inf)
        m = jnp.max(s, axis=-1, keepdims=True)
        e = jnp.exp(s - m)
        den = jnp.sum(e, axis=-1, keepdims=True)
        o2 = _dot((e / den).astype(BF16), vv)
        lse = jnp.broadcast_to(m + jnp.log(den), (2 * sp, LANES))
        o_ref[i * sp:(i + 1) * sp, :] = jnp.where(low, o2[:sp], o2[sp:])
        l_ref[i * sp:(i + 1) * sp, :] = jnp.where(low, lse[:sp], lse[sp:])


def _dilated(qa, ka, va, group, dilation):
    b, s, _ = qa.shape
    m = s // dilation
    tb = min(512, m)
    nsub = tb // DIL_SPAN
    ngrp = len(DIL_GROUPS)
    view = lambda t: t.reshape(b, m, dilation * D_QA)
    cur = pl.BlockSpec((None, tb, LANES), lambda i, r, n: (i, n, r * ngrp + group))
    prev = pl.BlockSpec((None, DIL_SPAN, LANES),
                        lambda i, r, n: (i, jnp.maximum(n * nsub - 1, 0), r * ngrp + group))
    out = pl.BlockSpec((None, tb, LANES), lambda i, r, n: (i, n, r))
    o, lse = pl.pallas_call(
        functools.partial(_dilated_kernel, nsub=nsub),
        grid=(b, dilation, m // tb),
        in_specs=[cur, cur, cur, prev, prev],
        out_specs=[out, out],
        out_shape=[jax.ShapeDtypeStruct((b, m, dilation * LANES), F32)] * 2,
        compiler_params=pltpu.CompilerParams(dimension_semantics=("arbitrary",) * 3),
        name=f"dilated_g{group}",
    )(view(qa), view(ka), view(va), view(ka), view(va))
    return o.reshape(b, s, LANES), lse.reshape(b, s, LANES)


def _dsa_kernel(shift_ref, qit_ref, wit_ref, ki_ref, qbt_ref, kb_ref, vta_ref, o_ref,
                hi_ref, lo_ref, sel_ref, qic_ref, qbm_ref, m_ref, acc_ref,
                *, tq, ck, rb, topk, idx_bits):
    i = pl.program_id(1)
    nrows = (i + 1) * tq
    nchunk = lax.div(nrows + (ck - 1), ck)
    nblk = lax.div(nrows, rb)
    nblk_pad = nchunk * (ck // rb)

    rows = lax.broadcasted_iota(I32, (2 * HEAD_DIM, tq), 0)
    for j in range(DSA_Q_HEADS):
        g = j // DSA_GROUP
        q = qbt_ref[j * HEAD_DIM:(j + 1) * HEAD_DIM, :]
        q2 = jnp.concatenate([q, q], axis=0)
        own = (rows >= g * HEAD_DIM) & (rows < (g + 1) * HEAD_DIM)
        qbm_ref[:, j * tq:(j + 1) * tq] = jnp.where(own, q2, jnp.zeros_like(q2))
    for h in range(IDX_HEADS):
        qic_ref[:, h * tq:(h + 1) * tq] = qit_ref[h * HEAD_DIM:(h + 1) * HEAD_DIM, :]

    kpos = lax.broadcasted_iota(I32, (ck, tq), 0)
    qpos = i * tq + lax.broadcasted_iota(I32, (ck, tq), 1)

    def index_chunk(c, carry):
        off = pl.multiple_of(c * ck, ck)
        sc = _dot(ki_ref[pl.ds(off, ck), :], qic_ref[...])
        acc = None
        for h in range(IDX_HEADS):
            t = jnp.maximum(sc[:, h * tq:(h + 1) * tq], 0.0) * wit_ref[h:h + 1, :]
            acc = t if acc is None else acc + t
        acc = jnp.where(kpos + off <= qpos, acc, -jnp.inf)
        bits = lax.bitcast_convert_type(acc, I32)
        key = jnp.where(bits < 0, INT_MIN - bits, bits)
        hi_ref[pl.ds(off, ck), :] = lax.shift_right_arithmetic(key, 16).astype(I16)
        lo_ref[pl.ds(off, ck), :] = (key ^ 0x8000).astype(I16)
        return carry

    lax.fori_loop(0, nchunk, index_chunk, 0)

    sub16 = lax.broadcasted_iota(I32, (16, tq), 0)

    def count(refs, pred):
        def body(r, accs):
            base = pl.multiple_of(r * rb, rb)
            blks = [ref[pl.ds(base, rb), :] for ref in refs]
            accs = list(accs)
            for u in range(rb // 16):
                hit = pred([blk[u * 16:(u + 1) * 16, :] for blk in blks], base + u * 16)
                accs[u % N_ACC] = accs[u % N_ACC] + jnp.where(hit, jnp.int16(1), jnp.int16(0))
            return tuple(accs)
        accs = lax.fori_loop(0, nblk, body, tuple(jnp.zeros((16, tq), I16) for _ in range(N_ACC)))
        tot = accs[0].astype(I32)
        for a in accs[1:]:
            tot = tot + a.astype(I32)
        return jnp.sum(tot, axis=0, keepdims=True)

    def rows16(v):
        return jnp.broadcast_to(v, (16, tq)).astype(I16)

    def bisect(ref, want):
        def step(it, u):
            uc = u | lax.shift_left(jnp.int32(1), 15 - it)
            cand = rows16(uc - 32768)
            cnt = count([ref], lambda b, p0: b[0] >= cand)
            return jnp.where(cnt >= want, uc, u)
        return lax.fori_loop(0, 16, step, jnp.zeros((1, tq), I32)) - 32768

    a32 = jnp.maximum(bisect(hi_ref, topk), HI_NEG_INF)
    a16 = rows16(a32)
    n_hi = count([hi_ref], lambda b, p0: b[0] > a16)

    def bucket(r, carry):
        base = pl.multiple_of(r * rb, rb)
        a_b = jnp.broadcast_to(a32, (rb, tq)).astype(I16)
        sel_ref[pl.ds(base, rb), :] = jnp.where(hi_ref[pl.ds(base, rb), :] == a_b, lo_ref[pl.ds(base, rb), :],
                                                jnp.int16(I16_MIN))
        return carry

    lax.fori_loop(0, nblk, bucket, 0)

    b32 = bisect(sel_ref, topk - n_hi)
    b32 = jnp.where(a32 == HI_NEG_INF, jnp.maximum(b32, I16_MIN + 1), b32)
    b16 = rows16(b32)

    def chosen(r, carry):
        base = pl.multiple_of(r * rb, rb)
        a_b = jnp.broadcast_to(a32, (rb, tq)).astype(I16)
        b_b = jnp.broadcast_to(b32, (rb, tq)).astype(I16)
        hi = hi_ref[pl.ds(base, rb), :]
        sel = (hi > a_b) | ((hi == a_b) & (lo_ref[pl.ds(base, rb), :] >= b_b))
        sel_ref[pl.ds(base, rb), :] = jnp.where(sel, jnp.int16(BF16_ONE_BITS), jnp.int16(0))
        return carry

    lax.fori_loop(0, nblk_pad, chosen, 0)

    n_ge = n_hi + count([hi_ref, lo_ref], lambda b, p0: (b[0] == a16) & (b[1] >= b16))
    need = n_ge > topk

    @pl.when(jnp.max(jnp.where(need, 1, 0)) > 0)
    def _():
        n_gt = n_hi + count([hi_ref, lo_ref], lambda b, p0: (b[0] == a16) & (b[1] > b16))
        want = jnp.where(need, topk - n_gt, jnp.int32(2 ** 30))

        def pos_step(it, x):
            xc = x | lax.shift_left(jnp.int32(1), idx_bits - 1 - it)
            xc16 = rows16(xc)
            below = count([hi_ref, lo_ref], lambda b, p0: (b[0] == a16) & (b[1] == b16)
                          & ((sub16 + p0).astype(I16) < xc16))
            return jnp.where(below < want, xc, x)

        x = lax.fori_loop(0, idx_bits, pos_step, jnp.zeros((1, tq), I32))

        def demote(r, carry):
            base = pl.multiple_of(r * rb, rb)
            a_b = jnp.broadcast_to(a32, (rb, tq)).astype(I16)
            b_b = jnp.broadcast_to(b32, (rb, tq)).astype(I16)
            pos = (base + lax.broadcasted_iota(I32, (rb, tq), 0)).astype(I16)
            drop = ((hi_ref[pl.ds(base, rb), :] == a_b) & (lo_ref[pl.ds(base, rb), :] == b_b)
                    & (pos > jnp.broadcast_to(x, (rb, tq)).astype(I16)))
            sel_ref[pl.ds(base, rb), :] = jnp.where(drop, jnp.int16(0), sel_ref[pl.ds(base, rb), :])
            return carry

        lax.fori_loop(0, nblk, demote, 0)

    shift = shift_ref[0]

    @pl.when(shift < SAFE_SHIFT_LOG2)
    def _():
        acc_ref[...] = jnp.zeros(acc_ref.shape, F32)

        def attend_chunk(c, carry):
            off = pl.multiple_of(c * ck, ck)
            kch = kb_ref[pl.ds(off, ck), :]
            msk = lax.bitcast_convert_type(sel_ref[pl.ds(off, ck), :], BF16)
            for g in range(DSA_KV_HEADS):
                sc = _dot(kch, qbm_ref[:, g * DSA_GROUP * tq:(g + 1) * DSA_GROUP * tq])
                p = jnp.exp2(sc - shift).astype(BF16) * _tile_lanes(msk, DSA_GROUP)
                acc_ref[g] += _dot(vta_ref[g * V_AUG:(g + 1) * V_AUG, pl.ds(off, ck)], p)
            return carry

        lax.fori_loop(0, nchunk, attend_chunk, 0)

    @pl.when(shift >= SAFE_SHIFT_LOG2)
    def _():
        m_ref[...] = jnp.full(m_ref.shape, NEG_BIG, F32)
        acc_ref[...] = jnp.zeros(acc_ref.shape, F32)

        def attend_chunk(c, carry):
            off = pl.multiple_of(c * ck, ck)
            kch = kb_ref[pl.ds(off, ck), :]
            sel = sel_ref[pl.ds(off, ck), :].astype(I32) != 0
            for j in range(DSA_Q_HEADS):
                g = j // DSA_GROUP
                cols = slice((j % DSA_GROUP) * tq, (j % DSA_GROUP + 1) * tq)
                sc = jnp.where(sel, _dot(kch, qbm_ref[:, j * tq:(j + 1) * tq]), NEG_BIG)
                m_prev = m_ref[j]
                m_new = jnp.maximum(m_prev, jnp.max(sc, axis=0, keepdims=True))
                p = jnp.where(sel, jnp.exp2(sc - m_new), 0.0).astype(BF16)
                acc_ref[g, :, cols] = jnp.exp2(m_prev - m_new) * acc_ref[g, :, cols] + _dot(
                    vta_ref[g * V_AUG:(g + 1) * V_AUG, pl.ds(off, ck)], p)
                m_ref[j] = m_new
            return carry

        lax.fori_loop(0, nchunk, attend_chunk, 0)

    for pair in range(DSA_Q_HEADS // 2):
        halves = []
        for j in (2 * pair, 2 * pair + 1):
            a = acc_ref[j // DSA_GROUP, :, (j % DSA_GROUP) * tq:(j % DSA_GROUP + 1) * tq]
            halves.append(a[:HEAD_DIM] / a[HEAD_DIM:HEAD_DIM + 1])
        o_ref[:, pair * LANES:(pair + 1) * LANES] = jnp.concatenate(halves, axis=0).T.astype(BF16)


def _dsa(shift, qit, wit, ki, qbt, kb, vta):
    b, _, s = qbt.shape
    tq = min(256, s)
    ck = min(512, s)
    rb = 256
    topk = min(DSA_TOPK_MAX, s // 4)
    idx_bits = max(1, (s - 1).bit_length()) + 1
    qt = lambda w: pl.BlockSpec((None, w, tq), lambda i, j: (i, 0, j))
    seq = lambda w: pl.BlockSpec((None, s, w), lambda i, j: (i, 0, 0))
    return pl.pallas_call(
        functools.partial(_dsa_kernel, tq=tq, ck=ck, rb=rb, topk=topk, idx_bits=idx_bits),
        grid=(b, s // tq),
        in_specs=[pl.BlockSpec(memory_space=pltpu.SMEM), qt(D_QI), qt(IDX_HEADS), seq(HEAD_DIM), qt(D_QB), seq(D_KVB),
                  pl.BlockSpec((None, DSA_KV_HEADS * V_AUG, s), lambda i, j: (i, 0, 0))],
        out_specs=pl.BlockSpec((None, tq, D_QB), lambda i, j: (i, j, 0)),
        out_shape=jax.ShapeDtypeStruct((b, s, D_QB), BF16),
        scratch_shapes=[
            pltpu.VMEM((s, tq), I16),
            pltpu.VMEM((s, tq), I16),
            pltpu.VMEM((s, tq), I16),
            pltpu.VMEM((HEAD_DIM, IDX_HEADS * tq), BF16),
            pltpu.VMEM((2 * HEAD_DIM, DSA_Q_HEADS * tq), BF16),
            pltpu.VMEM((DSA_Q_HEADS, 1, tq), F32),
            pltpu.VMEM((DSA_KV_HEADS, V_AUG, DSA_GROUP * tq), F32),
        ],
        compiler_params=pltpu.CompilerParams(dimension_semantics=("arbitrary", "arbitrary"),
                                             vmem_limit_bytes=VMEM_LIMIT),
        name="dsa",
    )(shift, qit, wit, ki, qbt, kb, vta)


def _merge_kernel(x_ref, g_ref, wg_ref, o0_ref, l0_ref, o1_ref, l1_ref, o2_ref, l2_ref, ob_ref, qc_ref,
                  km_ref, vm_ref, wa_ref, wb_ref, wc_ref, wo_ref, out_ref):
    x = x_ref[...]
    d = x.shape[1]
    tm = x.shape[0]
    h = _rms_rows(x, g_ref[...]).astype(BF16)

    def gate(k):
        z = _dot(h, wg_ref[:, k * d:(k + 1) * d])
        return 1.0 / (1.0 + jnp.exp(-z))

    l0, l1, l2 = l0_ref[...], l1_ref[...], l2_ref[...]
    mx = jnp.maximum(jnp.maximum(l0, l1), l2)
    e0, e1, e2 = jnp.exp(l0 - mx), jnp.exp(l1 - mx), jnp.exp(l2 - mx)
    oa = (e0 * o0_ref[...] + e1 * o1_ref[...] + e2 * o2_ref[...]) / (e0 + e1 + e2)
    merged = gate(0) * _dot(oa.astype(BF16), wa_ref[...])

    merged = merged + gate(1) * _dot(ob_ref[...], wb_ref[...])

    low = _low_half((tm, LANES))
    cols = []
    for mcol in range(MEM_HEADS // 2):
        q = qc_ref[:, mcol * LANES:(mcol + 1) * LANES]
        km = km_ref[:, mcol * LANES:(mcol + 1) * LANES]
        vm = vm_ref[:, mcol * LANES:(mcol + 1) * LANES]
        zero = jnp.zeros_like(q)
        outs = []
        for qh in (jnp.where(low, q, zero), jnp.where(low, zero, q)):
            s = _dot_nt(qh, km)
            e = jnp.exp(s - jnp.max(s, axis=-1, keepdims=True))
            p = e / jnp.sum(e, axis=-1, keepdims=True)
            outs.append(_dot(p.astype(BF16), vm))
        cols.append(jnp.where(low, outs[0], outs[1]))
    oc = jnp.concatenate(cols, axis=1)
    merged = merged + gate(2) * _dot(oc.astype(BF16), wc_ref[...])

    out_ref[...] = x + _dot(merged.astype(BF16), wo_ref[...])


def _merge(x, g_mix, w_gate, dil, ob, qc, km, vm, w_a, w_b, w_c, w_o, tm):
    b, s, d = x.shape
    mlen = km.shape[1]
    tok = lambda w: pl.BlockSpec((None, tm, w), lambda i, j: (i, j, 0))
    full = lambda *shape: pl.BlockSpec(shape, lambda i, j: (0,) * len(shape))
    memb = pl.BlockSpec((None, mlen, D_QC), lambda i, j: (i, 0, 0))
    dil_args = [t for pair in dil for t in pair]
    return pl.pallas_call(
        _merge_kernel,
        grid=(b, s // tm),
        in_specs=[tok(d), full(1, d), full(d, 3 * d)] + [tok(LANES)] * 6 + [tok(D_QB), tok(D_QC), memb, memb,
                  full(LANES, d), full(D_QB, d), full(D_QC, d), full(d, d)],
        out_specs=tok(d),
        out_shape=jax.ShapeDtypeStruct((b, s, d), F32),
        compiler_params=pltpu.CompilerParams(dimension_semantics=("arbitrary", "arbitrary"),
                                             vmem_limit_bytes=VMEM_LIMIT),
        name="merge",
    )(x, g_mix, w_gate, *dil_args, ob, qc, km, vm, w_a, w_b, w_c, w_o)


def _mlp_kernel(x_ref, g_ref, w1_ref, w2_ref, out_ref, *, fchunk):
    x = x_ref[...]
    h = _rms_rows(x, g_ref[...]).astype(BF16)
    acc = x
    for c in range(w1_ref.shape[1] // fchunk):
        u = jnp.maximum(_dot(h, w1_ref[:, c * fchunk:(c + 1) * fchunk]), 0.0)
        acc = acc + _dot((u * u).astype(BF16), w2_ref[c * fchunk:(c + 1) * fchunk, :])
    out_ref[...] = acc


def _mlp(x, g_mlp, w_1, w_2, tm):
    b, s, d = x.shape
    f = w_1.shape[1]
    tok = pl.BlockSpec((None, tm, d), lambda i, j: (i, j, 0))
    full = lambda *shape: pl.BlockSpec(shape, lambda i, j: (0,) * len(shape))
    return pl.pallas_call(
        functools.partial(_mlp_kernel, fchunk=min(1024, f)),
        grid=(b, s // tm),
        in_specs=[tok, full(1, d), full(d, f), full(f, d)],
        out_specs=tok,
        out_shape=jax.ShapeDtypeStruct((b, s, d), F32),
        compiler_params=pltpu.CompilerParams(dimension_semantics=("arbitrary", "arbitrary"),
                                             vmem_limit_bytes=VMEM_LIMIT),
        name="mlp",
    )(x, g_mlp, w_1, w_2)


def _rotary_tables(positions):
    inv = jnp.power(jnp.float32(ROPE_THETA), -jnp.arange(ROT_HALF, dtype=F32) / ROT_HALF)
    ang = positions.astype(F32)[..., None] * inv
    cos, sin = jnp.cos(ang), jnp.sin(ang)
    rest = HEAD_DIM - 2 * ROT_HALF
    ones = jnp.ones(cos.shape[:-1] + (rest,), F32)
    zeros = jnp.zeros(cos.shape[:-1] + (rest,), F32)
    z8 = jnp.zeros_like(sin)
    head = lambda parts: jnp.tile(jnp.concatenate(parts, axis=-1), (1, 1, LANES // HEAD_DIM))
    return (head([cos, cos, ones]), head([-sin, z8, zeros]), head([z8, sin, zeros]),
            cos.transpose(0, 2, 1), sin.transpose(0, 2, 1))


def _block_diag_mean(width):
    r = jnp.arange(width) // HEAD_DIM
    return jnp.where(r[:, None] == r[None, :], 1.0 / HEAD_DIM, 0.0).astype(BF16)


def _layer(x, mem, tables, g_mix, g_mem, w_in, g_qa, g_ka, g_qb, g_kb, g_qc, g_kc,
           w_mem_kv, w_a, w_b, w_c, w_o, g_mlp, w_1, w_2):
    b, s, d = x.shape
    tm = min(256, s)
    cos, slo, shi, cos_t, sin_t = tables
    bd = _block_diag_mean(D_QA)
    scale = HEAD_DIM ** -0.5

    offs, acc = [], 0
    for w in (D_QA, D_QA, D_QA, D_QB, D_KVB, D_KVB, D_QI, HEAD_DIM, IDX_HEADS, D_QC):
        offs.append((acc, acc + w))
        acc += w
    seg = lambda k: w_in[:, offs[k][0]:offs[k][1]]
    w_std = jnp.concatenate([seg(0), seg(1), seg(2), seg(4), seg(7), seg(7), seg(9)], axis=1).astype(BF16)
    w_t = jnp.concatenate([seg(3), seg(6), seg(5), seg(8), jnp.zeros((d, _R_END - _R_WI - IDX_HEADS), w_in.dtype)],
                          axis=1).T.astype(BF16)
    w_gate = w_in[:, acc:].astype(BF16)

    tile6 = lambda g: jnp.tile(g, D_QA // HEAD_DIM)
    hg = jnp.stack([tile6(g_qa) * scale, tile6(g_ka), tile6(g_kb), tile6(g_qc) * scale,
                    jnp.zeros(D_QA), jnp.zeros(D_QA), jnp.zeros(D_QA), jnp.zeros(D_QA)]).astype(F32)
    gqt = jnp.broadcast_to((g_qb * (scale * LOG2E))[:, None], (HEAD_DIM, LANES)).astype(F32)
    shift = (HEAD_DIM * scale * LOG2E * 1.02) * jnp.max(jnp.abs(g_qb)) * jnp.max(jnp.abs(g_kb))
    shift = jnp.reshape(shift, (1,)).astype(F32)

    km, vm = _memkv(mem, g_mem[None, :], w_mem_kv.astype(BF16), bd[:D_QC, :D_QC],
                    jnp.tile(g_kc, MEM_HEADS)[None, :])
    qa, ka, va, kb, ki, qc, qbt, qit, vta, wit = _inproj(x, g_mix[None, :], w_std, w_t, cos, slo, shi, cos_t, sin_t,
                                                         bd, hg, gqt, tm)
    dil = [_dilated(qa, ka, va, g, dl) for g, (_, dl) in enumerate(DIL_GROUPS)]
    ob = _dsa(shift, qit, wit, ki, qbt, kb, vta)
    x = _merge(x, g_mix[None, :], w_gate, dil, ob, qc, km, vm, w_a.astype(BF16), w_b.astype(BF16),
               w_c.astype(BF16), w_o.astype(BF16), tm)
    return _mlp(x, g_mlp[None, :], w_1.astype(BF16), w_2.astype(BF16), tm)


def kernel(x, mem, positions, g_mix, g_mem, w_in, g_qa, g_ka, g_qb, g_kb, g_qc, g_kc, w_mem_kv, w_a, w_b, w_c, w_o, g_mlp, w_1, w_2):
    tables = _rotary_tables(positions)
    for i in range(g_mix.shape[0]):
        x = _layer(x, mem, tables, g_mix[i], g_mem[i], w_in[i], g_qa[i], g_ka[i], g_qb[i], g_kb[i], g_qc[i],
                   g_kc[i], w_mem_kv[i], w_a[i], w_b[i], w_c[i], w_o[i], g_mlp[i], w_1[i], w_2[i])
    return x
```

```python
import functools
import math

import jax
import jax.numpy as jnp
from jax import lax
from jax.experimental import pallas as pl
from jax.experimental.pallas import tpu as pltpu

F32 = jnp.float32
BF16 = jnp.bfloat16
I32 = jnp.int32
I16 = jnp.int16

LANES = 128
HEAD_DIM = 64
ROT_HALF = 8
ROPE_THETA = 500000.0
EPS = 1e-6
DIL_GROUPS = ((128, 1), (512, 4), (2048, 16))
DIL_SPAN = 128
N_DIL_HEADS = 6
DSA_Q_HEADS = 6
DSA_KV_HEADS = 2
DSA_GROUP = DSA_Q_HEADS // DSA_KV_HEADS
DSA_TOPK_MAX = 256
IDX_HEADS = 8
MEM_HEADS = 4
D_QA = N_DIL_HEADS * HEAD_DIM
D_QB = DSA_Q_HEADS * HEAD_DIM
D_KVB = DSA_KV_HEADS * HEAD_DIM
D_QI = IDX_HEADS * HEAD_DIM
D_QC = MEM_HEADS * HEAD_DIM
V_AUG = 80
LOG2E = math.log2(math.e)
NEG_BIG = -1e30
INT_MIN = -2147483648
I16_MIN = -32768
BF16_ONE_BITS = 0x3F80
HI_NEG_INF = -32640
SAFE_SHIFT_LOG2 = 55.0
N_ACC = 4
VMEM_LIMIT = 56 * 1024 * 1024

_CONTRACT_LAST = (((1,), (1,)), ((), ()))


def _dot(a, b):
    return jnp.dot(a, b, preferred_element_type=F32)


def _dot_nt(a, b):
    return lax.dot_general(a, b, _CONTRACT_LAST, preferred_element_type=F32)


def _tile_lanes(a, reps):
    return a if reps == 1 else jnp.concatenate([a] * reps, axis=1)


def _low_half(shape):
    return (lax.broadcasted_iota(I32, shape, 1) % LANES) < HEAD_DIM


def _rms_rows(x, g):
    ms = jnp.mean(x * x, axis=-1, keepdims=True)
    return x * lax.rsqrt(ms + EPS) * g


def _norm_heads(p, bd, gain):
    sq = p * p
    hi = sq.astype(BF16)
    lo = (sq - hi.astype(F32)).astype(BF16)
    ms = _dot(hi, bd) + _dot(lo, bd)
    return p * lax.rsqrt(ms + EPS) * gain


def _rotary(y, cos, sin_lo, sin_hi):
    w = y.shape[1]
    reps = w // LANES
    c = _tile_lanes(cos, reps)
    a = _tile_lanes(sin_lo, reps)
    b = _tile_lanes(sin_hi, reps)
    return y * c + pltpu.roll(y, w - ROT_HALF, 1) * a + pltpu.roll(y, ROT_HALF, 1) * b


def _rotary_t(blk, cos_t, sin_t):
    x1, x2 = blk[:ROT_HALF], blk[ROT_HALF:2 * ROT_HALF]
    return jnp.concatenate([x1 * cos_t - x2 * sin_t, x2 * cos_t + x1 * sin_t, blk[2 * ROT_HALF:]], axis=0)


def _memkv_kernel(mem_ref, g_ref, w_ref, bd_ref, gk_ref, k_ref, v_ref):
    h = _rms_rows(mem_ref[...], g_ref[...]).astype(BF16)
    kv = _dot(h, w_ref[...])
    k_ref[...] = _norm_heads(kv[:, :D_QC], bd_ref[...], gk_ref[...]).astype(BF16)
    v_ref[...] = kv[:, D_QC:].astype(BF16)


def _memkv(mem, g_mem, w_mem_kv, bd, gk):
    b, m, d = mem.shape
    full = lambda *shape: pl.BlockSpec(shape, lambda i: (0,) * len(shape))
    return pl.pallas_call(
        _memkv_kernel,
        grid=(b,),
        in_specs=[pl.BlockSpec((None, m, d), lambda i: (i, 0, 0)), full(1, d), full(d, 2 * D_QC),
                  full(D_QC, D_QC), full(1, D_QC)],
        out_specs=[pl.BlockSpec((None, m, D_QC), lambda i: (i, 0, 0))] * 2,
        out_shape=[jax.ShapeDtypeStruct((b, m, D_QC), BF16)] * 2,
        name="memkv",
    )(mem, g_mem, w_mem_kv, bd, gk)


_C_QA, _C_KA, _C_VA, _C_KB, _C_KI, _C_QC, _C_END = 0, 384, 768, 1152, 1280, 1408, 1664
_R_QB, _R_QI, _R_VB, _R_WI, _R_END = 0, 384, 896, 1024, 1040


def _inproj_kernel(x_ref, g_ref, w_ref, wt_ref, cos_ref, slo_ref, shi_ref, cost_ref, sint_ref, bd_ref, hg_ref, gqt_ref,
                   q0_ref, q1_ref, q2_ref, k0_ref, k1_ref, k2_ref, v0_ref, v1_ref, v2_ref,
                   kb_ref, ki_ref, qc_ref, qbt_ref, qit_ref, vta_ref, wit_ref, dil_ref):
    tm = x_ref.shape[0]
    h = _rms_rows(x_ref[...], g_ref[...]).astype(BF16)
    cos, slo, shi = cos_ref[...], slo_ref[...], shi_ref[...]
    bd = bd_ref[...]

    def proj(a, b):
        return _dot(h, w_ref[:, a:b])

    def norm_rot(a, b, gain_row):
        w = b - a
        y = _norm_heads(proj(a, b), bd[:w, :w], hg_ref[gain_row:gain_row + 1, :w])
        return _rotary(y, cos, slo, shi)

    def store_dilated(y, out_refs):
        for g, ((_, dil), out_ref) in enumerate(zip(DIL_GROUPS, out_refs)):
            yg = y[:, g * LANES:(g + 1) * LANES]
            if dil == 1:
                out_ref[...] = yg.astype(BF16)
            else:
                dil_ref[...] = yg
                for r in range(dil):
                    out_ref[:, r * LANES:(r + 1) * LANES] = dil_ref[pl.ds(r, tm // dil, stride=dil), :].astype(BF16)

    store_dilated(norm_rot(_C_QA, _C_KA, 0), (q0_ref, q1_ref, q2_ref))
    store_dilated(norm_rot(_C_KA, _C_VA, 1), (k0_ref, k1_ref, k2_ref))
    store_dilated(proj(_C_VA, _C_KB), (v0_ref, v1_ref, v2_ref))
    kb_ref[...] = norm_rot(_C_KB, _C_KI, 2).astype(BF16)
    ki_ref[...] = _rotary(proj(_C_KI, _C_QC), cos, slo, shi)[:, :HEAD_DIM].astype(BF16)
    qc_ref[...] = _norm_heads(proj(_C_QC, _C_END), bd[:D_QC, :D_QC], hg_ref[3:4, :D_QC]).astype(BF16)

    pt = _dot_nt(wt_ref[...], h)
    cos_t, sin_t = cost_ref[...], sint_ref[...]
    gq = _tile_lanes(gqt_ref[...], tm // LANES)
    for hd in range(DSA_Q_HEADS):
        blk = pt[_R_QB + hd * HEAD_DIM:_R_QB + (hd + 1) * HEAD_DIM]
        ms = jnp.mean(blk * blk, axis=0, keepdims=True)
        qbt_ref[hd * HEAD_DIM:(hd + 1) * HEAD_DIM, :] = _rotary_t(blk * lax.rsqrt(ms + EPS) * gq, cos_t, sin_t).astype(BF16)
    for hd in range(IDX_HEADS):
        blk = pt[_R_QI + hd * HEAD_DIM:_R_QI + (hd + 1) * HEAD_DIM]
        qit_ref[hd * HEAD_DIM:(hd + 1) * HEAD_DIM, :] = _rotary_t(blk, cos_t, sin_t).astype(BF16)
    vt = pt[_R_VB:_R_WI].astype(BF16)
    pad = jnp.where(lax.broadcasted_iota(I32, (V_AUG - HEAD_DIM, tm), 0) == 0, 1.0, 0.0).astype(BF16)
    vta_ref[...] = jnp.concatenate([vt[:HEAD_DIM], pad, vt[HEAD_DIM:], pad], axis=0)
    wit_ref[...] = pt[_R_WI:_R_WI + IDX_HEADS] * ((IDX_HEADS ** -0.5) * (HEAD_DIM ** -0.5))


def _inproj(x, g_mix, w_std, w_t, cos, slo, shi, cos_t, sin_t, bd, hg, gqt, tm):
    b, s, d = x.shape
    tok = lambda w: pl.BlockSpec((None, tm, w), lambda i, j: (i, j, 0))
    tok_t = lambda r: pl.BlockSpec((None, r, tm), lambda i, j: (i, 0, j))
    full = lambda *shape: pl.BlockSpec(shape, lambda i, j: (0,) * len(shape))
    dils = [dl for _, dl in DIL_GROUPS] * 3
    std = ((D_KVB, BF16), (HEAD_DIM, BF16), (D_QC, BF16))
    tr = ((D_QB, BF16), (D_QI, BF16), (DSA_KV_HEADS * V_AUG, BF16), (IDX_HEADS, F32))
    return pl.pallas_call(
        _inproj_kernel,
        grid=(b, s // tm),
        in_specs=[tok(d), full(1, d), full(d, _C_END), full(_R_END, d), tok(LANES), tok(LANES), tok(LANES),
                  tok_t(ROT_HALF), tok_t(ROT_HALF), full(D_QA, D_QA), full(8, D_QA), full(HEAD_DIM, LANES)],
        out_specs=[pl.BlockSpec((None, tm // dl, dl * LANES), lambda i, j: (i, j, 0)) for dl in dils]
        + [tok(w) for w, _ in std] + [tok_t(r) for r, _ in tr],
        out_shape=[jax.ShapeDtypeStruct((b, s // dl, dl * LANES), BF16) for dl in dils]
        + [jax.ShapeDtypeStruct((b, s, w), t) for w, t in std]
        + [jax.ShapeDtypeStruct((b, r, s), t) for r, t in tr],
        scratch_shapes=[pltpu.VMEM((tm, LANES), F32)],
        compiler_params=pltpu.CompilerParams(dimension_semantics=("arbitrary", "arbitrary"),
                                             vmem_limit_bytes=VMEM_LIMIT),
        name="inproj",
    )(x, g_mix, w_std, w_t, cos, slo, shi, cos_t, sin_t, bd, hg, gqt)


def _dilated_kernel(q_ref, k_ref, v_ref, kp_ref, vp_ref, o_ref, l_ref, *, nsub):
    n = pl.program_id(2)
    sp = DIL_SPAN
    low = _low_half((sp, LANES))
    rq = lax.broadcasted_iota(I32, (2 * sp, 2 * sp), 0) % sp
    kj = lax.broadcasted_iota(I32, (2 * sp, 2 * sp), 1)
    dist = sp + rq - kj
    band = (dist >= 0) & (dist <= sp)
    kmin = jnp.where(n > 0, 0, sp)
    for i in range(nsub):
        q = q_ref[i * sp:(i + 1) * sp, :]
        if i == 0:
            kprev, vprev = kp_ref[...], vp_ref[...]
            mask = band & (kj >= kmin)
        else:
            kprev, vprev = k_ref[(i - 1) * sp:i * sp, :], v_ref[(i - 1) * sp:i * sp, :]
            mask = band
        kk = jnp.concatenate([kprev, k_ref[i * sp:(i + 1) * sp, :]], axis=0)
        vv = jnp.concatenate([vprev, v_ref[i * sp:(i + 1) * sp, :]], axis=0)
        zero = jnp.zeros_like(q)
        qs = jnp.concatenate([jnp.where(low, q, zero), jnp.where(low, zero, q)], axis=0)
        s = jnp.where(mask, _dot_nt(qs, kk), -jnp.inf)
        m = jnp.max(s, axis=-1, keepdims=True)
        e = jnp.exp(s - m)
        den = jnp.sum(e, axis=-1, keepdims=True)
        o2 = _dot((e / den).astype(BF16), vv)
        lse = jnp.broadcast_to(m + jnp.log(den), (2 * sp, LANES))
        o_ref[i * sp:(i + 1) * sp, :] = jnp.where(low, o2[:sp], o2[sp:])
        l_ref[i * sp:(i + 1) * sp, :] = jnp.where(low, lse[:sp], lse[sp:])


def _dilated(q, k, v, group):
    b, m, width = q.shape
    dilation = width // LANES
    tb = min(512, m)
    nsub = tb // DIL_SPAN
    cur = pl.BlockSpec((None, tb, LANES), lambda i, r, n: (i, n, r))
    prev = pl.BlockSpec((None, DIL_SPAN, LANES), lambda i, r, n: (i, jnp.maximum(n * nsub - 1, 0), r))
    return pl.pallas_call(
        functools.partial(_dilated_kernel, nsub=nsub),
        grid=(b, dilation, m // tb),
        in_specs=[cur, cur, cur, prev, prev],
        out_specs=[cur, cur],
        out_shape=[jax.ShapeDtypeStruct((b, m, width), F32)] * 2,
        compiler_params=pltpu.CompilerParams(dimension_semantics=("arbitrary",) * 3),
        name=f"dilated_g{group}",
    )(q, k, v, k, v)


def _dsa_kernel(shift_ref, qit_ref, wit_ref, ki_ref, qbt_ref, kb_ref, vta_ref, o_ref,
                hi_ref, lo_ref, sel_ref, qic_ref, qbm_ref, m_ref, acc_ref,
                *, tq, ck, rb, topk, idx_bits):
    i = pl.program_id(1)
    nrows = (i + 1) * tq
    nchunk = lax.div(nrows + (ck - 1), ck)
    nblk = lax.div(nrows, rb)
    nblk_pad = nchunk * (ck // rb)

    rows = lax.broadcasted_iota(I32, (2 * HEAD_DIM, tq), 0)
    for j in range(DSA_Q_HEADS):
        g = j // DSA_GROUP
        q = qbt_ref[j * HEAD_DIM:(j + 1) * HEAD_DIM, :]
        q2 = jnp.concatenate([q, q], axis=0)
        own = (rows >= g * HEAD_DIM) & (rows < (g + 1) * HEAD_DIM)
        qbm_ref[:, j * tq:(j + 1) * tq] = jnp.where(own, q2, jnp.zeros_like(q2))
    for h in range(IDX_HEADS):
        qic_ref[:, h * tq:(h + 1) * tq] = qit_ref[h * HEAD_DIM:(h + 1) * HEAD_DIM, :]

    kpos = lax.broadcasted_iota(I32, (ck, tq), 0)
    qpos = i * tq + lax.broadcasted_iota(I32, (ck, tq), 1)

    def index_chunk(c, carry):
        off = pl.multiple_of(c * ck, ck)
        sc = _dot(ki_ref[pl.ds(off, ck), :], qic_ref[...])
        acc = None
        for h in range(IDX_HEADS):
            t = jnp.maximum(sc[:, h * tq:(h + 1) * tq], 0.0) * wit_ref[h:h + 1, :]
            acc = t if acc is None else acc + t
        acc = jnp.where(kpos + off <= qpos, acc, -jnp.inf)
        bits = lax.bitcast_convert_type(acc, I32)
        key = jnp.where(bits < 0, INT_MIN - bits, bits)
        hi_ref[pl.ds(off, ck), :] = lax.shift_right_arithmetic(key, 16).astype(I16)
        lo_ref[pl.ds(off, ck), :] = (key ^ 0x8000).astype(I16)
        return carry

    def for_each_chunk(fn):
        def pair(t, carry):
            fn(2 * t, carry)
            return fn(2 * t + 1, carry)
        lax.fori_loop(0, lax.div(nchunk, 2), pair, 0)

        @pl.when(lax.rem(nchunk, 2) == 1)
        def _():
            fn(nchunk - 1, 0)

    for_each_chunk(index_chunk)

    sub16 = lax.broadcasted_iota(I32, (16, tq), 0)

    def count(refs, pred, nb=nblk, emit=None):
        def body(r, accs):
            base = pl.multiple_of(r * rb, rb)
            blks = [ref[pl.ds(base, rb), :] for ref in refs]
            if emit is not None:
                emit(blks, base)
            accs = list(accs)
            for u in range(rb // 16):
                hit = pred([blk[u * 16:(u + 1) * 16, :] for blk in blks], base + u * 16)
                accs[u % N_ACC] = accs[u % N_ACC] + jnp.where(hit, jnp.int16(1), jnp.int16(0))
            return tuple(accs)
        accs = lax.fori_loop(0, nb, body, tuple(jnp.zeros((16, tq), I16) for _ in range(N_ACC)))
        tot = accs[0].astype(I32)
        for a in accs[1:]:
            tot = tot + a.astype(I32)
        return jnp.sum(tot, axis=0, keepdims=True)

    def rows16(v):
        return jnp.broadcast_to(v, (16, tq)).astype(I16)

    def bisect(ref, want):
        def step(it, u):
            uc = u | lax.shift_left(jnp.int32(1), 15 - it)
            cand = rows16(uc - 32768)
            cnt = count([ref], lambda b, p0: b[0] >= cand)
            return jnp.where(cnt >= want, uc, u)
        return lax.fori_loop(0, 16, step, jnp.zeros((1, tq), I32)) - 32768

    a32 = jnp.maximum(bisect(hi_ref, topk), HI_NEG_INF)
    a16 = rows16(a32)
    a_blk = jnp.broadcast_to(a32, (rb, tq)).astype(I16)

    def emit_bucket(blks, base):
        sel_ref[pl.ds(base, rb), :] = jnp.where(blks[0] == a_blk, blks[1], jnp.int16(I16_MIN))

    n_hi = count([hi_ref, lo_ref], lambda b, p0: b[0] > a16, emit=emit_bucket)

    b32 = bisect(sel_ref, topk - n_hi)
    b32 = jnp.where(a32 == HI_NEG_INF, jnp.maximum(b32, I16_MIN + 1), b32)
    b16 = rows16(b32)
    b_blk = jnp.broadcast_to(b32, (rb, tq)).astype(I16)

    def emit_selection(blks, base):
        sel = (blks[0] > a_blk) | ((blks[0] == a_blk) & (blks[1] >= b_blk))
        sel_ref[pl.ds(base, rb), :] = jnp.where(sel, jnp.int16(BF16_ONE_BITS), jnp.int16(0))

    n_ge = count([hi_ref, lo_ref], lambda b, p0: (b[0] > a16) | ((b[0] == a16) & (b[1] >= b16)),
                 nb=nblk_pad, emit=emit_selection)

    need = n_ge > topk

    @pl.when(jnp.max(jnp.where(need, 1, 0)) > 0)
    def _():
        n_gt = n_hi + count([hi_ref, lo_ref], lambda b, p0: (b[0] == a16) & (b[1] > b16))
        want = jnp.where(need, topk - n_gt, jnp.int32(2 ** 30))

        def pos_step(it, x):
            xc = x | lax.shift_left(jnp.int32(1), idx_bits - 1 - it)
            xc16 = rows16(xc)
            below = count([hi_ref, lo_ref], lambda b, p0: (b[0] == a16) & (b[1] == b16)
                          & ((sub16 + p0).astype(I16) < xc16))
            return jnp.where(below < want, xc, x)

        x = lax.fori_loop(0, idx_bits, pos_step, jnp.zeros((1, tq), I32))

        def demote(r, carry):
            base = pl.multiple_of(r * rb, rb)
            a_b = jnp.broadcast_to(a32, (rb, tq)).astype(I16)
            b_b = jnp.broadcast_to(b32, (rb, tq)).astype(I16)
            pos = (base + lax.broadcasted_iota(I32, (rb, tq), 0)).astype(I16)
            drop = ((hi_ref[pl.ds(base, rb), :] == a_b) & (lo_ref[pl.ds(base, rb), :] == b_b)
                    & (pos > jnp.broadcast_to(x, (rb, tq)).astype(I16)))
            sel_ref[pl.ds(base, rb), :] = jnp.where(drop, jnp.int16(0), sel_ref[pl.ds(base, rb), :])
            return carry

        lax.fori_loop(0, nblk, demote, 0)

    shift = shift_ref[0]

    @pl.when(shift < SAFE_SHIFT_LOG2)
    def _():
        acc_ref[...] = jnp.zeros(acc_ref.shape, F32)

        def attend_chunk(c, carry):
            off = pl.multiple_of(c * ck, ck)
            kch = kb_ref[pl.ds(off, ck), :]
            msk = lax.bitcast_convert_type(sel_ref[pl.ds(off, ck), :], BF16)
            for g in range(DSA_KV_HEADS):
                sc = _dot(kch, qbm_ref[:, g * DSA_GROUP * tq:(g + 1) * DSA_GROUP * tq])
                p = jnp.exp2(sc - shift).astype(BF16) * _tile_lanes(msk, DSA_GROUP)
                acc_ref[g] += _dot(vta_ref[g * V_AUG:(g + 1) * V_AUG, pl.ds(off, ck)], p)
            return carry

        for_each_chunk(attend_chunk)

    @pl.when(shift >= SAFE_SHIFT_LOG2)
    def _():
        m_ref[...] = jnp.full(m_ref.shape, NEG_BIG, F32)
        acc_ref[...] = jnp.zeros(acc_ref.shape, F32)

        def attend_chunk(c, carry):
            off = pl.multiple_of(c * ck, ck)
            kch = kb_ref[pl.ds(off, ck), :]
            sel = sel_ref[pl.ds(off, ck), :].astype(I32) != 0
            for j in range(DSA_Q_HEADS):
                g = j // DSA_GROUP
                cols = slice((j % DSA_GROUP) * tq, (j % DSA_GROUP + 1) * tq)
                sc = jnp.where(sel, _dot(kch, qbm_ref[:, j * tq:(j + 1) * tq]), NEG_BIG)
                m_prev = m_ref[j]
                m_new = jnp.maximum(m_prev, jnp.max(sc, axis=0, keepdims=True))
                p = jnp.where(sel, jnp.exp2(sc - m_new), 0.0).astype(BF16)
                acc_ref[g, :, cols] = jnp.exp2(m_prev - m_new) * acc_ref[g, :, cols] + _dot(
                    vta_ref[g * V_AUG:(g + 1) * V_AUG, pl.ds(off, ck)], p)
                m_ref[j] = m_new
            return carry

        lax.fori_loop(0, nchunk, attend_chunk, 0)

    for pair in range(DSA_Q_HEADS // 2):
        halves = []
        for j in (2 * pair, 2 * pair + 1):
            a = acc_ref[j // DSA_GROUP, :, (j % DSA_GROUP) * tq:(j % DSA_GROUP + 1) * tq]
            halves.append(a[:HEAD_DIM] / a[HEAD_DIM:HEAD_DIM + 1])
        o_ref[:, pair * LANES:(pair + 1) * LANES] = jnp.concatenate(halves, axis=0).T.astype(BF16)


def _dsa(shift, qit, wit, ki, qbt, kb, vta):
    b, _, s = qbt.shape
    tq = min(256, s)
    ck = min(512, s)
    rb = 256
    topk = min(DSA_TOPK_MAX, s // 4)
    idx_bits = max(1, (s - 1).bit_length()) + 1
    qt = lambda w: pl.BlockSpec((None, w, tq), lambda i, j: (i, 0, j))
    seq = lambda w: pl.BlockSpec((None, s, w), lambda i, j: (i, 0, 0))
    return pl.pallas_call(
        functools.partial(_dsa_kernel, tq=tq, ck=ck, rb=rb, topk=topk, idx_bits=idx_bits),
        grid=(b, s // tq),
        in_specs=[pl.BlockSpec(memory_space=pltpu.SMEM), qt(D_QI), qt(IDX_HEADS), seq(HEAD_DIM), qt(D_QB), seq(D_KVB),
                  pl.BlockSpec((None, DSA_KV_HEADS * V_AUG, s), lambda i, j: (i, 0, 0))],
        out_specs=pl.BlockSpec((None, tq, D_QB), lambda i, j: (i, j, 0)),
        out_shape=jax.ShapeDtypeStruct((b, s, D_QB), BF16),
        scratch_shapes=[
            pltpu.VMEM((s, tq), I16),
            pltpu.VMEM((s, tq), I16),
            pltpu.VMEM((s, tq), I16),
            pltpu.VMEM((HEAD_DIM, IDX_HEADS * tq), BF16),
            pltpu.VMEM((2 * HEAD_DIM, DSA_Q_HEADS * tq), BF16),
            pltpu.VMEM((DSA_Q_HEADS, 1, tq), F32),
            pltpu.VMEM((DSA_KV_HEADS, V_AUG, DSA_GROUP * tq), F32),
        ],
        compiler_params=pltpu.CompilerParams(dimension_semantics=("arbitrary", "arbitrary"),
                                             vmem_limit_bytes=VMEM_LIMIT),
        name="dsa",
    )(shift, qit, wit, ki, qbt, kb, vta)


def _merge_kernel(x_ref, g_ref, wg_ref, o0_ref, l0_ref, o1_ref, l1_ref, o2_ref, l2_ref, ob_ref, qc_ref,
                  km_ref, vm_ref, wa_ref, wb_ref, wc_ref, wo_ref, out_ref, *stage_refs):
    x = x_ref[...]
    d = x.shape[1]
    tm = x.shape[0]
    h = _rms_rows(x, g_ref[...]).astype(BF16)

    def token_major(ref, stage_ref):
        dil = ref.shape[1] // LANES
        if dil == 1:
            return ref[...]
        for r in range(dil):
            stage_ref[pl.ds(r, tm // dil, stride=dil), :] = ref[:, r * LANES:(r + 1) * LANES]
        return stage_ref[...]

    def gate(k):
        z = _dot(h, wg_ref[:, k * d:(k + 1) * d])
        return 1.0 / (1.0 + jnp.exp(-z))

    l0, l1, l2 = l0_ref[...], token_major(l1_ref, stage_refs[0]), token_major(l2_ref, stage_refs[1])
    o0, o1, o2 = o0_ref[...], token_major(o1_ref, stage_refs[2]), token_major(o2_ref, stage_refs[3])
    mx = jnp.maximum(jnp.maximum(l0, l1), l2)
    e0, e1, e2 = jnp.exp(l0 - mx), jnp.exp(l1 - mx), jnp.exp(l2 - mx)
    oa = (e0 * o0 + e1 * o1 + e2 * o2) / (e0 + e1 + e2)
    merged = gate(0) * _dot(oa.astype(BF16), wa_ref[...])

    merged = merged + gate(1) * _dot(ob_ref[...], wb_ref[...])

    low = _low_half((tm, LANES))
    cols = []
    for mcol in range(MEM_HEADS // 2):
        q = qc_ref[:, mcol * LANES:(mcol + 1) * LANES]
        km = km_ref[:, mcol * LANES:(mcol + 1) * LANES]
        vm = vm_ref[:, mcol * LANES:(mcol + 1) * LANES]
        zero = jnp.zeros_like(q)
        outs = []
        for qh in (jnp.where(low, q, zero), jnp.where(low, zero, q)):
            s = _dot_nt(qh, km)
            e = jnp.exp(s - jnp.max(s, axis=-1, keepdims=True))
            p = e / jnp.sum(e, axis=-1, keepdims=True)
            outs.append(_dot(p.astype(BF16), vm))
        cols.append(jnp.where(low, outs[0], outs[1]))
    oc = jnp.concatenate(cols, axis=1)
    merged = merged + gate(2) * _dot(oc.astype(BF16), wc_ref[...])

    out_ref[...] = x + _dot(merged.astype(BF16), wo_ref[...])


def _merge(x, g_mix, w_gate, dil, ob, qc, km, vm, w_a, w_b, w_c, w_o, tm):
    b, s, d = x.shape
    mlen = km.shape[1]
    tok = lambda w: pl.BlockSpec((None, tm, w), lambda i, j: (i, j, 0))
    full = lambda *shape: pl.BlockSpec(shape, lambda i, j: (0,) * len(shape))
    memb = pl.BlockSpec((None, mlen, D_QC), lambda i, j: (i, 0, 0))
    dil_args = [t for pair in dil for t in pair]
    dil_specs = [pl.BlockSpec((None, tm // dl, dl * LANES), lambda i, j: (i, j, 0))
                 for _, dl in DIL_GROUPS for _ in range(2)]
    return pl.pallas_call(
        _merge_kernel,
        grid=(b, s // tm),
        in_specs=[tok(d), full(1, d), full(d, 3 * d)] + dil_specs + [tok(D_QB), tok(D_QC), memb, memb,
                  full(LANES, d), full(D_QB, d), full(D_QC, d), full(d, d)],
        out_specs=tok(d),
        out_shape=jax.ShapeDtypeStruct((b, s, d), F32),
        scratch_shapes=[pltpu.VMEM((tm, LANES), F32)] * 4,
        compiler_params=pltpu.CompilerParams(dimension_semantics=("arbitrary", "arbitrary"),
                                             vmem_limit_bytes=VMEM_LIMIT),
        name="merge",
    )(x, g_mix, w_gate, *dil_args, ob, qc, km, vm, w_a, w_b, w_c, w_o)


def _mlp_kernel(x_ref, g_ref, w1_ref, w2_ref, out_ref, *, fchunk):
    x = x_ref[...]
    h = _rms_rows(x, g_ref[...]).astype(BF16)
    acc = x
    for c in range(w1_ref.shape[1] // fchunk):
        u = jnp.maximum(_dot(h, w1_ref[:, c * fchunk:(c + 1) * fchunk]), 0.0)
        acc = acc + _dot((u * u).astype(BF16), w2_ref[c * fchunk:(c + 1) * fchunk, :])
    out_ref[...] = acc


def _mlp(x, g_mlp, w_1, w_2, tm):
    b, s, d = x.shape
    f = w_1.shape[1]
    tok = pl.BlockSpec((None, tm, d), lambda i, j: (i, j, 0))
    full = lambda *shape: pl.BlockSpec(shape, lambda i, j: (0,) * len(shape))
    return pl.pallas_call(
        functools.partial(_mlp_kernel, fchunk=min(1024, f)),
        grid=(b, s // tm),
        in_specs=[tok, full(1, d), full(d, f), full(f, d)],
        out_specs=tok,
        out_shape=jax.ShapeDtypeStruct((b, s, d), F32),
        compiler_params=pltpu.CompilerParams(dimension_semantics=("arbitrary", "arbitrary"),
                                             vmem_limit_bytes=VMEM_LIMIT),
        name="mlp",
    )(x, g_mlp, w_1, w_2)


def _rotary_tables(positions):
    inv = jnp.power(jnp.float32(ROPE_THETA), -jnp.arange(ROT_HALF, dtype=F32) / ROT_HALF)
    ang = positions.astype(F32)[..., None] * inv
    cos, sin = jnp.cos(ang), jnp.sin(ang)
    rest = HEAD_DIM - 2 * ROT_HALF
    ones = jnp.ones(cos.shape[:-1] + (rest,), F32)
    zeros = jnp.zeros(cos.shape[:-1] + (rest,), F32)
    z8 = jnp.zeros_like(sin)
    head = lambda parts: jnp.tile(jnp.concatenate(parts, axis=-1), (1, 1, LANES // HEAD_DIM))
    return (head([cos, cos, ones]), head([-sin, z8, zeros]), head([z8, sin, zeros]),
            cos.transpose(0, 2, 1), sin.transpose(0, 2, 1))


def _block_diag_mean(width):
    r = jnp.arange(width) // HEAD_DIM
    return jnp.where(r[:, None] == r[None, :], 1.0 / HEAD_DIM, 0.0).astype(BF16)


def _layer(x, mem, tables, g_mix, g_mem, w_in, g_qa, g_ka, g_qb, g_kb, g_qc, g_kc,
           w_mem_kv, w_a, w_b, w_c, w_o, g_mlp, w_1, w_2):
    b, s, d = x.shape
    tm = min(256, s)
    tm_wide = min(512, s)
    cos, slo, shi, cos_t, sin_t = tables
    bd = _block_diag_mean(D_QA)
    scale = HEAD_DIM ** -0.5

    offs, acc = [], 0
    for w in (D_QA, D_QA, D_QA, D_QB, D_KVB, D_KVB, D_QI, HEAD_DIM, IDX_HEADS, D_QC):
        offs.append((acc, acc + w))
        acc += w
    seg = lambda k: w_in[:, offs[k][0]:offs[k][1]]
    w_std = jnp.concatenate([seg(0), seg(1), seg(2), seg(4), seg(7), seg(7), seg(9)], axis=1).astype(BF16)
    w_t = jnp.concatenate([seg(3), seg(6), seg(5), seg(8), jnp.zeros((d, _R_END - _R_WI - IDX_HEADS), w_in.dtype)],
                          axis=1).T.astype(BF16)
    w_gate = w_in[:, acc:].astype(BF16)

    tile6 = lambda g: jnp.tile(g, D_QA // HEAD_DIM)
    hg = jnp.stack([tile6(g_qa) * scale, tile6(g_ka), tile6(g_kb), tile6(g_qc) * scale,
                    jnp.zeros(D_QA), jnp.zeros(D_QA), jnp.zeros(D_QA), jnp.zeros(D_QA)]).astype(F32)
    gqt = jnp.broadcast_to((g_qb * (scale * LOG2E))[:, None], (HEAD_DIM, LANES)).astype(F32)
    shift = (HEAD_DIM * scale * LOG2E * 1.02) * jnp.max(jnp.abs(g_qb)) * jnp.max(jnp.abs(g_kb))
    shift = jnp.reshape(shift, (1,)).astype(F32)

    km, vm = _memkv(mem, g_mem[None, :], w_mem_kv.astype(BF16), bd[:D_QC, :D_QC],
                    jnp.tile(g_kc, MEM_HEADS)[None, :])
    (q0, q1, q2, k0, k1, k2, v0, v1, v2, kb, ki, qc, qbt, qit, vta, wit) = _inproj(
        x, g_mix[None, :], w_std, w_t, cos, slo, shi, cos_t, sin_t, bd, hg, gqt, tm_wide)
    dil = [_dilated(q, k, v, g) for g, (q, k, v) in enumerate(((q0, k0, v0), (q1, k1, v1), (q2, k2, v2)))]
    ob = _dsa(shift, qit, wit, ki, qbt, kb, vta)
    x = _merge(x, g_mix[None, :], w_gate, dil, ob, qc, km, vm, w_a.astype(BF16), w_b.astype(BF16),
               w_c.astype(BF16), w_o.astype(BF16), tm_wide)
    return _mlp(x, g_mlp[None, :], w_1.astype(BF16), w_2.astype(BF16), tm)


def kernel(x, mem, positions, g_mix, g_mem, w_in, g_qa, g_ka, g_qb, g_kb, g_qc, g_kc, w_mem_kv, w_a, w_b, w_c, w_o, g_mlp, w_1, w_2):
    tables = _rotary_tables(positions)
    for i in range(g_mix.shape[0]):
        x = _layer(x, mem, tables, g_mix[i], g_mem[i], w_in[i], g_qa[i], g_ka[i], g_qb[i], g_kb[i], g_qc[i],
                   g_kc[i], w_mem_kv[i], w_a[i], w_b[i], w_c[i], w_o[i], g_mlp[i], w_1[i], w_2[i])
    return x
```

```python
import functools
import math

import jax
import jax.numpy as jnp
from jax import lax
from jax.experimental import pallas as pl
from jax.experimental.pallas import tpu as pltpu

F32 = jnp.float32
BF16 = jnp.bfloat16
I32 = jnp.int32
I16 = jnp.int16

LANES = 128
HEAD_DIM = 64
ROT_HALF = 8
ROPE_THETA = 500000.0
EPS = 1e-6
DIL_GROUPS = ((128, 1), (512, 4), (2048, 16))
DIL_SPAN = 128
N_DIL_HEADS = 6
DSA_Q_HEADS = 6
DSA_KV_HEADS = 2
DSA_GROUP = DSA_Q_HEADS // DSA_KV_HEADS
DSA_TOPK_MAX = 256
IDX_HEADS = 8
MEM_HEADS = 4
D_QA = N_DIL_HEADS * HEAD_DIM
D_QB = DSA_Q_HEADS * HEAD_DIM
D_KVB = DSA_KV_HEADS * HEAD_DIM
D_QI = IDX_HEADS * HEAD_DIM
D_QC = MEM_HEADS * HEAD_DIM
V_AUG = 80
LOG2E = math.log2(math.e)
NEG_BIG = -1e30
INT_MIN = -2147483648
I16_MIN = -32768
I16_MAX = 32767
FOLD_MAX_MEMBERS = 2
BF16_ONE_BITS = 0x3F80
HI_NEG_INF = -32640
SAFE_SHIFT_LOG2 = 55.0
N_ACC = 4
VMEM_LIMIT = 56 * 1024 * 1024

_CONTRACT_LAST = (((1,), (1,)), ((), ()))


def _dot(a, b):
    return jnp.dot(a, b, preferred_element_type=F32)


def _dot_nt(a, b):
    return lax.dot_general(a, b, _CONTRACT_LAST, preferred_element_type=F32)


def _tile_lanes(a, reps):
    return a if reps == 1 else jnp.concatenate([a] * reps, axis=1)


def _low_half(shape):
    return (lax.broadcasted_iota(I32, shape, 1) % LANES) < HEAD_DIM


def _rms_rows(x, g):
    ms = jnp.mean(x * x, axis=-1, keepdims=True)
    return x * lax.rsqrt(ms + EPS) * g


def _norm_heads(p, bd, gain):
    sq = p * p
    hi = sq.astype(BF16)
    lo = (sq - hi.astype(F32)).astype(BF16)
    ms = _dot(hi, bd) + _dot(lo, bd)
    return p * lax.rsqrt(ms + EPS) * gain


def _rotary(y, cos, sin_lo, sin_hi):
    w = y.shape[1]
    reps = w // LANES
    c = _tile_lanes(cos, reps)
    a = _tile_lanes(sin_lo, reps)
    b = _tile_lanes(sin_hi, reps)
    return y * c + pltpu.roll(y, w - ROT_HALF, 1) * a + pltpu.roll(y, ROT_HALF, 1) * b


def _rotary_t(blk, cos_t, sin_t):
    x1, x2 = blk[:ROT_HALF], blk[ROT_HALF:2 * ROT_HALF]
    return jnp.concatenate([x1 * cos_t - x2 * sin_t, x2 * cos_t + x1 * sin_t, blk[2 * ROT_HALF:]], axis=0)


def _memkv_kernel(mem_ref, g_ref, w_ref, bd_ref, gk_ref, k_ref, v_ref):
    h = _rms_rows(mem_ref[...], g_ref[...]).astype(BF16)
    kv = _dot(h, w_ref[...])
    k_ref[...] = _norm_heads(kv[:, :D_QC], bd_ref[...], gk_ref[...]).astype(BF16)
    v_ref[...] = kv[:, D_QC:].astype(BF16)


def _memkv(mem, g_mem, w_mem_kv, bd, gk):
    b, m, d = mem.shape
    full = lambda *shape: pl.BlockSpec(shape, lambda i: (0,) * len(shape))
    return pl.pallas_call(
        _memkv_kernel,
        grid=(b,),
        in_specs=[pl.BlockSpec((None, m, d), lambda i: (i, 0, 0)), full(1, d), full(d, 2 * D_QC),
                  full(D_QC, D_QC), full(1, D_QC)],
        out_specs=[pl.BlockSpec((None, m, D_QC), lambda i: (i, 0, 0))] * 2,
        out_shape=[jax.ShapeDtypeStruct((b, m, D_QC), BF16)] * 2,
        name="memkv",
    )(mem, g_mem, w_mem_kv, bd, gk)


_C_QA, _C_KA, _C_VA, _C_KB, _C_KI, _C_QC, _C_END = 0, 384, 768, 1152, 1280, 1408, 1664
_R_QB, _R_QI, _R_VB, _R_WI, _R_END = 0, 384, 896, 1024, 1040


def _inproj_kernel(x_ref, g_ref, w_ref, wt_ref, cos_ref, slo_ref, shi_ref, cost_ref, sint_ref, bd_ref, hg_ref, gqt_ref,
                   q0_ref, q1_ref, q2_ref, k0_ref, k1_ref, k2_ref, v0_ref, v1_ref, v2_ref,
                   kb_ref, ki_ref, qc_ref, qbt_ref, qit_ref, vta_ref, wit_ref, dil_ref):
    tm = x_ref.shape[0]
    h = _rms_rows(x_ref[...], g_ref[...]).astype(BF16)
    cos, slo, shi = cos_ref[...], slo_ref[...], shi_ref[...]
    bd = bd_ref[...]

    def proj(a, b):
        return _dot(h, w_ref[:, a:b])

    def norm_rot(a, b, gain_row):
        w = b - a
        y = _norm_heads(proj(a, b), bd[:w, :w], hg_ref[gain_row:gain_row + 1, :w])
        return _rotary(y, cos, slo, shi)

    def store_dilated(y, out_refs):
        for g, ((_, dil), out_ref) in enumerate(zip(DIL_GROUPS, out_refs)):
            yg = y[:, g * LANES:(g + 1) * LANES]
            if dil == 1:
                out_ref[...] = yg.astype(BF16)
            else:
                dil_ref[...] = yg
                for r in range(dil):
                    out_ref[:, r * LANES:(r + 1) * LANES] = dil_ref[pl.ds(r, tm // dil, stride=dil), :].astype(BF16)

    store_dilated(norm_rot(_C_QA, _C_KA, 0), (q0_ref, q1_ref, q2_ref))
    store_dilated(norm_rot(_C_KA, _C_VA, 1), (k0_ref, k1_ref, k2_ref))
    store_dilated(proj(_C_VA, _C_KB), (v0_ref, v1_ref, v2_ref))
    kb_ref[...] = norm_rot(_C_KB, _C_KI, 2).astype(BF16)
    ki_ref[...] = _rotary(proj(_C_KI, _C_QC), cos, slo, shi)[:, :HEAD_DIM].astype(BF16)
    qc_ref[...] = _norm_heads(proj(_C_QC, _C_END), bd[:D_QC, :D_QC], hg_ref[3:4, :D_QC]).astype(BF16)

    pt = _dot_nt(wt_ref[...], h)
    cos_t, sin_t = cost_ref[...], sint_ref[...]
    gq = _tile_lanes(gqt_ref[...], tm // LANES)
    for hd in range(DSA_Q_HEADS):
        blk = pt[_R_QB + hd * HEAD_DIM:_R_QB + (hd + 1) * HEAD_DIM]
        ms = jnp.mean(blk * blk, axis=0, keepdims=True)
        qbt_ref[hd * HEAD_DIM:(hd + 1) * HEAD_DIM, :] = _rotary_t(blk * lax.rsqrt(ms + EPS) * gq, cos_t, sin_t).astype(BF16)
    for hd in range(IDX_HEADS):
        blk = pt[_R_QI + hd * HEAD_DIM:_R_QI + (hd + 1) * HEAD_DIM]
        qit_ref[hd * HEAD_DIM:(hd + 1) * HEAD_DIM, :] = _rotary_t(blk, cos_t, sin_t).astype(BF16)
    vt = pt[_R_VB:_R_WI].astype(BF16)
    pad = jnp.where(lax.broadcasted_iota(I32, (V_AUG - HEAD_DIM, tm), 0) == 0, 1.0, 0.0).astype(BF16)
    vta_ref[...] = jnp.concatenate([vt[:HEAD_DIM], pad, vt[HEAD_DIM:], pad], axis=0)
    wit_ref[...] = pt[_R_WI:_R_WI + IDX_HEADS] * ((IDX_HEADS ** -0.5) * (HEAD_DIM ** -0.5))


def _inproj(x, g_mix, w_std, w_t, cos, slo, shi, cos_t, sin_t, bd, hg, gqt, tm):
    b, s, d = x.shape
    tok = lambda w: pl.BlockSpec((None, tm, w), lambda i, j: (i, j, 0))
    tok_t = lambda r: pl.BlockSpec((None, r, tm), lambda i, j: (i, 0, j))
    full = lambda *shape: pl.BlockSpec(shape, lambda i, j: (0,) * len(shape))
    dils = [dl for _, dl in DIL_GROUPS] * 3
    std = ((D_KVB, BF16), (HEAD_DIM, BF16), (D_QC, BF16))
    tr = ((D_QB, BF16), (D_QI, BF16), (DSA_KV_HEADS * V_AUG, BF16), (IDX_HEADS, F32))
    return pl.pallas_call(
        _inproj_kernel,
        grid=(b, s // tm),
        in_specs=[tok(d), full(1, d), full(d, _C_END), full(_R_END, d), tok(LANES), tok(LANES), tok(LANES),
                  tok_t(ROT_HALF), tok_t(ROT_HALF), full(D_QA, D_QA), full(8, D_QA), full(HEAD_DIM, LANES)],
        out_specs=[pl.BlockSpec((None, tm // dl, dl * LANES), lambda i, j: (i, j, 0)) for dl in dils]
        + [tok(w) for w, _ in std] + [tok_t(r) for r, _ in tr],
        out_shape=[jax.ShapeDtypeStruct((b, s // dl, dl * LANES), BF16) for dl in dils]
        + [jax.ShapeDtypeStruct((b, s, w), t) for w, t in std]
        + [jax.ShapeDtypeStruct((b, r, s), t) for r, t in tr],
        scratch_shapes=[pltpu.VMEM((tm, LANES), F32)],
        compiler_params=pltpu.CompilerParams(dimension_semantics=("arbitrary", "arbitrary"),
                                             vmem_limit_bytes=VMEM_LIMIT),
        name="inproj",
    )(x, g_mix, w_std, w_t, cos, slo, shi, cos_t, sin_t, bd, hg, gqt)


def _dilated_kernel(q_ref, k_ref, v_ref, kp_ref, vp_ref, o_ref, l_ref, *, nsub):
    n = pl.program_id(2)
    sp = DIL_SPAN
    low = _low_half((sp, LANES))
    rq = lax.broadcasted_iota(I32, (2 * sp, 2 * sp), 0) % sp
    kj = lax.broadcasted_iota(I32, (2 * sp, 2 * sp), 1)
    dist = sp + rq - kj
    band = (dist >= 0) & (dist <= sp)
    kmin = jnp.where(n > 0, 0, sp)
    for i in range(nsub):
        q = q_ref[i * sp:(i + 1) * sp, :]
        if i == 0:
            kprev, vprev = kp_ref[...], vp_ref[...]
            mask = band & (kj >= kmin)
        else:
            kprev, vprev = k_ref[(i - 1) * sp:i * sp, :], v_ref[(i - 1) * sp:i * sp, :]
            mask = band
        kk = jnp.concatenate([kprev, k_ref[i * sp:(i + 1) * sp, :]], axis=0)
        vv = jnp.concatenate([vprev, v_ref[i * sp:(i + 1) * sp, :]], axis=0)
        zero = jnp.zeros_like(q)
        qs = jnp.concatenate([jnp.where(low, q, zero), jnp.where(low, zero, q)], axis=0)
        s = jnp.where(mask, _dot_nt(qs, kk), -jnp.inf)
        m = jnp.max(s, axis=-1, keepdims=True)
        e = jnp.exp(s - m)
        den = jnp.sum(e, axis=-1, keepdims=True)
        o2 = _dot((e / den).astype(BF16), vv)
        lse = jnp.broadcast_to(m + jnp.log(den), (2 * sp, LANES))
        o_ref[i * sp:(i + 1) * sp, :] = jnp.where(low, o2[:sp], o2[sp:])
        l_ref[i * sp:(i + 1) * sp, :] = jnp.where(low, lse[:sp], lse[sp:])


def _dilated(q, k, v, group):
    b, m, width = q.shape
    dilation = width // LANES
    tb = min(512, m)
    nsub = tb // DIL_SPAN
    cur = pl.BlockSpec((None, tb, LANES), lambda i, r, n: (i, n, r))
    prev = pl.BlockSpec((None, DIL_SPAN, LANES), lambda i, r, n: (i, jnp.maximum(n * nsub - 1, 0), r))
    return pl.pallas_call(
        functools.partial(_dilated_kernel, nsub=nsub),
        grid=(b, dilation, m // tb),
        in_specs=[cur, cur, cur, prev, prev],
        out_specs=[cur, cur],
        out_shape=[jax.ShapeDtypeStruct((b, m, width), F32)] * 2,
        compiler_params=pltpu.CompilerParams(dimension_semantics=("arbitrary",) * 3),
        name=f"dilated_g{group}",
    )(q, k, v, k, v)


def _dsa_kernel(shift_ref, qit_ref, wit_ref, ki_ref, qbt_ref, kb_ref, vta_ref, o_ref,
                hi_ref, lo_ref, sel_ref, fmax_ref, fmin_ref, blo_ref, qic_ref, qbm_ref, m_ref, acc_ref,
                *, tq, ck, rb, topk, idx_bits):
    i = pl.program_id(1)
    nrows = (i + 1) * tq
    nchunk = lax.div(nrows + (ck - 1), ck)
    nblk = lax.div(nrows, rb)
    nblk_pad = nchunk * (ck // rb)

    rows = lax.broadcasted_iota(I32, (2 * HEAD_DIM, tq), 0)
    for j in range(DSA_Q_HEADS):
        g = j // DSA_GROUP
        q = qbt_ref[j * HEAD_DIM:(j + 1) * HEAD_DIM, :]
        q2 = jnp.concatenate([q, q], axis=0)
        own = (rows >= g * HEAD_DIM) & (rows < (g + 1) * HEAD_DIM)
        qbm_ref[:, j * tq:(j + 1) * tq] = jnp.where(own, q2, jnp.zeros_like(q2))
    for h in range(IDX_HEADS):
        qic_ref[:, h * tq:(h + 1) * tq] = qit_ref[h * HEAD_DIM:(h + 1) * HEAD_DIM, :]

    kpos = lax.broadcasted_iota(I32, (ck, tq), 0)
    qpos = i * tq + lax.broadcasted_iota(I32, (ck, tq), 1)

    def index_chunk(c, carry):
        off = pl.multiple_of(c * ck, ck)
        sc = _dot(ki_ref[pl.ds(off, ck), :], qic_ref[...])
        acc = None
        for h in range(IDX_HEADS):
            t = jnp.maximum(sc[:, h * tq:(h + 1) * tq], 0.0) * wit_ref[h:h + 1, :]
            acc = t if acc is None else acc + t
        acc = jnp.where(kpos + off <= qpos, acc, -jnp.inf)
        bits = lax.bitcast_convert_type(acc, I32)
        key = jnp.where(bits < 0, INT_MIN - bits, bits)
        hi_ref[pl.ds(off, ck), :] = lax.shift_right_arithmetic(key, 16).astype(I16)
        lo_ref[pl.ds(off, ck), :] = (key ^ 0x8000).astype(I16)
        return carry

    def for_each_chunk(fn):
        def pair(t, carry):
            fn(2 * t, carry)
            return fn(2 * t + 1, carry)
        lax.fori_loop(0, lax.div(nchunk, 2), pair, 0)

        @pl.when(lax.rem(nchunk, 2) == 1)
        def _():
            fn(nchunk - 1, 0)

    for_each_chunk(index_chunk)

    sub16 = lax.broadcasted_iota(I32, (16, tq), 0)

    def count(refs, pred, nb=nblk, emit=None):
        def body(r, accs):
            base = pl.multiple_of(r * rb, rb)
            blks = [ref[pl.ds(base, rb), :] for ref in refs]
            if emit is not None:
                emit(blks, base)
            accs = list(accs)
            for u in range(rb // 16):
                hit = pred([blk[u * 16:(u + 1) * 16, :] for blk in blks], base + u * 16)
                accs[u % N_ACC] = accs[u % N_ACC] + jnp.where(hit, jnp.int16(1), jnp.int16(0))
            return tuple(accs)
        accs = lax.fori_loop(0, nb, body, tuple(jnp.zeros((16, tq), I16) for _ in range(N_ACC)))
        tot = accs[0].astype(I32)
        for a in accs[1:]:
            tot = tot + a.astype(I32)
        return jnp.sum(tot, axis=0, keepdims=True)

    def rows16(v):
        return jnp.broadcast_to(v, (16, tq)).astype(I16)

    def bisect(ref, want):
        def step(it, u):
            uc = u | lax.shift_left(jnp.int32(1), 15 - it)
            cand = rows16(uc - 32768)
            cnt = count([ref], lambda b, p0: b[0] >= cand)
            return jnp.where(cnt >= want, uc, u)
        return lax.fori_loop(0, 16, step, jnp.zeros((1, tq), I32)) - 32768

    a32 = jnp.maximum(bisect(hi_ref, topk), HI_NEG_INF)
    a16 = rows16(a32)
    a_blk = jnp.broadcast_to(a32, (rb, tq)).astype(I16)

    fmax_ref[...] = jnp.full(fmax_ref.shape, I16_MIN, I16)
    fmin_ref[...] = jnp.full(fmin_ref.shape, I16_MIN, I16)

    def fold_block(r, carry):
        accs, most = carry
        base = pl.multiple_of(r * rb, rb)
        hi, lo = hi_ref[pl.ds(base, rb), :], lo_ref[pl.ds(base, rb), :]
        top = jnp.full((16, tq), I16_MIN, I16)
        low = jnp.full((16, tq), I16_MAX, I16)
        members = jnp.zeros((16, tq), I16)
        accs = list(accs)
        for u in range(rb // 16):
            h, l = hi[u * 16:(u + 1) * 16, :], lo[u * 16:(u + 1) * 16, :]
            member = h == a16
            up, down = jnp.where(member, l, jnp.int16(I16_MIN)), jnp.where(member, l, jnp.int16(I16_MAX))
            top = jnp.where(up > top, up, top)
            low = jnp.where(down < low, down, low)
            members = members + jnp.where(member, jnp.int16(1), jnp.int16(0))
            accs[u % N_ACC] = accs[u % N_ACC] + jnp.where(h > a16, jnp.int16(1), jnp.int16(0))
        row = pl.multiple_of(r * 16, 16)
        fmax_ref[pl.ds(row, 16), :] = top
        fmin_ref[pl.ds(row, 16), :] = jnp.where(members >= 2, low, jnp.int16(I16_MIN))
        return tuple(accs), jnp.where(members > most, members, most)

    accs, most = lax.fori_loop(0, nblk, fold_block, (tuple(jnp.zeros((16, tq), I16) for _ in range(N_ACC)),
                                                     jnp.zeros((16, tq), I16)))
    n_hi = accs[0].astype(I32)
    for a in accs[1:]:
        n_hi = n_hi + a.astype(I32)
    n_hi = jnp.sum(n_hi, axis=0, keepdims=True)
    want_lo = topk - n_hi

    def folded_step(it, u):
        uc = u | lax.shift_left(jnp.int32(1), 15 - it)
        cand = rows16(uc - 32768)
        accs = [jnp.zeros((16, tq), I16) for _ in range(N_ACC)]
        for r in range(fmax_ref.shape[0] // 16):
            for k, ref in enumerate((fmax_ref, fmin_ref)):
                hit = ref[r * 16:(r + 1) * 16, :] >= cand
                accs[(2 * r + k) % N_ACC] = accs[(2 * r + k) % N_ACC] + jnp.where(hit, jnp.int16(1), jnp.int16(0))
        tot = accs[0].astype(I32)
        for a in accs[1:]:
            tot = tot + a.astype(I32)
        return jnp.where(jnp.sum(tot, axis=0, keepdims=True) >= want_lo, uc, u)

    blo_ref[...] = jnp.broadcast_to(lax.fori_loop(0, 16, folded_step, jnp.zeros((1, tq), I32)) - 32768, blo_ref.shape)

    @pl.when(jnp.max(most.astype(I32)) > FOLD_MAX_MEMBERS)
    def _():
        def bucket(r, carry):
            base = pl.multiple_of(r * rb, rb)
            sel_ref[pl.ds(base, rb), :] = jnp.where(hi_ref[pl.ds(base, rb), :] == a_blk, lo_ref[pl.ds(base, rb), :],
                                                    jnp.int16(I16_MIN))
            return carry
        lax.fori_loop(0, nblk, bucket, 0)
        blo_ref[...] = jnp.broadcast_to(bisect(sel_ref, want_lo), blo_ref.shape)

    b32 = blo_ref[0:1, :]
    b32 = jnp.where(a32 == HI_NEG_INF, jnp.maximum(b32, I16_MIN + 1), b32)
    b16 = rows16(b32)
    b_blk = jnp.broadcast_to(b32, (rb, tq)).astype(I16)

    def emit_selection(blks, base):
        sel = (blks[0] > a_blk) | ((blks[0] == a_blk) & (blks[1] >= b_blk))
        sel_ref[pl.ds(base, rb), :] = jnp.where(sel, jnp.int16(BF16_ONE_BITS), jnp.int16(0))

    n_ge = count([hi_ref, lo_ref], lambda b, p0: (b[0] > a16) | ((b[0] == a16) & (b[1] >= b16)),
                 nb=nblk_pad, emit=emit_selection)

    need = n_ge > topk

    @pl.when(jnp.max(jnp.where(need, 1, 0)) > 0)
    def _():
        n_gt = n_hi + count([hi_ref, lo_ref], lambda b, p0: (b[0] == a16) & (b[1] > b16))
        want = jnp.where(need, topk - n_gt, jnp.int32(2 ** 30))

        def pos_step(it, x):
            xc = x | lax.shift_left(jnp.int32(1), idx_bits - 1 - it)
            xc16 = rows16(xc)
            below = count([hi_ref, lo_ref], lambda b, p0: (b[0] == a16) & (b[1] == b16)
                          & ((sub16 + p0).astype(I16) < xc16))
            return jnp.where(below < want, xc, x)

        x = lax.fori_loop(0, idx_bits, pos_step, jnp.zeros((1, tq), I32))

        def demote(r, carry):
            base = pl.multiple_of(r * rb, rb)
            a_b = jnp.broadcast_to(a32, (rb, tq)).astype(I16)
            b_b = jnp.broadcast_to(b32, (rb, tq)).astype(I16)
            pos = (base + lax.broadcasted_iota(I32, (rb, tq), 0)).astype(I16)
            drop = ((hi_ref[pl.ds(base, rb), :] == a_b) & (lo_ref[pl.ds(base, rb), :] == b_b)
                    & (pos > jnp.broadcast_to(x, (rb, tq)).astype(I16)))
            sel_ref[pl.ds(base, rb), :] = jnp.where(drop, jnp.int16(0), sel_ref[pl.ds(base, rb), :])
            return carry

        lax.fori_loop(0, nblk, demote, 0)

    shift = shift_ref[0]

    @pl.when(shift < SAFE_SHIFT_LOG2)
    def _():
        acc_ref[...] = jnp.zeros(acc_ref.shape, F32)

        def attend_chunk(c, carry):
            off = pl.multiple_of(c * ck, ck)
            kch = kb_ref[pl.ds(off, ck), :]
            msk = lax.bitcast_convert_type(sel_ref[pl.ds(off, ck), :], BF16)
            for g in range(DSA_KV_HEADS):
                sc = _dot(kch, qbm_ref[:, g * DSA_GROUP * tq:(g + 1) * DSA_GROUP * tq])
                p = jnp.exp2(sc - shift).astype(BF16) * _tile_lanes(msk, DSA_GROUP)
                acc_ref[g] += _dot(vta_ref[g * V_AUG:(g + 1) * V_AUG, pl.ds(off, ck)], p)
            return carry

        for_each_chunk(attend_chunk)

    @pl.when(shift >= SAFE_SHIFT_LOG2)
    def _():
        m_ref[...] = jnp.full(m_ref.shape, NEG_BIG, F32)
        acc_ref[...] = jnp.zeros(acc_ref.shape, F32)

        def attend_chunk(c, carry):
            off = pl.multiple_of(c * ck, ck)
            kch = kb_ref[pl.ds(off, ck), :]
            sel = sel_ref[pl.ds(off, ck), :].astype(I32) != 0
            for j in range(DSA_Q_HEADS):
                g = j // DSA_GROUP
                cols = slice((j % DSA_GROUP) * tq, (j % DSA_GROUP + 1) * tq)
                sc = jnp.where(sel, _dot(kch, qbm_ref[:, j * tq:(j + 1) * tq]), NEG_BIG)
                m_prev = m_ref[j]
                m_new = jnp.maximum(m_prev, jnp.max(sc, axis=0, keepdims=True))
                p = jnp.where(sel, jnp.exp2(sc - m_new), 0.0).astype(BF16)
                acc_ref[g, :, cols] = jnp.exp2(m_prev - m_new) * acc_ref[g, :, cols] + _dot(
                    vta_ref[g * V_AUG:(g + 1) * V_AUG, pl.ds(off, ck)], p)
                m_ref[j] = m_new
            return carry

        lax.fori_loop(0, nchunk, attend_chunk, 0)

    for pair in range(DSA_Q_HEADS // 2):
        halves = []
        for j in (2 * pair, 2 * pair + 1):
            a = acc_ref[j // DSA_GROUP, :, (j % DSA_GROUP) * tq:(j % DSA_GROUP + 1) * tq]
            halves.append(a[:HEAD_DIM] / a[HEAD_DIM:HEAD_DIM + 1])
        o_ref[:, pair * LANES:(pair + 1) * LANES] = jnp.concatenate(halves, axis=0).T.astype(BF16)


def _dsa(shift, qit, wit, ki, qbt, kb, vta):
    b, _, s = qbt.shape
    tq = min(256, s)
    ck = min(512, s)
    rb = 256
    topk = min(DSA_TOPK_MAX, s // 4)
    idx_bits = max(1, (s - 1).bit_length()) + 1
    qt = lambda w: pl.BlockSpec((None, w, tq), lambda i, j: (i, 0, j))
    seq = lambda w: pl.BlockSpec((None, s, w), lambda i, j: (i, 0, 0))
    return pl.pallas_call(
        functools.partial(_dsa_kernel, tq=tq, ck=ck, rb=rb, topk=topk, idx_bits=idx_bits),
        grid=(b, s // tq),
        in_specs=[pl.BlockSpec(memory_space=pltpu.SMEM), qt(D_QI), qt(IDX_HEADS), seq(HEAD_DIM), qt(D_QB), seq(D_KVB),
                  pl.BlockSpec((None, DSA_KV_HEADS * V_AUG, s), lambda i, j: (i, 0, 0))],
        out_specs=pl.BlockSpec((None, tq, D_QB), lambda i, j: (i, j, 0)),
        out_shape=jax.ShapeDtypeStruct((b, s, D_QB), BF16),
        scratch_shapes=[
            pltpu.VMEM((s, tq), I16),
            pltpu.VMEM((s, tq), I16),
            pltpu.VMEM((s, tq), I16),
            pltpu.VMEM((s // rb * 16, tq), I16),
            pltpu.VMEM((s // rb * 16, tq), I16),
            pltpu.VMEM((8, tq), I32),
            pltpu.VMEM((HEAD_DIM, IDX_HEADS * tq), BF16),
            pltpu.VMEM((2 * HEAD_DIM, DSA_Q_HEADS * tq), BF16),
            pltpu.VMEM((DSA_Q_HEADS, 1, tq), F32),
            pltpu.VMEM((DSA_KV_HEADS, V_AUG, DSA_GROUP * tq), F32),
        ],
        compiler_params=pltpu.CompilerParams(dimension_semantics=("arbitrary", "arbitrary"),
                                             vmem_limit_bytes=VMEM_LIMIT),
        name="dsa",
    )(shift, qit, wit, ki, qbt, kb, vta)


def _merge_kernel(x_ref, g_ref, wg_ref, o0_ref, l0_ref, o1_ref, l1_ref, o2_ref, l2_ref, ob_ref, qc_ref,
                  km_ref, vm_ref, wa_ref, wb_ref, wc_ref, wo_ref, out_ref, *stage_refs):
    x = x_ref[...]
    d = x.shape[1]
    tm = x.shape[0]
    h = _rms_rows(x, g_ref[...]).astype(BF16)

    def token_major(ref, stage_ref):
        dil = ref.shape[1] // LANES
        if dil == 1:
            return ref[...]
        for r in range(dil):
            stage_ref[pl.ds(r, tm // dil, stride=dil), :] = ref[:, r * LANES:(r + 1) * LANES]
        return stage_ref[...]

    def gate(k):
        z = _dot(h, wg_ref[:, k * d:(k + 1) * d])
        return 1.0 / (1.0 + jnp.exp(-z))

    l0, l1, l2 = l0_ref[...], token_major(l1_ref, stage_refs[0]), token_major(l2_ref, stage_refs[1])
    o0, o1, o2 = o0_ref[...], token_major(o1_ref, stage_refs[2]), token_major(o2_ref, stage_refs[3])
    mx = jnp.maximum(jnp.maximum(l0, l1), l2)
    e0, e1, e2 = jnp.exp(l0 - mx), jnp.exp(l1 - mx), jnp.exp(l2 - mx)
    oa = (e0 * o0 + e1 * o1 + e2 * o2) / (e0 + e1 + e2)
    merged = gate(0) * _dot(oa.astype(BF16), wa_ref[...])

    merged = merged + gate(1) * _dot(ob_ref[...], wb_ref[...])

    low = _low_half((tm, LANES))
    cols = []
    for mcol in range(MEM_HEADS // 2):
        q = qc_ref[:, mcol * LANES:(mcol + 1) * LANES]
        km = km_ref[:, mcol * LANES:(mcol + 1) * LANES]
        vm = vm_ref[:, mcol * LANES:(mcol + 1) * LANES]
        zero = jnp.zeros_like(q)
        outs = []
        for qh in (jnp.where(low, q, zero), jnp.where(low, zero, q)):
            s = _dot_nt(qh, km)
            e = jnp.exp(s - jnp.max(s, axis=-1, keepdims=True))
            p = e / jnp.sum(e, axis=-1, keepdims=True)
            outs.append(_dot(p.astype(BF16), vm))
        cols.append(jnp.where(low, outs[0], outs[1]))
    oc = jnp.concatenate(cols, axis=1)
    merged = merged + gate(2) * _dot(oc.astype(BF16), wc_ref[...])

    out_ref[...] = x + _dot(merged.astype(BF16), wo_ref[...])


def _merge(x, g_mix, w_gate, dil, ob, qc, km, vm, w_a, w_b, w_c, w_o, tm):
    b, s, d = x.shape
    mlen = km.shape[1]
    tok = lambda w: pl.BlockSpec((None, tm, w), lambda i, j: (i, j, 0))
    full = lambda *shape: pl.BlockSpec(shape, lambda i, j: (0,) * len(shape))
    memb = pl.BlockSpec((None, mlen, D_QC), lambda i, j: (i, 0, 0))
    dil_args = [t for pair in dil for t in pair]
    dil_specs = [pl.BlockSpec((None, tm // dl, dl * LANES), lambda i, j: (i, j, 0))
                 for _, dl in DIL_GROUPS for _ in range(2)]
    return pl.pallas_call(
        _merge_kernel,
        grid=(b, s // tm),
        in_specs=[tok(d), full(1, d), full(d, 3 * d)] + dil_specs + [tok(D_QB), tok(D_QC), memb, memb,
                  full(LANES, d), full(D_QB, d), full(D_QC, d), full(d, d)],
        out_specs=tok(d),
        out_shape=jax.ShapeDtypeStruct((b, s, d), F32),
        scratch_shapes=[pltpu.VMEM((tm, LANES), F32)] * 4,
        compiler_params=pltpu.CompilerParams(dimension_semantics=("arbitrary", "arbitrary"),
                                             vmem_limit_bytes=VMEM_LIMIT),
        name="merge",
    )(x, g_mix, w_gate, *dil_args, ob, qc, km, vm, w_a, w_b, w_c, w_o)


def _mlp_kernel(x_ref, g_ref, w1_ref, w2_ref, out_ref, *, fchunk):
    x = x_ref[...]
    h = _rms_rows(x, g_ref[...]).astype(BF16)
    acc = x
    for c in range(w1_ref.shape[1] // fchunk):
        u = jnp.maximum(_dot(h, w1_ref[:, c * fchunk:(c + 1) * fchunk]), 0.0)
        acc = acc + _dot((u * u).astype(BF16), w2_ref[c * fchunk:(c + 1) * fchunk, :])
    out_ref[...] = acc


def _mlp(x, g_mlp, w_1, w_2, tm):
    b, s, d = x.shape
    f = w_1.shape[1]
    tok = pl.BlockSpec((None, tm, d), lambda i, j: (i, j, 0))
    full = lambda *shape: pl.BlockSpec(shape, lambda i, j: (0,) * len(shape))
    return pl.pallas_call(
        functools.partial(_mlp_kernel, fchunk=min(1024, f)),
        grid=(b, s // tm),
        in_specs=[tok, full(1, d), full(d, f), full(f, d)],
        out_specs=tok,
        out_shape=jax.ShapeDtypeStruct((b, s, d), F32),
        compiler_params=pltpu.CompilerParams(dimension_semantics=("arbitrary", "arbitrary"),
                                             vmem_limit_bytes=VMEM_LIMIT),
        name="mlp",
    )(x, g_mlp, w_1, w_2)


def _rotary_tables(positions):
    inv = jnp.power(jnp.float32(ROPE_THETA), -jnp.arange(ROT_HALF, dtype=F32) / ROT_HALF)
    ang = positions.astype(F32)[..., None] * inv
    cos, sin = jnp.cos(ang), jnp.sin(ang)
    rest = HEAD_DIM - 2 * ROT_HALF
    ones = jnp.ones(cos.shape[:-1] + (rest,), F32)
    zeros = jnp.zeros(cos.shape[:-1] + (rest,), F32)
    z8 = jnp.zeros_like(sin)
    head = lambda parts: jnp.tile(jnp.concatenate(parts, axis=-1), (1, 1, LANES // HEAD_DIM))
    return (head([cos, cos, ones]), head([-sin, z8, zeros]), head([z8, sin, zeros]),
            cos.transpose(0, 2, 1), sin.transpose(0, 2, 1))


def _block_diag_mean(width):
    r = jnp.arange(width) // HEAD_DIM
    return jnp.where(r[:, None] == r[None, :], 1.0 / HEAD_DIM, 0.0).astype(BF16)


def _layer(x, mem, tables, g_mix, g_mem, w_in, g_qa, g_ka, g_qb, g_kb, g_qc, g_kc,
           w_mem_kv, w_a, w_b, w_c, w_o, g_mlp, w_1, w_2):
    b, s, d = x.shape
    tm = min(256, s)
    tm_wide = min(512, s)
    cos, slo, shi, cos_t, sin_t = tables
    bd = _block_diag_mean(D_QA)
    scale = HEAD_DIM ** -0.5

    offs, acc = [], 0
    for w in (D_QA, D_QA, D_QA, D_QB, D_KVB, D_KVB, D_QI, HEAD_DIM, IDX_HEADS, D_QC):
        offs.append((acc, acc + w))
        acc += w
    seg = lambda k: w_in[:, offs[k][0]:offs[k][1]]
    w_std = jnp.concatenate([seg(0), seg(1), seg(2), seg(4), seg(7), seg(7), seg(9)], axis=1).astype(BF16)
    w_t = jnp.concatenate([seg(3), seg(6), seg(5), seg(8), jnp.zeros((d, _R_END - _R_WI - IDX_HEADS), w_in.dtype)],
                          axis=1).T.astype(BF16)
    w_gate = w_in[:, acc:].astype(BF16)

    tile6 = lambda g: jnp.tile(g, D_QA // HEAD_DIM)
    hg = jnp.stack([tile6(g_qa) * scale, tile6(g_ka), tile6(g_kb), tile6(g_qc) * scale,
                    jnp.zeros(D_QA), jnp.zeros(D_QA), jnp.zeros(D_QA), jnp.zeros(D_QA)]).astype(F32)
    gqt = jnp.broadcast_to((g_qb * (scale * LOG2E))[:, None], (HEAD_DIM, LANES)).astype(F32)
    shift = (HEAD_DIM * scale * LOG2E * 1.02) * jnp.max(jnp.abs(g_qb)) * jnp.max(jnp.abs(g_kb))
    shift = jnp.reshape(shift, (1,)).astype(F32)

    km, vm = _memkv(mem, g_mem[None, :], w_mem_kv.astype(BF16), bd[:D_QC, :D_QC],
                    jnp.tile(g_kc, MEM_HEADS)[None, :])
    (q0, q1, q2, k0, k1, k2, v0, v1, v2, kb, ki, qc, qbt, qit, vta, wit) = _inproj(
        x, g_mix[None, :], w_std, w_t, cos, slo, shi, cos_t, sin_t, bd, hg, gqt, tm_wide)
    dil = [_dilated(q, k, v, g) for g, (q, k, v) in enumerate(((q0, k0, v0), (q1, k1, v1), (q2, k2, v2)))]
    ob = _dsa(shift, qit, wit, ki, qbt, kb, vta)
    x = _merge(x, g_mix[None, :], w_gate, dil, ob, qc, km, vm, w_a.astype(BF16), w_b.astype(BF16),
               w_c.astype(BF16), w_o.astype(BF16), tm_wide)
    return _mlp(x, g_mlp[None, :], w_1.astype(BF16), w_2.astype(BF16), tm)


def kernel(x, mem, positions, g_mix, g_mem, w_in, g_qa, g_ka, g_qb, g_kb, g_qc, g_kc, w_mem_kv, w_a, w_b, w_c, w_o, g_mlp, w_1, w_2):
    tables = _rotary_tables(positions)
    for i in range(g_mix.shape[0]):
        x = _layer(x, mem, tables, g_mix[i], g_mem[i], w_in[i], g_qa[i], g_ka[i], g_qb[i], g_kb[i], g_qc[i],
                   g_kc[i], w_mem_kv[i], w_a[i], w_b[i], w_c[i], w_o[i], g_mlp[i], w_1[i], w_2[i])
    return x
```

```python
import functools
import math

import jax
import jax.numpy as jnp
from jax import lax
from jax.experimental import pallas as pl
from jax.experimental.pallas import tpu as pltpu

F32 = jnp.float32
BF16 = jnp.bfloat16
I32 = jnp.int32
I16 = jnp.int16

LANES = 128
HEAD_DIM = 64
ROT_HALF = 8
ROPE_THETA = 500000.0
EPS = 1e-6
DIL_GROUPS = ((128, 1), (512, 4), (2048, 16))
DIL_SPAN = 128
N_DIL_HEADS = 6
DSA_Q_HEADS = 6
DSA_KV_HEADS = 2
DSA_GROUP = DSA_Q_HEADS // DSA_KV_HEADS
DSA_TOPK_MAX = 256
IDX_HEADS = 8
MEM_HEADS = 4
D_QA = N_DIL_HEADS * HEAD_DIM
D_QB = DSA_Q_HEADS * HEAD_DIM
D_KVB = DSA_KV_HEADS * HEAD_DIM
D_QI = IDX_HEADS * HEAD_DIM
D_QC = MEM_HEADS * HEAD_DIM
V_AUG = 80
LOG2E = math.log2(math.e)
NEG_BIG = -1e30
INT_MIN = -2147483648
I16_MIN = -32768
I16_MAX = 32767
FOLD_MAX_MEMBERS = 2
BF16_ONE_BITS = 0x3F80
HI_NEG_INF = -32640
SAFE_SHIFT_LOG2 = 55.0
N_ACC = 4
VMEM_LIMIT = 56 * 1024 * 1024

_CONTRACT_LAST = (((1,), (1,)), ((), ()))


def _dot(a, b):
    return jnp.dot(a, b, preferred_element_type=F32)


def _dot_nt(a, b):
    return lax.dot_general(a, b, _CONTRACT_LAST, preferred_element_type=F32)


def _tile_lanes(a, reps):
    return a if reps == 1 else jnp.concatenate([a] * reps, axis=1)


def _low_half(shape):
    return (lax.broadcasted_iota(I32, shape, 1) % LANES) < HEAD_DIM


def _rms_rows(x, g):
    ms = jnp.mean(x * x, axis=-1, keepdims=True)
    return x * lax.rsqrt(ms + EPS) * g


def _norm_heads(p, bd, gain):
    sq = p * p
    hi = sq.astype(BF16)
    lo = (sq - hi.astype(F32)).astype(BF16)
    ms = _dot(hi, bd) + _dot(lo, bd)
    return p * lax.rsqrt(ms + EPS) * gain


def _rotary(y, cos, sin_lo, sin_hi):
    w = y.shape[1]
    reps = w // LANES
    c = _tile_lanes(cos, reps)
    a = _tile_lanes(sin_lo, reps)
    b = _tile_lanes(sin_hi, reps)
    return y * c + pltpu.roll(y, w - ROT_HALF, 1) * a + pltpu.roll(y, ROT_HALF, 1) * b


def _rotary_t(blk, cos_t, sin_t):
    x1, x2 = blk[:ROT_HALF], blk[ROT_HALF:2 * ROT_HALF]
    return jnp.concatenate([x1 * cos_t - x2 * sin_t, x2 * cos_t + x1 * sin_t, blk[2 * ROT_HALF:]], axis=0)


def _memkv_kernel(mem_ref, g_ref, w_ref, bd_ref, gk_ref, k_ref, v_ref):
    h = _rms_rows(mem_ref[...], g_ref[...]).astype(BF16)
    kv = _dot(h, w_ref[...])
    k_ref[...] = _norm_heads(kv[:, :D_QC], bd_ref[...], gk_ref[...]).astype(BF16)
    v_ref[...] = kv[:, D_QC:].astype(BF16)


def _memkv(mem, g_mem, w_mem_kv, bd, gk):
    b, m, d = mem.shape
    full = lambda *shape: pl.BlockSpec(shape, lambda i: (0,) * len(shape))
    return pl.pallas_call(
        _memkv_kernel,
        grid=(b,),
        in_specs=[pl.BlockSpec((None, m, d), lambda i: (i, 0, 0)), full(1, d), full(d, 2 * D_QC),
                  full(D_QC, D_QC), full(1, D_QC)],
        out_specs=[pl.BlockSpec((None, m, D_QC), lambda i: (i, 0, 0))] * 2,
        out_shape=[jax.ShapeDtypeStruct((b, m, D_QC), BF16)] * 2,
        name="memkv",
    )(mem, g_mem, w_mem_kv, bd, gk)


_C_QA, _C_KA, _C_VA, _C_KB, _C_KI, _C_QC, _C_END = 0, 384, 768, 1152, 1280, 1408, 1664
_R_QB, _R_QI, _R_VB, _R_WI, _R_END = 0, 384, 896, 1024, 1040


def _inproj_kernel(x_ref, g_ref, w_ref, wt_ref, cos_ref, slo_ref, shi_ref, cost_ref, sint_ref, bd_ref, hg_ref, gqt_ref,
                   q0_ref, q1_ref, q2_ref, k0_ref, k1_ref, k2_ref, v0_ref, v1_ref, v2_ref,
                   kb_ref, ki_ref, qc_ref, qbt_ref, qit_ref, vta_ref, wit_ref, dil_ref):
    tm = x_ref.shape[0]
    h = _rms_rows(x_ref[...], g_ref[...]).astype(BF16)
    cos, slo, shi = cos_ref[...], slo_ref[...], shi_ref[...]
    bd = bd_ref[...]

    def proj(a, b):
        return _dot(h, w_ref[:, a:b])

    def norm_rot(a, b, gain_row):
        w = b - a
        y = _norm_heads(proj(a, b), bd[:w, :w], hg_ref[gain_row:gain_row + 1, :w])
        return _rotary(y, cos, slo, shi)

    def store_dilated(y, out_refs):
        for g, ((_, dil), out_ref) in enumerate(zip(DIL_GROUPS, out_refs)):
            yg = y[:, g * LANES:(g + 1) * LANES]
            if dil == 1:
                out_ref[...] = yg.astype(BF16)
            else:
                dil_ref[...] = yg
                for r in range(dil):
                    out_ref[:, r * LANES:(r + 1) * LANES] = dil_ref[pl.ds(r, tm // dil, stride=dil), :].astype(BF16)

    store_dilated(norm_rot(_C_QA, _C_KA, 0), (q0_ref, q1_ref, q2_ref))
    store_dilated(norm_rot(_C_KA, _C_VA, 1), (k0_ref, k1_ref, k2_ref))
    store_dilated(proj(_C_VA, _C_KB), (v0_ref, v1_ref, v2_ref))
    kb_ref[...] = norm_rot(_C_KB, _C_KI, 2).astype(BF16)
    ki_ref[...] = _rotary(proj(_C_KI, _C_QC), cos, slo, shi)[:, :HEAD_DIM].astype(BF16)
    qc_ref[...] = _norm_heads(proj(_C_QC, _C_END), bd[:D_QC, :D_QC], hg_ref[3:4, :D_QC]).astype(BF16)

    pt = _dot_nt(wt_ref[...], h)
    cos_t, sin_t = cost_ref[...], sint_ref[...]
    gq = _tile_lanes(gqt_ref[...], tm // LANES)
    for hd in range(DSA_Q_HEADS):
        blk = pt[_R_QB + hd * HEAD_DIM:_R_QB + (hd + 1) * HEAD_DIM]
        ms = jnp.mean(blk * blk, axis=0, keepdims=True)
        qbt_ref[hd * HEAD_DIM:(hd + 1) * HEAD_DIM, :] = _rotary_t(blk * lax.rsqrt(ms + EPS) * gq, cos_t, sin_t).astype(BF16)
    for hd in range(IDX_HEADS):
        blk = pt[_R_QI + hd * HEAD_DIM:_R_QI + (hd + 1) * HEAD_DIM]
        qit_ref[hd * HEAD_DIM:(hd + 1) * HEAD_DIM, :] = _rotary_t(blk, cos_t, sin_t).astype(BF16)
    vt = pt[_R_VB:_R_WI].astype(BF16)
    pad = jnp.where(lax.broadcasted_iota(I32, (V_AUG - HEAD_DIM, tm), 0) == 0, 1.0, 0.0).astype(BF16)
    vta_ref[...] = jnp.concatenate([vt[:HEAD_DIM], pad, vt[HEAD_DIM:], pad], axis=0)
    wit_ref[...] = pt[_R_WI:_R_WI + IDX_HEADS] * ((IDX_HEADS ** -0.5) * (HEAD_DIM ** -0.5))


def _inproj(x, g_mix, w_std, w_t, cos, slo, shi, cos_t, sin_t, bd, hg, gqt, tm):
    b, s, d = x.shape
    tok = lambda w: pl.BlockSpec((None, tm, w), lambda i, j: (i, j, 0))
    tok_t = lambda r: pl.BlockSpec((None, r, tm), lambda i, j: (i, 0, j))
    full = lambda *shape: pl.BlockSpec(shape, lambda i, j: (0,) * len(shape))
    dils = [dl for _, dl in DIL_GROUPS] * 3
    std = ((D_KVB, BF16), (HEAD_DIM, BF16), (D_QC, BF16))
    tr = ((D_QB, BF16), (D_QI, BF16), (DSA_KV_HEADS * V_AUG, BF16), (IDX_HEADS, F32))
    return pl.pallas_call(
        _inproj_kernel,
        grid=(b, s // tm),
        in_specs=[tok(d), full(1, d), full(d, _C_END), full(_R_END, d), tok(LANES), tok(LANES), tok(LANES),
                  tok_t(ROT_HALF), tok_t(ROT_HALF), full(D_QA, D_QA), full(8, D_QA), full(HEAD_DIM, LANES)],
        out_specs=[pl.BlockSpec((None, tm // dl, dl * LANES), lambda i, j: (i, j, 0)) for dl in dils]
        + [tok(w) for w, _ in std] + [tok_t(r) for r, _ in tr],
        out_shape=[jax.ShapeDtypeStruct((b, s // dl, dl * LANES), BF16) for dl in dils]
        + [jax.ShapeDtypeStruct((b, s, w), t) for w, t in std]
        + [jax.ShapeDtypeStruct((b, r, s), t) for r, t in tr],
        scratch_shapes=[pltpu.VMEM((tm, LANES), F32)],
        compiler_params=pltpu.CompilerParams(dimension_semantics=("arbitrary", "arbitrary"),
                                             vmem_limit_bytes=VMEM_LIMIT),
        name="inproj",
    )(x, g_mix, w_std, w_t, cos, slo, shi, cos_t, sin_t, bd, hg, gqt)


def _dilated_kernel(q_ref, k_ref, v_ref, kp_ref, vp_ref, o_ref, l_ref, *, nsub):
    n = pl.program_id(2)
    sp = DIL_SPAN
    low = _low_half((sp, LANES))
    rq = lax.broadcasted_iota(I32, (2 * sp, 2 * sp), 0) % sp
    kj = lax.broadcasted_iota(I32, (2 * sp, 2 * sp), 1)
    dist = sp + rq - kj
    band = (dist >= 0) & (dist <= sp)
    kmin = jnp.where(n > 0, 0, sp)
    for i in range(nsub):
        q = q_ref[i * sp:(i + 1) * sp, :]
        if i == 0:
            kprev, vprev = kp_ref[...], vp_ref[...]
            mask = band & (kj >= kmin)
        else:
            kprev, vprev = k_ref[(i - 1) * sp:i * sp, :], v_ref[(i - 1) * sp:i * sp, :]
            mask = band
        kk = jnp.concatenate([kprev, k_ref[i * sp:(i + 1) * sp, :]], axis=0)
        vv = jnp.concatenate([vprev, v_ref[i * sp:(i + 1) * sp, :]], axis=0)
        zero = jnp.zeros_like(q)
        qs = jnp.concatenate([jnp.where(low, q, zero), jnp.where(low, zero, q)], axis=0)
        s = jnp.where(mask, _dot_nt(qs, kk), -jnp.inf)
        m = jnp.max(s, axis=-1, keepdims=True)
        e = jnp.exp(s - m)
        den = jnp.sum(e, axis=-1, keepdims=True)
        o2 = _dot((e / den).astype(BF16), vv)
        lse = jnp.broadcast_to(m + jnp.log(den), (2 * sp, LANES))
        o_ref[i * sp:(i + 1) * sp, :] = jnp.where(low, o2[:sp], o2[sp:])
        l_ref[i * sp:(i + 1) * sp, :] = jnp.where(low, lse[:sp], lse[sp:])


def _dilated(q, k, v, group):
    b, m, width = q.shape
    dilation = width // LANES
    tb = min(512, m)
    nsub = tb // DIL_SPAN
    cur = pl.BlockSpec((None, tb, LANES), lambda i, r, n: (i, n, r))
    prev = pl.BlockSpec((None, DIL_SPAN, LANES), lambda i, r, n: (i, jnp.maximum(n * nsub - 1, 0), r))
    return pl.pallas_call(
        functools.partial(_dilated_kernel, nsub=nsub),
        grid=(b, dilation, m // tb),
        in_specs=[cur, cur, cur, prev, prev],
        out_specs=[cur, cur],
        out_shape=[jax.ShapeDtypeStruct((b, m, width), F32)] * 2,
        compiler_params=pltpu.CompilerParams(dimension_semantics=("arbitrary",) * 3),
        name=f"dilated_g{group}",
    )(q, k, v, k, v)


def _dsa_kernel(shift_ref, qit_ref, wit_ref, ki_ref, qbt_ref, kb_ref, vta_ref, o_ref,
                hi_ref, lo_ref, sel_ref, fmax_ref, fmin_ref, blo_ref, tcnt_ref, tsel_ref, qic_ref, qbm_ref, m_ref, acc_ref,
                *, tq, ck, rb, topk):
    i = pl.program_id(1)
    nrows = (i + 1) * tq
    nchunk = lax.div(nrows + (ck - 1), ck)
    nblk = lax.div(nrows, rb)
    nblk_pad = nchunk * (ck // rb)

    rows = lax.broadcasted_iota(I32, (2 * HEAD_DIM, tq), 0)
    for j in range(DSA_Q_HEADS):
        g = j // DSA_GROUP
        q = qbt_ref[j * HEAD_DIM:(j + 1) * HEAD_DIM, :]
        q2 = jnp.concatenate([q, q], axis=0)
        own = (rows >= g * HEAD_DIM) & (rows < (g + 1) * HEAD_DIM)
        qbm_ref[:, j * tq:(j + 1) * tq] = jnp.where(own, q2, jnp.zeros_like(q2))
    for h in range(IDX_HEADS):
        qic_ref[:, h * tq:(h + 1) * tq] = qit_ref[h * HEAD_DIM:(h + 1) * HEAD_DIM, :]

    kpos = lax.broadcasted_iota(I32, (ck, tq), 0)
    qpos = i * tq + lax.broadcasted_iota(I32, (ck, tq), 1)

    def index_chunk(c, carry):
        off = pl.multiple_of(c * ck, ck)
        sc = _dot(ki_ref[pl.ds(off, ck), :], qic_ref[...])
        acc = None
        for h in range(IDX_HEADS):
            t = jnp.maximum(sc[:, h * tq:(h + 1) * tq], 0.0) * wit_ref[h:h + 1, :]
            acc = t if acc is None else acc + t
        acc = jnp.where(kpos + off <= qpos, acc, -jnp.inf)
        bits = lax.bitcast_convert_type(acc, I32)
        key = jnp.where(bits < 0, INT_MIN - bits, bits)
        hi_ref[pl.ds(off, ck), :] = lax.shift_right_arithmetic(key, 16).astype(I16)
        lo_ref[pl.ds(off, ck), :] = (key ^ 0x8000).astype(I16)
        return carry

    def for_each_chunk(fn):
        def pair(t, carry):
            fn(2 * t, carry)
            return fn(2 * t + 1, carry)
        lax.fori_loop(0, lax.div(nchunk, 2), pair, 0)

        @pl.when(lax.rem(nchunk, 2) == 1)
        def _():
            fn(nchunk - 1, 0)

    for_each_chunk(index_chunk)


    def count(refs, pred, nb=nblk, emit=None):
        def body(r, accs):
            base = pl.multiple_of(r * rb, rb)
            blks = [ref[pl.ds(base, rb), :] for ref in refs]
            if emit is not None:
                emit(blks, base)
            accs = list(accs)
            for u in range(rb // 16):
                hit = pred([blk[u * 16:(u + 1) * 16, :] for blk in blks], base + u * 16)
                accs[u % N_ACC] = accs[u % N_ACC] + jnp.where(hit, jnp.int16(1), jnp.int16(0))
            return tuple(accs)
        accs = lax.fori_loop(0, nb, body, tuple(jnp.zeros((16, tq), I16) for _ in range(N_ACC)))
        tot = accs[0].astype(I32)
        for a in accs[1:]:
            tot = tot + a.astype(I32)
        return jnp.sum(tot, axis=0, keepdims=True)

    def rows16(v):
        return jnp.broadcast_to(v, (16, tq)).astype(I16)

    def bisect(ref, want):
        def step(it, u):
            uc = u | lax.shift_left(jnp.int32(1), 15 - it)
            cand = rows16(uc - 32768)
            cnt = count([ref], lambda b, p0: b[0] >= cand)
            return jnp.where(cnt >= want, uc, u)
        return lax.fori_loop(0, 16, step, jnp.zeros((1, tq), I32)) - 32768

    a32 = jnp.maximum(bisect(hi_ref, topk), HI_NEG_INF)
    a16 = rows16(a32)
    a_blk = jnp.broadcast_to(a32, (rb, tq)).astype(I16)

    fmax_ref[...] = jnp.full(fmax_ref.shape, I16_MIN, I16)
    fmin_ref[...] = jnp.full(fmin_ref.shape, I16_MIN, I16)

    def fold_block(r, carry):
        accs, most = carry
        base = pl.multiple_of(r * rb, rb)
        hi, lo = hi_ref[pl.ds(base, rb), :], lo_ref[pl.ds(base, rb), :]
        top = jnp.full((16, tq), I16_MIN, I16)
        low = jnp.full((16, tq), I16_MAX, I16)
        members = jnp.zeros((16, tq), I16)
        accs = list(accs)
        for u in range(rb // 16):
            h, l = hi[u * 16:(u + 1) * 16, :], lo[u * 16:(u + 1) * 16, :]
            member = h == a16
            up, down = jnp.where(member, l, jnp.int16(I16_MIN)), jnp.where(member, l, jnp.int16(I16_MAX))
            top = jnp.where(up > top, up, top)
            low = jnp.where(down < low, down, low)
            members = members + jnp.where(member, jnp.int16(1), jnp.int16(0))
            accs[u % N_ACC] = accs[u % N_ACC] + jnp.where(h > a16, jnp.int16(1), jnp.int16(0))
        row = pl.multiple_of(r * 16, 16)
        fmax_ref[pl.ds(row, 16), :] = top
        fmin_ref[pl.ds(row, 16), :] = jnp.where(members >= 2, low, jnp.int16(I16_MIN))
        return tuple(accs), jnp.where(members > most, members, most)

    accs, most = lax.fori_loop(0, nblk, fold_block, (tuple(jnp.zeros((16, tq), I16) for _ in range(N_ACC)),
                                                     jnp.zeros((16, tq), I16)))
    n_hi = accs[0].astype(I32)
    for a in accs[1:]:
        n_hi = n_hi + a.astype(I32)
    n_hi = jnp.sum(n_hi, axis=0, keepdims=True)
    want_lo = topk - n_hi

    def folded_step(it, u):
        uc = u | lax.shift_left(jnp.int32(1), 15 - it)
        cand = rows16(uc - 32768)
        accs = [jnp.zeros((16, tq), I16) for _ in range(N_ACC)]
        for r in range(fmax_ref.shape[0] // 16):
            for k, ref in enumerate((fmax_ref, fmin_ref)):
                hit = ref[r * 16:(r + 1) * 16, :] >= cand
                accs[(2 * r + k) % N_ACC] = accs[(2 * r + k) % N_ACC] + jnp.where(hit, jnp.int16(1), jnp.int16(0))
        tot = accs[0].astype(I32)
        for a in accs[1:]:
            tot = tot + a.astype(I32)
        return jnp.where(jnp.sum(tot, axis=0, keepdims=True) >= want_lo, uc, u)

    blo_ref[...] = jnp.broadcast_to(lax.fori_loop(0, 16, folded_step, jnp.zeros((1, tq), I32)) - 32768, blo_ref.shape)

    @pl.when(jnp.max(most.astype(I32)) > FOLD_MAX_MEMBERS)
    def _():
        def bucket(r, carry):
            base = pl.multiple_of(r * rb, rb)
            sel_ref[pl.ds(base, rb), :] = jnp.where(hi_ref[pl.ds(base, rb), :] == a_blk, lo_ref[pl.ds(base, rb), :],
                                                    jnp.int16(I16_MIN))
            return carry
        lax.fori_loop(0, nblk, bucket, 0)
        blo_ref[...] = jnp.broadcast_to(bisect(sel_ref, want_lo), blo_ref.shape)

    b32 = blo_ref[0:1, :]
    b32 = jnp.where(a32 == HI_NEG_INF, jnp.maximum(b32, I16_MIN + 1), b32)
    b16 = rows16(b32)
    b_blk = jnp.broadcast_to(b32, (rb, tq)).astype(I16)

    def emit_selection(blks, base):
        sel = (blks[0] > a_blk) | ((blks[0] == a_blk) & (blks[1] >= b_blk))
        sel_ref[pl.ds(base, rb), :] = jnp.where(sel, jnp.int16(BF16_ONE_BITS), jnp.int16(0))

    n_ge = count([hi_ref, lo_ref], lambda b, p0: (b[0] > a16) | ((b[0] == a16) & (b[1] >= b16)),
                 nb=nblk_pad, emit=emit_selection)

    need = n_ge > topk

    @pl.when(jnp.max(jnp.where(need, 1, 0)) > 0)
    def _():
        nb_max = tcnt_ref.shape[0]
        none = jnp.int32(2 ** 30)

        def tied(hi, lo, a, b):
            return (hi == a) & (lo == b)

        tcnt_ref[...] = jnp.zeros(tcnt_ref.shape, I32)

        def tie_block(r, carry):
            base = pl.multiple_of(r * rb, rb)
            hi, lo = hi_ref[pl.ds(base, rb), :], lo_ref[pl.ds(base, rb), :]
            acc = jnp.zeros((16, tq), I16)
            for u in range(rb // 16):
                acc = acc + jnp.where(tied(hi[u * 16:(u + 1) * 16, :], lo[u * 16:(u + 1) * 16, :], a16, b16),
                                      jnp.int16(1), jnp.int16(0))
            tcnt_ref[r] = jnp.broadcast_to(jnp.sum(acc.astype(I32), axis=0, keepdims=True), (8, tq))
            return carry

        lax.fori_loop(0, nblk, tie_block, 0)

        per_block = [tcnt_ref[r][0:1, :] for r in range(nb_max)]
        n_tied = per_block[0]
        for c in per_block[1:]:
            n_tied = n_tied + c
        want = jnp.where(need, topk - (n_ge - n_tied), none)
        last = jnp.full((1, tq), nb_max, I32)
        before = jnp.zeros((1, tq), I32)
        prefix = jnp.zeros((1, tq), I32)
        for r, c in enumerate(per_block):
            reached = (last == nb_max) & (prefix + c >= want)
            last = jnp.where(reached, r, last)
            before = jnp.where(reached, prefix, before)
            prefix = prefix + c
        want_here = want - before

        tsel_ref[...] = jnp.zeros(tsel_ref.shape, I16)

        def pick_block(r, carry):
            base = pl.multiple_of(r * rb, rb)
            here = jnp.broadcast_to(jnp.where(last == r, 1, 0), (rb, tq)).astype(I16) != 0
            hit = tied(hi_ref[pl.ds(base, rb), :], lo_ref[pl.ds(base, rb), :], a_blk, b_blk) & here
            tsel_ref[...] = tsel_ref[...] + jnp.where(hit, jnp.int16(1), jnp.int16(0))
            return carry

        lax.fori_loop(0, nblk, pick_block, 0)

        in_block = lax.broadcasted_iota(I32, (rb, tq), 0).astype(I16)
        block_bits = (rb - 1).bit_length()

        def row_step(it, e):
            ec = e | lax.shift_left(jnp.int32(1), block_bits - 1 - it)
            hit = (tsel_ref[...] != 0) & (in_block < jnp.broadcast_to(ec, (rb, tq)).astype(I16))
            ones = jnp.where(hit, jnp.int16(1), jnp.int16(0))
            acc = ones[0:16, :]
            for u in range(1, rb // 16):
                acc = acc + ones[u * 16:(u + 1) * 16, :]
            below = jnp.sum(acc.astype(I32), axis=0, keepdims=True)
            return jnp.where(below < want_here, ec, e)

        e_last = lax.fori_loop(0, block_bits, row_step, jnp.zeros((1, tq), I32))
        x = jnp.where(last < nb_max, jnp.minimum(last * rb + e_last, I16_MAX), I16_MAX)

        def demote(r, carry):
            base = pl.multiple_of(r * rb, rb)
            pos = (base + lax.broadcasted_iota(I32, (rb, tq), 0)).astype(I16)
            drop = (tied(hi_ref[pl.ds(base, rb), :], lo_ref[pl.ds(base, rb), :], a_blk, b_blk)
                    & (pos > jnp.broadcast_to(x, (rb, tq)).astype(I16)))
            sel_ref[pl.ds(base, rb), :] = jnp.where(drop, jnp.int16(0), sel_ref[pl.ds(base, rb), :])
            return carry

        lax.fori_loop(0, nblk, demote, 0)

    shift = shift_ref[0]

    @pl.when(shift < SAFE_SHIFT_LOG2)
    def _():
        acc_ref[...] = jnp.zeros(acc_ref.shape, F32)

        def attend_chunk(c, carry):
            off = pl.multiple_of(c * ck, ck)
            kch = kb_ref[pl.ds(off, ck), :]
            msk = lax.bitcast_convert_type(sel_ref[pl.ds(off, ck), :], BF16)
            for g in range(DSA_KV_HEADS):
                sc = _dot(kch, qbm_ref[:, g * DSA_GROUP * tq:(g + 1) * DSA_GROUP * tq])
                p = jnp.exp2(sc - shift).astype(BF16) * _tile_lanes(msk, DSA_GROUP)
                acc_ref[g] += _dot(vta_ref[g * V_AUG:(g + 1) * V_AUG, pl.ds(off, ck)], p)
            return carry

        for_each_chunk(attend_chunk)

    @pl.when(shift >= SAFE_SHIFT_LOG2)
    def _():
        m_ref[...] = jnp.full(m_ref.shape, NEG_BIG, F32)
        acc_ref[...] = jnp.zeros(acc_ref.shape, F32)

        def attend_chunk(c, carry):
            off = pl.multiple_of(c * ck, ck)
            kch = kb_ref[pl.ds(off, ck), :]
            sel = sel_ref[pl.ds(off, ck), :].astype(I32) != 0
            for j in range(DSA_Q_HEADS):
                g = j // DSA_GROUP
                cols = slice((j % DSA_GROUP) * tq, (j % DSA_GROUP + 1) * tq)
                sc = jnp.where(sel, _dot(kch, qbm_ref[:, j * tq:(j + 1) * tq]), NEG_BIG)
                m_prev = m_ref[j]
                m_new = jnp.maximum(m_prev, jnp.max(sc, axis=0, keepdims=True))
                p = jnp.where(sel, jnp.exp2(sc - m_new), 0.0).astype(BF16)
                acc_ref[g, :, cols] = jnp.exp2(m_prev - m_new) * acc_ref[g, :, cols] + _dot(
                    vta_ref[g * V_AUG:(g + 1) * V_AUG, pl.ds(off, ck)], p)
                m_ref[j] = m_new
            return carry

        lax.fori_loop(0, nchunk, attend_chunk, 0)

    for pair in range(DSA_Q_HEADS // 2):
        halves = []
        for j in (2 * pair, 2 * pair + 1):
            a = acc_ref[j // DSA_GROUP, :, (j % DSA_GROUP) * tq:(j % DSA_GROUP + 1) * tq]
            halves.append(a[:HEAD_DIM] / a[HEAD_DIM:HEAD_DIM + 1])
        o_ref[:, pair * LANES:(pair + 1) * LANES] = jnp.concatenate(halves, axis=0).T.astype(BF16)


def _dsa(shift, qit, wit, ki, qbt, kb, vta):
    b, _, s = qbt.shape
    tq = min(256, s)
    ck = min(512, s)
    rb = 256
    topk = min(DSA_TOPK_MAX, s // 4)
    qt = lambda w: pl.BlockSpec((None, w, tq), lambda i, j: (i, 0, j))
    seq = lambda w: pl.BlockSpec((None, s, w), lambda i, j: (i, 0, 0))
    return pl.pallas_call(
        functools.partial(_dsa_kernel, tq=tq, ck=ck, rb=rb, topk=topk),
        grid=(b, s // tq),
        in_specs=[pl.BlockSpec(memory_space=pltpu.SMEM), qt(D_QI), qt(IDX_HEADS), seq(HEAD_DIM), qt(D_QB), seq(D_KVB),
                  pl.BlockSpec((None, DSA_KV_HEADS * V_AUG, s), lambda i, j: (i, 0, 0))],
        out_specs=pl.BlockSpec((None, tq, D_QB), lambda i, j: (i, j, 0)),
        out_shape=jax.ShapeDtypeStruct((b, s, D_QB), BF16),
        scratch_shapes=[
            pltpu.VMEM((s, tq), I16),
            pltpu.VMEM((s, tq), I16),
            pltpu.VMEM((s, tq), I16),
            pltpu.VMEM((s // rb * 16, tq), I16),
            pltpu.VMEM((s // rb * 16, tq), I16),
            pltpu.VMEM((8, tq), I32),
            pltpu.VMEM((s // rb, 8, tq), I32),
            pltpu.VMEM((rb, tq), I16),
            pltpu.VMEM((HEAD_DIM, IDX_HEADS * tq), BF16),
            pltpu.VMEM((2 * HEAD_DIM, DSA_Q_HEADS * tq), BF16),
            pltpu.VMEM((DSA_Q_HEADS, 1, tq), F32),
            pltpu.VMEM((DSA_KV_HEADS, V_AUG, DSA_GROUP * tq), F32),
        ],
        compiler_params=pltpu.CompilerParams(dimension_semantics=("arbitrary", "arbitrary"),
                                             vmem_limit_bytes=VMEM_LIMIT),
        name="dsa",
    )(shift, qit, wit, ki, qbt, kb, vta)


def _merge_kernel(x_ref, g_ref, wg_ref, o0_ref, l0_ref, o1_ref, l1_ref, o2_ref, l2_ref, ob_ref, qc_ref,
                  km_ref, vm_ref, wa_ref, wb_ref, wc_ref, wo_ref, out_ref, *stage_refs):
    x = x_ref[...]
    d = x.shape[1]
    tm = x.shape[0]
    h = _rms_rows(x, g_ref[...]).astype(BF16)

    def token_major(ref, stage_ref):
        dil = ref.shape[1] // LANES
        if dil == 1:
            return ref[...]
        for r in range(dil):
            stage_ref[pl.ds(r, tm // dil, stride=dil), :] = ref[:, r * LANES:(r + 1) * LANES]
        return stage_ref[...]

    def gate(k):
        z = _dot(h, wg_ref[:, k * d:(k + 1) * d])
        return 1.0 / (1.0 + jnp.exp(-z))

    l0, l1, l2 = l0_ref[...], token_major(l1_ref, stage_refs[0]), token_major(l2_ref, stage_refs[1])
    o0, o1, o2 = o0_ref[...], token_major(o1_ref, stage_refs[2]), token_major(o2_ref, stage_refs[3])
    mx = jnp.maximum(jnp.maximum(l0, l1), l2)
    e0, e1, e2 = jnp.exp(l0 - mx), jnp.exp(l1 - mx), jnp.exp(l2 - mx)
    oa = (e0 * o0 + e1 * o1 + e2 * o2) / (e0 + e1 + e2)
    merged = gate(0) * _dot(oa.astype(BF16), wa_ref[...])

    merged = merged + gate(1) * _dot(ob_ref[...], wb_ref[...])

    low = _low_half((tm, LANES))
    cols = []
    for mcol in range(MEM_HEADS // 2):
        q = qc_ref[:, mcol * LANES:(mcol + 1) * LANES]
        km = km_ref[:, mcol * LANES:(mcol + 1) * LANES]
        vm = vm_ref[:, mcol * LANES:(mcol + 1) * LANES]
        zero = jnp.zeros_like(q)
        outs = []
        for qh in (jnp.where(low, q, zero), jnp.where(low, zero, q)):
            s = _dot_nt(qh, km)
            e = jnp.exp(s - jnp.max(s, axis=-1, keepdims=True))
            p = e / jnp.sum(e, axis=-1, keepdims=True)
            outs.append(_dot(p.astype(BF16), vm))
        cols.append(jnp.where(low, outs[0], outs[1]))
    oc = jnp.concatenate(cols, axis=1)
    merged = merged + gate(2) * _dot(oc.astype(BF16), wc_ref[...])

    out_ref[...] = x + _dot(merged.astype(BF16), wo_ref[...])


def _merge(x, g_mix, w_gate, dil, ob, qc, km, vm, w_a, w_b, w_c, w_o, tm):
    b, s, d = x.shape
    mlen = km.shape[1]
    tok = lambda w: pl.BlockSpec((None, tm, w), lambda i, j: (i, j, 0))
    full = lambda *shape: pl.BlockSpec(shape, lambda i, j: (0,) * len(shape))
    memb = pl.BlockSpec((None, mlen, D_QC), lambda i, j: (i, 0, 0))
    dil_args = [t for pair in dil for t in pair]
    dil_specs = [pl.BlockSpec((None, tm // dl, dl * LANES), lambda i, j: (i, j, 0))
                 for _, dl in DIL_GROUPS for _ in range(2)]
    return pl.pallas_call(
        _merge_kernel,
        grid=(b, s // tm),
        in_specs=[tok(d), full(1, d), full(d, 3 * d)] + dil_specs + [tok(D_QB), tok(D_QC), memb, memb,
                  full(LANES, d), full(D_QB, d), full(D_QC, d), full(d, d)],
        out_specs=tok(d),
        out_shape=jax.ShapeDtypeStruct((b, s, d), F32),
        scratch_shapes=[pltpu.VMEM((tm, LANES), F32)] * 4,
        compiler_params=pltpu.CompilerParams(dimension_semantics=("arbitrary", "arbitrary"),
                                             vmem_limit_bytes=VMEM_LIMIT),
        name="merge",
    )(x, g_mix, w_gate, *dil_args, ob, qc, km, vm, w_a, w_b, w_c, w_o)


def _mlp_kernel(x_ref, g_ref, w1_ref, w2_ref, out_ref, *, fchunk):
    x = x_ref[...]
    h = _rms_rows(x, g_ref[...]).astype(BF16)
    acc = x
    for c in range(w1_ref.shape[1] // fchunk):
        u = jnp.maximum(_dot(h, w1_ref[:, c * fchunk:(c + 1) * fchunk]), 0.0)
        acc = acc + _dot((u * u).astype(BF16), w2_ref[c * fchunk:(c + 1) * fchunk, :])
    out_ref[...] = acc


def _mlp(x, g_mlp, w_1, w_2, tm):
    b, s, d = x.shape
    f = w_1.shape[1]
    tok = pl.BlockSpec((None, tm, d), lambda i, j: (i, j, 0))
    full = lambda *shape: pl.BlockSpec(shape, lambda i, j: (0,) * len(shape))
    return pl.pallas_call(
        functools.partial(_mlp_kernel, fchunk=min(1024, f)),
        grid=(b, s // tm),
        in_specs=[tok, full(1, d), full(d, f), full(f, d)],
        out_specs=tok,
        out_shape=jax.ShapeDtypeStruct((b, s, d), F32),
        compiler_params=pltpu.CompilerParams(dimension_semantics=("arbitrary", "arbitrary"),
                                             vmem_limit_bytes=VMEM_LIMIT),
        name="mlp",
    )(x, g_mlp, w_1, w_2)


def _rotary_tables(positions):
    inv = jnp.power(jnp.float32(ROPE_THETA), -jnp.arange(ROT_HALF, dtype=F32) / ROT_HALF)
    ang = positions.astype(F32)[..., None] * inv
    cos, sin = jnp.cos(ang), jnp.sin(ang)
    rest = HEAD_DIM - 2 * ROT_HALF
    ones = jnp.ones(cos.shape[:-1] + (rest,), F32)
    zeros = jnp.zeros(cos.shape[:-1] + (rest,), F32)
    z8 = jnp.zeros_like(sin)
    head = lambda parts: jnp.tile(jnp.concatenate(parts, axis=-1), (1, 1, LANES // HEAD_DIM))
    return (head([cos, cos, ones]), head([-sin, z8, zeros]), head([z8, sin, zeros]),
            cos.transpose(0, 2, 1), sin.transpose(0, 2, 1))


def _block_diag_mean(width):
    r = jnp.arange(width) // HEAD_DIM
    return jnp.where(r[:, None] == r[None, :], 1.0 / HEAD_DIM, 0.0).astype(BF16)


def _layer(x, mem, tables, g_mix, g_mem, w_in, g_qa, g_ka, g_qb, g_kb, g_qc, g_kc,
           w_mem_kv, w_a, w_b, w_c, w_o, g_mlp, w_1, w_2):
    b, s, d = x.shape
    tm = min(256, s)
    tm_wide = min(512, s)
    cos, slo, shi, cos_t, sin_t = tables
    bd = _block_diag_mean(D_QA)
    scale = HEAD_DIM ** -0.5

    offs, acc = [], 0
    for w in (D_QA, D_QA, D_QA, D_QB, D_KVB, D_KVB, D_QI, HEAD_DIM, IDX_HEADS, D_QC):
        offs.append((acc, acc + w))
        acc += w
    seg = lambda k: w_in[:, offs[k][0]:offs[k][1]]
    w_std = jnp.concatenate([seg(0), seg(1), seg(2), seg(4), seg(7), seg(7), seg(9)], axis=1).astype(BF16)
    w_t = jnp.concatenate([seg(3), seg(6), seg(5), seg(8), jnp.zeros((d, _R_END - _R_WI - IDX_HEADS), w_in.dtype)],
                          axis=1).T.astype(BF16)
    w_gate = w_in[:, acc:].astype(BF16)

    tile6 = lambda g: jnp.tile(g, D_QA // HEAD_DIM)
    hg = jnp.stack([tile6(g_qa) * scale, tile6(g_ka), tile6(g_kb), tile6(g_qc) * scale,
                    jnp.zeros(D_QA), jnp.zeros(D_QA), jnp.zeros(D_QA), jnp.zeros(D_QA)]).astype(F32)
    gqt = jnp.broadcast_to((g_qb * (scale * LOG2E))[:, None], (HEAD_DIM, LANES)).astype(F32)
    shift = (HEAD_DIM * scale * LOG2E * 1.02) * jnp.max(jnp.abs(g_qb)) * jnp.max(jnp.abs(g_kb))
    shift = jnp.reshape(shift, (1,)).astype(F32)

    km, vm = _memkv(mem, g_mem[None, :], w_mem_kv.astype(BF16), bd[:D_QC, :D_QC],
                    jnp.tile(g_kc, MEM_HEADS)[None, :])
    (q0, q1, q2, k0, k1, k2, v0, v1, v2, kb, ki, qc, qbt, qit, vta, wit) = _inproj(
        x, g_mix[None, :], w_std, w_t, cos, slo, shi, cos_t, sin_t, bd, hg, gqt, tm_wide)
    dil = [_dilated(q, k, v, g) for g, (q, k, v) in enumerate(((q0, k0, v0), (q1, k1, v1), (q2, k2, v2)))]
    ob = _dsa(shift, qit, wit, ki, qbt, kb, vta)
    x = _merge(x, g_mix[None, :], w_gate, dil, ob, qc, km, vm, w_a.astype(BF16), w_b.astype(BF16),
               w_c.astype(BF16), w_o.astype(BF16), tm_wide)
    return _mlp(x, g_mlp[None, :], w_1.astype(BF16), w_2.astype(BF16), tm)


def kernel(x, mem, positions, g_mix, g_mem, w_in, g_qa, g_ka, g_qb, g_kb, g_qc, g_kc, w_mem_kv, w_a, w_b, w_c, w_o, g_mlp, w_1, w_2):
    tables = _rotary_tables(positions)
    for i in range(g_mix.shape[0]):
        x = _layer(x, mem, tables, g_mix[i], g_mem[i], w_in[i], g_qa[i], g_ka[i], g_qb[i], g_kb[i], g_qc[i],
                   g_kc[i], w_mem_kv[i], w_a[i], w_b[i], w_c[i], w_o[i], g_mlp[i], w_1[i], w_2[i])
    return x
```

```python
import functools
import math

import jax
import jax.numpy as jnp
from jax import lax
from jax.experimental import pallas as pl
from jax.experimental.pallas import tpu as pltpu

F32 = jnp.float32
BF16 = jnp.bfloat16
I32 = jnp.int32
I16 = jnp.int16

LANES = 128
HEAD_DIM = 64
ROT_HALF = 8
ROPE_THETA = 500000.0
EPS = 1e-6
DIL_GROUPS = ((128, 1), (512, 4), (2048, 16))
DIL_SPAN = 128
N_DIL_HEADS = 6
DSA_Q_HEADS = 6
DSA_KV_HEADS = 2
DSA_GROUP = DSA_Q_HEADS // DSA_KV_HEADS
DSA_TOPK_MAX = 256
IDX_HEADS = 8
MEM_HEADS = 4
D_QA = N_DIL_HEADS * HEAD_DIM
D_QB = DSA_Q_HEADS * HEAD_DIM
D_KVB = DSA_KV_HEADS * HEAD_DIM
D_QI = IDX_HEADS * HEAD_DIM
D_QC = MEM_HEADS * HEAD_DIM
V_AUG = 80
LOG2E = math.log2(math.e)
NEG_BIG = -1e30
INT_MIN = -2147483648
I16_MIN = -32768
I16_MAX = 32767
FOLD_MAX_MEMBERS = 2
BF16_ONE_BITS = 0x3F80
HI_NEG_INF = -32640
SAFE_SHIFT_LOG2 = 55.0
N_ACC = 4
VMEM_LIMIT = 56 * 1024 * 1024

_CONTRACT_LAST = (((1,), (1,)), ((), ()))


def _dot(a, b):
    return jnp.dot(a, b, preferred_element_type=F32)


def _dot_nt(a, b):
    return lax.dot_general(a, b, _CONTRACT_LAST, preferred_element_type=F32)


def _tile_lanes(a, reps):
    return a if reps == 1 else jnp.concatenate([a] * reps, axis=1)


def _low_half(shape):
    return (lax.broadcasted_iota(I32, shape, 1) % LANES) < HEAD_DIM


def _rms_rows(x, g):
    ms = jnp.mean(x * x, axis=-1, keepdims=True)
    return x * lax.rsqrt(ms + EPS) * g


def _norm_heads(p, bd, gain):
    sq = p * p
    hi = sq.astype(BF16)
    lo = (sq - hi.astype(F32)).astype(BF16)
    ms = _dot(hi, bd) + _dot(lo, bd)
    return p * lax.rsqrt(ms + EPS) * gain


def _rotary(y, cos, sin_lo, sin_hi):
    w = y.shape[1]
    reps = w // LANES
    c = _tile_lanes(cos, reps)
    a = _tile_lanes(sin_lo, reps)
    b = _tile_lanes(sin_hi, reps)
    return y * c + pltpu.roll(y, w - ROT_HALF, 1) * a + pltpu.roll(y, ROT_HALF, 1) * b


def _rotary_t(blk, cos_t, sin_t):
    x1, x2 = blk[:ROT_HALF], blk[ROT_HALF:2 * ROT_HALF]
    return jnp.concatenate([x1 * cos_t - x2 * sin_t, x2 * cos_t + x1 * sin_t, blk[2 * ROT_HALF:]], axis=0)


def _memkv_kernel(mem_ref, g_ref, w_ref, bd_ref, gk_ref, k_ref, v_ref):
    h = _rms_rows(mem_ref[...], g_ref[...]).astype(BF16)
    kv = _dot(h, w_ref[...])
    k_ref[...] = _norm_heads(kv[:, :D_QC], bd_ref[...], gk_ref[...]).astype(BF16)
    v_ref[...] = kv[:, D_QC:].astype(BF16)


def _memkv(mem, g_mem, w_mem_kv, bd, gk):
    b, m, d = mem.shape
    full = lambda *shape: pl.BlockSpec(shape, lambda i: (0,) * len(shape))
    return pl.pallas_call(
        _memkv_kernel,
        grid=(b,),
        in_specs=[pl.BlockSpec((None, m, d), lambda i: (i, 0, 0)), full(1, d), full(d, 2 * D_QC),
                  full(D_QC, D_QC), full(1, D_QC)],
        out_specs=[pl.BlockSpec((None, m, D_QC), lambda i: (i, 0, 0))] * 2,
        out_shape=[jax.ShapeDtypeStruct((b, m, D_QC), BF16)] * 2,
        name="memkv",
    )(mem, g_mem, w_mem_kv, bd, gk)


_C_QA, _C_KA, _C_VA, _C_KB, _C_KI, _C_QC, _C_END = 0, 384, 768, 1152, 1280, 1408, 1664
_R_QB, _R_QI, _R_VB, _R_WI, _R_END = 0, 384, 896, 1024, 1040


def _inproj_kernel(x_ref, g_ref, w_ref, wt_ref, cos_ref, slo_ref, shi_ref, cost_ref, sint_ref, bd_ref, hg_ref, gqt_ref,
                   q0_ref, q1_ref, q2_ref, k0_ref, k1_ref, k2_ref, v0_ref, v1_ref, v2_ref,
                   kb_ref, ki_ref, qc_ref, qbt_ref, qit_ref, vta_ref, wit_ref, dil_ref):
    tm = x_ref.shape[0]
    h = _rms_rows(x_ref[...], g_ref[...]).astype(BF16)
    cos, slo, shi = cos_ref[...], slo_ref[...], shi_ref[...]
    bd = bd_ref[...]

    def proj(a, b):
        return _dot(h, w_ref[:, a:b])

    def norm_rot(a, b, gain_row):
        w = b - a
        y = _norm_heads(proj(a, b), bd[:w, :w], hg_ref[gain_row:gain_row + 1, :w])
        return _rotary(y, cos, slo, shi)

    def store_dilated(y, out_refs):
        for g, ((_, dil), out_ref) in enumerate(zip(DIL_GROUPS, out_refs)):
            yg = y[:, g * LANES:(g + 1) * LANES]
            if dil == 1:
                out_ref[...] = yg.astype(BF16)
            else:
                dil_ref[...] = yg
                for r in range(dil):
                    out_ref[:, r * LANES:(r + 1) * LANES] = dil_ref[pl.ds(r, tm // dil, stride=dil), :].astype(BF16)

    store_dilated(norm_rot(_C_QA, _C_KA, 0), (q0_ref, q1_ref, q2_ref))
    store_dilated(norm_rot(_C_KA, _C_VA, 1), (k0_ref, k1_ref, k2_ref))
    store_dilated(proj(_C_VA, _C_KB), (v0_ref, v1_ref, v2_ref))
    kb_ref[...] = norm_rot(_C_KB, _C_KI, 2).astype(BF16)
    ki_ref[...] = _rotary(proj(_C_KI, _C_QC), cos, slo, shi)[:, :HEAD_DIM].astype(BF16)
    qc_ref[...] = _norm_heads(proj(_C_QC, _C_END), bd[:D_QC, :D_QC], hg_ref[3:4, :D_QC]).astype(BF16)

    pt = _dot_nt(wt_ref[...], h)
    cos_t, sin_t = cost_ref[...], sint_ref[...]
    gq = _tile_lanes(gqt_ref[...], tm // LANES)
    for hd in range(DSA_Q_HEADS):
        blk = pt[_R_QB + hd * HEAD_DIM:_R_QB + (hd + 1) * HEAD_DIM]
        ms = jnp.mean(blk * blk, axis=0, keepdims=True)
        qbt_ref[hd * HEAD_DIM:(hd + 1) * HEAD_DIM, :] = _rotary_t(blk * lax.rsqrt(ms + EPS) * gq, cos_t, sin_t).astype(BF16)
    for hd in range(IDX_HEADS):
        blk = pt[_R_QI + hd * HEAD_DIM:_R_QI + (hd + 1) * HEAD_DIM]
        qit_ref[hd * HEAD_DIM:(hd + 1) * HEAD_DIM, :] = _rotary_t(blk, cos_t, sin_t).astype(BF16)
    vt = pt[_R_VB:_R_WI].astype(BF16)
    pad = jnp.where(lax.broadcasted_iota(I32, (V_AUG - HEAD_DIM, tm), 0) == 0, 1.0, 0.0).astype(BF16)
    vta_ref[...] = jnp.concatenate([vt[:HEAD_DIM], pad, vt[HEAD_DIM:], pad], axis=0)
    wit_ref[...] = pt[_R_WI:_R_WI + IDX_HEADS] * ((IDX_HEADS ** -0.5) * (HEAD_DIM ** -0.5))


def _inproj(x, g_mix, w_std, w_t, cos, slo, shi, cos_t, sin_t, bd, hg, gqt, tm):
    b, s, d = x.shape
    tok = lambda w: pl.BlockSpec((None, tm, w), lambda i, j: (i, j, 0))
    tok_t = lambda r: pl.BlockSpec((None, r, tm), lambda i, j: (i, 0, j))
    full = lambda *shape: pl.BlockSpec(shape, lambda i, j: (0,) * len(shape))
    dils = [dl for _, dl in DIL_GROUPS] * 3
    std = ((D_KVB, BF16), (HEAD_DIM, BF16), (D_QC, BF16))
    tr = ((D_QB, BF16), (D_QI, BF16), (DSA_KV_HEADS * V_AUG, BF16), (IDX_HEADS, F32))
    return pl.pallas_call(
        _inproj_kernel,
        grid=(b, s // tm),
        in_specs=[tok(d), full(1, d), full(d, _C_END), full(_R_END, d), tok(LANES), tok(LANES), tok(LANES),
                  tok_t(ROT_HALF), tok_t(ROT_HALF), full(D_QA, D_QA), full(8, D_QA), full(HEAD_DIM, LANES)],
        out_specs=[pl.BlockSpec((None, tm // dl, dl * LANES), lambda i, j: (i, j, 0)) for dl in dils]
        + [tok(w) for w, _ in std] + [tok_t(r) for r, _ in tr],
        out_shape=[jax.ShapeDtypeStruct((b, s // dl, dl * LANES), BF16) for dl in dils]
        + [jax.ShapeDtypeStruct((b, s, w), t) for w, t in std]
        + [jax.ShapeDtypeStruct((b, r, s), t) for r, t in tr],
        scratch_shapes=[pltpu.VMEM((tm, LANES), F32)],
        compiler_params=pltpu.CompilerParams(dimension_semantics=("arbitrary", "arbitrary"),
                                             vmem_limit_bytes=VMEM_LIMIT),
        name="inproj",
    )(x, g_mix, w_std, w_t, cos, slo, shi, cos_t, sin_t, bd, hg, gqt)


def _dilated_kernel(q_ref, k_ref, v_ref, kp_ref, vp_ref, o_ref, l_ref, *, nsub):
    n = pl.program_id(2)
    sp = DIL_SPAN
    low = _low_half((sp, LANES))
    rq = lax.broadcasted_iota(I32, (2 * sp, 2 * sp), 0) % sp
    kj = lax.broadcasted_iota(I32, (2 * sp, 2 * sp), 1)
    dist = sp + rq - kj
    band = (dist >= 0) & (dist <= sp)
    kmin = jnp.where(n > 0, 0, sp)
    scores = []
    for i in range(nsub):
        q = q_ref[i * sp:(i + 1) * sp, :]
        kprev = kp_ref[...] if i == 0 else k_ref[(i - 1) * sp:i * sp, :]
        mask = band & (kj >= kmin) if i == 0 else band
        kk = jnp.concatenate([kprev, k_ref[i * sp:(i + 1) * sp, :]], axis=0)
        zero = jnp.zeros_like(q)
        qs = jnp.concatenate([jnp.where(low, q, zero), jnp.where(low, zero, q)], axis=0)
        scores.append(jnp.where(mask, _dot_nt(qs, kk), -jnp.inf))
    probs = []
    for i, s in enumerate(scores):
        m = jnp.max(s, axis=-1, keepdims=True)
        e = jnp.exp(s - m)
        den = jnp.sum(e, axis=-1, keepdims=True)
        probs.append((e / den).astype(BF16))
        lse = jnp.broadcast_to(m + jnp.log(den), (2 * sp, LANES))
        l_ref[i * sp:(i + 1) * sp, :] = jnp.where(low, lse[:sp], lse[sp:])
    for i, p in enumerate(probs):
        vprev = vp_ref[...] if i == 0 else v_ref[(i - 1) * sp:i * sp, :]
        o2 = _dot(p, jnp.concatenate([vprev, v_ref[i * sp:(i + 1) * sp, :]], axis=0))
        o_ref[i * sp:(i + 1) * sp, :] = jnp.where(low, o2[:sp], o2[sp:])


def _dilated(q, k, v, group):
    b, m, width = q.shape
    dilation = width // LANES
    tb = min(512, m)
    nsub = tb // DIL_SPAN
    cur = pl.BlockSpec((None, tb, LANES), lambda i, r, n: (i, n, r))
    prev = pl.BlockSpec((None, DIL_SPAN, LANES), lambda i, r, n: (i, jnp.maximum(n * nsub - 1, 0), r))
    return pl.pallas_call(
        functools.partial(_dilated_kernel, nsub=nsub),
        grid=(b, dilation, m // tb),
        in_specs=[cur, cur, cur, prev, prev],
        out_specs=[cur, cur],
        out_shape=[jax.ShapeDtypeStruct((b, m, width), F32)] * 2,
        compiler_params=pltpu.CompilerParams(dimension_semantics=("arbitrary",) * 3),
        name=f"dilated_g{group}",
    )(q, k, v, k, v)


def _dsa_kernel(shift_ref, qit_ref, wit_ref, ki_ref, qbt_ref, kb_ref, vta_ref, o_ref,
                hi_ref, lo_ref, sel_ref, fmax_ref, fmin_ref, blo_ref, tcnt_ref, tsel_ref, qic_ref, qbm_ref, m_ref, acc_ref,
                *, tq, ck, rb, topk):
    i = pl.program_id(1)
    nrows = (i + 1) * tq
    nchunk = lax.div(nrows + (ck - 1), ck)
    nblk = lax.div(nrows, rb)
    nblk_pad = nchunk * (ck // rb)

    rows = lax.broadcasted_iota(I32, (2 * HEAD_DIM, tq), 0)
    for j in range(DSA_Q_HEADS):
        g = j // DSA_GROUP
        q = qbt_ref[j * HEAD_DIM:(j + 1) * HEAD_DIM, :]
        q2 = jnp.concatenate([q, q], axis=0)
        own = (rows >= g * HEAD_DIM) & (rows < (g + 1) * HEAD_DIM)
        qbm_ref[:, j * tq:(j + 1) * tq] = jnp.where(own, q2, jnp.zeros_like(q2))
    for h in range(IDX_HEADS):
        qic_ref[:, h * tq:(h + 1) * tq] = qit_ref[h * HEAD_DIM:(h + 1) * HEAD_DIM, :]

    kpos = lax.broadcasted_iota(I32, (ck, tq), 0)
    qpos = i * tq + lax.broadcasted_iota(I32, (ck, tq), 1)

    def index_chunk(c, carry):
        off = pl.multiple_of(c * ck, ck)
        sc = _dot(ki_ref[pl.ds(off, ck), :], qic_ref[...])
        acc = None
        for h in range(IDX_HEADS):
            t = jnp.maximum(sc[:, h * tq:(h + 1) * tq], 0.0) * wit_ref[h:h + 1, :]
            acc = t if acc is None else acc + t
        acc = jnp.where(kpos + off <= qpos, acc, -jnp.inf)
        bits = lax.bitcast_convert_type(acc, I32)
        key = jnp.where(bits < 0, INT_MIN - bits, bits)
        hi_ref[pl.ds(off, ck), :] = lax.shift_right_arithmetic(key, 16).astype(I16)
        lo_ref[pl.ds(off, ck), :] = (key ^ 0x8000).astype(I16)
        return carry

    def for_each_chunk(fn):
        def pair(t, carry):
            fn(2 * t, carry)
            return fn(2 * t + 1, carry)
        lax.fori_loop(0, lax.div(nchunk, 2), pair, 0)

        @pl.when(lax.rem(nchunk, 2) == 1)
        def _():
            fn(nchunk - 1, 0)

    for_each_chunk(index_chunk)


    def count(refs, pred, nb=nblk, emit=None):
        def body(r, accs):
            base = pl.multiple_of(r * rb, rb)
            blks = [ref[pl.ds(base, rb), :] for ref in refs]
            if emit is not None:
                emit(blks, base)
            accs = list(accs)
            for u in range(rb // 16):
                hit = pred([blk[u * 16:(u + 1) * 16, :] for blk in blks], base + u * 16)
                accs[u % N_ACC] = accs[u % N_ACC] + jnp.where(hit, jnp.int16(1), jnp.int16(0))
            return tuple(accs)
        accs = lax.fori_loop(0, nb, body, tuple(jnp.zeros((16, tq), I16) for _ in range(N_ACC)))
        tot = accs[0].astype(I32)
        for a in accs[1:]:
            tot = tot + a.astype(I32)
        return jnp.sum(tot, axis=0, keepdims=True)

    def rows16(v):
        return jnp.broadcast_to(v, (16, tq)).astype(I16)

    def bisect(ref, want):
        def step(it, u):
            uc = u | lax.shift_left(jnp.int32(1), 15 - it)
            cand = rows16(uc - 32768)
            cnt = count([ref], lambda b, p0: b[0] >= cand)
            return jnp.where(cnt >= want, uc, u)
        return lax.fori_loop(0, 16, step, jnp.zeros((1, tq), I32)) - 32768

    a32 = jnp.maximum(bisect(hi_ref, topk), HI_NEG_INF)
    a16 = rows16(a32)
    a_blk = jnp.broadcast_to(a32, (rb, tq)).astype(I16)

    fmax_ref[...] = jnp.full(fmax_ref.shape, I16_MIN, I16)
    fmin_ref[...] = jnp.full(fmin_ref.shape, I16_MIN, I16)

    def fold_block(r, carry):
        accs, most = carry
        base = pl.multiple_of(r * rb, rb)
        hi, lo = hi_ref[pl.ds(base, rb), :], lo_ref[pl.ds(base, rb), :]
        top = jnp.full((16, tq), I16_MIN, I16)
        low = jnp.full((16, tq), I16_MAX, I16)
        members = jnp.zeros((16, tq), I16)
        accs = list(accs)
        for u in range(rb // 16):
            h, l = hi[u * 16:(u + 1) * 16, :], lo[u * 16:(u + 1) * 16, :]
            member = h == a16
            up, down = jnp.where(member, l, jnp.int16(I16_MIN)), jnp.where(member, l, jnp.int16(I16_MAX))
            top = jnp.where(up > top, up, top)
            low = jnp.where(down < low, down, low)
            members = members + jnp.where(member, jnp.int16(1), jnp.int16(0))
            accs[u % N_ACC] = accs[u % N_ACC] + jnp.where(h > a16, jnp.int16(1), jnp.int16(0))
        row = pl.multiple_of(r * 16, 16)
        fmax_ref[pl.ds(row, 16), :] = top
        fmin_ref[pl.ds(row, 16), :] = jnp.where(members >= 2, low, jnp.int16(I16_MIN))
        return tuple(accs), jnp.where(members > most, members, most)

    accs, most = lax.fori_loop(0, nblk, fold_block, (tuple(jnp.zeros((16, tq), I16) for _ in range(N_ACC)),
                                                     jnp.zeros((16, tq), I16)))
    n_hi = accs[0].astype(I32)
    for a in accs[1:]:
        n_hi = n_hi + a.astype(I32)
    n_hi = jnp.sum(n_hi, axis=0, keepdims=True)
    want_lo = topk - n_hi

    def folded_step(it, u):
        uc = u | lax.shift_left(jnp.int32(1), 15 - it)
        cand = rows16(uc - 32768)
        accs = [jnp.zeros((16, tq), I16) for _ in range(N_ACC)]
        for r in range(fmax_ref.shape[0] // 16):
            for k, ref in enumerate((fmax_ref, fmin_ref)):
                hit = ref[r * 16:(r + 1) * 16, :] >= cand
                accs[(2 * r + k) % N_ACC] = accs[(2 * r + k) % N_ACC] + jnp.where(hit, jnp.int16(1), jnp.int16(0))
        tot = accs[0].astype(I32)
        for a in accs[1:]:
            tot = tot + a.astype(I32)
        return jnp.where(jnp.sum(tot, axis=0, keepdims=True) >= want_lo, uc, u)

    blo_ref[...] = jnp.broadcast_to(lax.fori_loop(0, 16, folded_step, jnp.zeros((1, tq), I32)) - 32768, blo_ref.shape)

    @pl.when(jnp.max(most.astype(I32)) > FOLD_MAX_MEMBERS)
    def _():
        def bucket(r, carry):
            base = pl.multiple_of(r * rb, rb)
            sel_ref[pl.ds(base, rb), :] = jnp.where(hi_ref[pl.ds(base, rb), :] == a_blk, lo_ref[pl.ds(base, rb), :],
                                                    jnp.int16(I16_MIN))
            return carry
        lax.fori_loop(0, nblk, bucket, 0)
        blo_ref[...] = jnp.broadcast_to(bisect(sel_ref, want_lo), blo_ref.shape)

    b32 = blo_ref[0:1, :]
    b32 = jnp.where(a32 == HI_NEG_INF, jnp.maximum(b32, I16_MIN + 1), b32)
    b16 = rows16(b32)
    b_blk = jnp.broadcast_to(b32, (rb, tq)).astype(I16)

    def emit_selection(blks, base):
        sel = (blks[0] > a_blk) | ((blks[0] == a_blk) & (blks[1] >= b_blk))
        sel_ref[pl.ds(base, rb), :] = jnp.where(sel, jnp.int16(BF16_ONE_BITS), jnp.int16(0))

    n_ge = count([hi_ref, lo_ref], lambda b, p0: (b[0] > a16) | ((b[0] == a16) & (b[1] >= b16)),
                 nb=nblk_pad, emit=emit_selection)

    need = n_ge > topk

    @pl.when(jnp.max(jnp.where(need, 1, 0)) > 0)
    def _():
        nb_max = tcnt_ref.shape[0]
        none = jnp.int32(2 ** 30)

        def tied(hi, lo, a, b):
            return (hi == a) & (lo == b)

        tcnt_ref[...] = jnp.zeros(tcnt_ref.shape, I32)

        def tie_block(r, carry):
            base = pl.multiple_of(r * rb, rb)
            hi, lo = hi_ref[pl.ds(base, rb), :], lo_ref[pl.ds(base, rb), :]
            acc = jnp.zeros((16, tq), I16)
            for u in range(rb // 16):
                acc = acc + jnp.where(tied(hi[u * 16:(u + 1) * 16, :], lo[u * 16:(u + 1) * 16, :], a16, b16),
                                      jnp.int16(1), jnp.int16(0))
            tcnt_ref[r] = jnp.broadcast_to(jnp.sum(acc.astype(I32), axis=0, keepdims=True), (8, tq))
            return carry

        lax.fori_loop(0, nblk, tie_block, 0)

        per_block = [tcnt_ref[r][0:1, :] for r in range(nb_max)]
        n_tied = per_block[0]
        for c in per_block[1:]:
            n_tied = n_tied + c
        want = jnp.where(need, topk - (n_ge - n_tied), none)
        last = jnp.full((1, tq), nb_max, I32)
        before = jnp.zeros((1, tq), I32)
        prefix = jnp.zeros((1, tq), I32)
        for r, c in enumerate(per_block):
            reached = (last == nb_max) & (prefix + c >= want)
            last = jnp.where(reached, r, last)
            before = jnp.where(reached, prefix, before)
            prefix = prefix + c
        want_here = want - before

        tsel_ref[...] = jnp.zeros(tsel_ref.shape, I16)

        def pick_block(r, carry):
            base = pl.multiple_of(r * rb, rb)
            here = jnp.broadcast_to(jnp.where(last == r, 1, 0), (rb, tq)).astype(I16) != 0
            hit = tied(hi_ref[pl.ds(base, rb), :], lo_ref[pl.ds(base, rb), :], a_blk, b_blk) & here
            tsel_ref[...] = tsel_ref[...] + jnp.where(hit, jnp.int16(1), jnp.int16(0))
            return carry

        lax.fori_loop(0, nblk, pick_block, 0)

        in_block = lax.broadcasted_iota(I32, (rb, tq), 0).astype(I16)
        block_bits = (rb - 1).bit_length()

        def row_step(it, e):
            ec = e | lax.shift_left(jnp.int32(1), block_bits - 1 - it)
            hit = (tsel_ref[...] != 0) & (in_block < jnp.broadcast_to(ec, (rb, tq)).astype(I16))
            ones = jnp.where(hit, jnp.int16(1), jnp.int16(0))
            acc = ones[0:16, :]
            for u in range(1, rb // 16):
                acc = acc + ones[u * 16:(u + 1) * 16, :]
            below = jnp.sum(acc.astype(I32), axis=0, keepdims=True)
            return jnp.where(below < want_here, ec, e)

        e_last = lax.fori_loop(0, block_bits, row_step, jnp.zeros((1, tq), I32))
        x = jnp.where(last < nb_max, jnp.minimum(last * rb + e_last, I16_MAX), I16_MAX)

        def demote(r, carry):
            base = pl.multiple_of(r * rb, rb)
            pos = (base + lax.broadcasted_iota(I32, (rb, tq), 0)).astype(I16)
            drop = (tied(hi_ref[pl.ds(base, rb), :], lo_ref[pl.ds(base, rb), :], a_blk, b_blk)
                    & (pos > jnp.broadcast_to(x, (rb, tq)).astype(I16)))
            sel_ref[pl.ds(base, rb), :] = jnp.where(drop, jnp.int16(0), sel_ref[pl.ds(base, rb), :])
            return carry

        lax.fori_loop(0, nblk, demote, 0)

    shift = shift_ref[0]

    @pl.when(shift < SAFE_SHIFT_LOG2)
    def _():
        acc_ref[...] = jnp.zeros(acc_ref.shape, F32)

        def attend_chunk(c, carry):
            off = pl.multiple_of(c * ck, ck)
            kch = kb_ref[pl.ds(off, ck), :]
            msk = lax.bitcast_convert_type(sel_ref[pl.ds(off, ck), :], BF16)
            for g in range(DSA_KV_HEADS):
                sc = _dot(kch, qbm_ref[:, g * DSA_GROUP * tq:(g + 1) * DSA_GROUP * tq])
                p = jnp.exp2(sc - shift).astype(BF16) * _tile_lanes(msk, DSA_GROUP)
                acc_ref[g] += _dot(vta_ref[g * V_AUG:(g + 1) * V_AUG, pl.ds(off, ck)], p)
            return carry

        for_each_chunk(attend_chunk)

    @pl.when(shift >= SAFE_SHIFT_LOG2)
    def _():
        m_ref[...] = jnp.full(m_ref.shape, NEG_BIG, F32)
        acc_ref[...] = jnp.zeros(acc_ref.shape, F32)

        def attend_chunk(c, carry):
            off = pl.multiple_of(c * ck, ck)
            kch = kb_ref[pl.ds(off, ck), :]
            sel = sel_ref[pl.ds(off, ck), :].astype(I32) != 0
            for j in range(DSA_Q_HEADS):
                g = j // DSA_GROUP
                cols = slice((j % DSA_GROUP) * tq, (j % DSA_GROUP + 1) * tq)
                sc = jnp.where(sel, _dot(kch, qbm_ref[:, j * tq:(j + 1) * tq]), NEG_BIG)
                m_prev = m_ref[j]
                m_new = jnp.maximum(m_prev, jnp.max(sc, axis=0, keepdims=True))
                p = jnp.where(sel, jnp.exp2(sc - m_new), 0.0).astype(BF16)
                acc_ref[g, :, cols] = jnp.exp2(m_prev - m_new) * acc_ref[g, :, cols] + _dot(
                    vta_ref[g * V_AUG:(g + 1) * V_AUG, pl.ds(off, ck)], p)
                m_ref[j] = m_new
            return carry

        lax.fori_loop(0, nchunk, attend_chunk, 0)

    for pair in range(DSA_Q_HEADS // 2):
        halves = []
        for j in (2 * pair, 2 * pair + 1):
            a = acc_ref[j // DSA_GROUP, :, (j % DSA_GROUP) * tq:(j % DSA_GROUP + 1) * tq]
            halves.append(a[:HEAD_DIM] / a[HEAD_DIM:HEAD_DIM + 1])
        o_ref[:, pair * LANES:(pair + 1) * LANES] = jnp.concatenate(halves, axis=0).T.astype(BF16)


def _dsa(shift, qit, wit, ki, qbt, kb, vta):
    b, _, s = qbt.shape
    tq = min(256, s)
    ck = min(512, s)
    rb = 256
    topk = min(DSA_TOPK_MAX, s // 4)
    qt = lambda w: pl.BlockSpec((None, w, tq), lambda i, j: (i, 0, j))
    seq = lambda w: pl.BlockSpec((None, s, w), lambda i, j: (i, 0, 0))
    return pl.pallas_call(
        functools.partial(_dsa_kernel, tq=tq, ck=ck, rb=rb, topk=topk),
        grid=(b, s // tq),
        in_specs=[pl.BlockSpec(memory_space=pltpu.SMEM), qt(D_QI), qt(IDX_HEADS), seq(HEAD_DIM), qt(D_QB), seq(D_KVB),
                  pl.BlockSpec((None, DSA_KV_HEADS * V_AUG, s), lambda i, j: (i, 0, 0))],
        out_specs=pl.BlockSpec((None, tq, D_QB), lambda i, j: (i, j, 0)),
        out_shape=jax.ShapeDtypeStruct((b, s, D_QB), BF16),
        scratch_shapes=[
            pltpu.VMEM((s, tq), I16),
            pltpu.VMEM((s, tq), I16),
            pltpu.VMEM((s, tq), I16),
            pltpu.VMEM((s // rb * 16, tq), I16),
            pltpu.VMEM((s // rb * 16, tq), I16),
            pltpu.VMEM((8, tq), I32),
            pltpu.VMEM((s // rb, 8, tq), I32),
            pltpu.VMEM((rb, tq), I16),
            pltpu.VMEM((HEAD_DIM, IDX_HEADS * tq), BF16),
            pltpu.VMEM((2 * HEAD_DIM, DSA_Q_HEADS * tq), BF16),
            pltpu.VMEM((DSA_Q_HEADS, 1, tq), F32),
            pltpu.VMEM((DSA_KV_HEADS, V_AUG, DSA_GROUP * tq), F32),
        ],
        compiler_params=pltpu.CompilerParams(dimension_semantics=("arbitrary", "arbitrary"),
                                             vmem_limit_bytes=VMEM_LIMIT),
        name="dsa",
    )(shift, qit, wit, ki, qbt, kb, vta)


def _merge_kernel(x_ref, g_ref, wg_ref, o0_ref, l0_ref, o1_ref, l1_ref, o2_ref, l2_ref, ob_ref, qc_ref,
                  km_ref, vm_ref, wa_ref, wb_ref, wc_ref, wo_ref, out_ref, *stage_refs):
    x = x_ref[...]
    d = x.shape[1]
    tm = x.shape[0]
    h = _rms_rows(x, g_ref[...]).astype(BF16)

    def token_major(ref, stage_ref):
        dil = ref.shape[1] // LANES
        if dil == 1:
            return ref[...]
        for r in range(dil):
            stage_ref[pl.ds(r, tm // dil, stride=dil), :] = ref[:, r * LANES:(r + 1) * LANES]
        return stage_ref[...]

    def gate(k):
        z = _dot(h, wg_ref[:, k * d:(k + 1) * d])
        return 1.0 / (1.0 + jnp.exp(-z))

    l0, l1, l2 = l0_ref[...], token_major(l1_ref, stage_refs[0]), token_major(l2_ref, stage_refs[1])
    o0, o1, o2 = o0_ref[...], token_major(o1_ref, stage_refs[2]), token_major(o2_ref, stage_refs[3])
    mx = jnp.maximum(jnp.maximum(l0, l1), l2)
    e0, e1, e2 = jnp.exp(l0 - mx), jnp.exp(l1 - mx), jnp.exp(l2 - mx)
    oa = (e0 * o0 + e1 * o1 + e2 * o2) / (e0 + e1 + e2)
    merged = gate(0) * _dot(oa.astype(BF16), wa_ref[...])

    merged = merged + gate(1) * _dot(ob_ref[...], wb_ref[...])

    low = _low_half((tm, LANES))
    cols = []
    for mcol in range(MEM_HEADS // 2):
        q = qc_ref[:, mcol * LANES:(mcol + 1) * LANES]
        km = km_ref[:, mcol * LANES:(mcol + 1) * LANES]
        vm = vm_ref[:, mcol * LANES:(mcol + 1) * LANES]
        zero = jnp.zeros_like(q)
        outs = []
        for qh in (jnp.where(low, q, zero), jnp.where(low, zero, q)):
            s = _dot_nt(qh, km)
            e = jnp.exp(s - jnp.max(s, axis=-1, keepdims=True))
            p = e / jnp.sum(e, axis=-1, keepdims=True)
            outs.append(_dot(p.astype(BF16), vm))
        cols.append(jnp.where(low, outs[0], outs[1]))
    oc = jnp.concatenate(cols, axis=1)
    merged = merged + gate(2) * _dot(oc.astype(BF16), wc_ref[...])

    out_ref[...] = x + _dot(merged.astype(BF16), wo_ref[...])


def _merge(x, g_mix, w_gate, dil, ob, qc, km, vm, w_a, w_b, w_c, w_o, tm):
    b, s, d = x.shape
    mlen = km.shape[1]
    tok = lambda w: pl.BlockSpec((None, tm, w), lambda i, j: (i, j, 0))
    full = lambda *shape: pl.BlockSpec(shape, lambda i, j: (0,) * len(shape))
    memb = pl.BlockSpec((None, mlen, D_QC), lambda i, j: (i, 0, 0))
    dil_args = [t for pair in dil for t in pair]
    dil_specs = [pl.BlockSpec((None, tm // dl, dl * LANES), lambda i, j: (i, j, 0))
                 for _, dl in DIL_GROUPS for _ in range(2)]
    return pl.pallas_call(
        _merge_kernel,
        grid=(b, s // tm),
        in_specs=[tok(d), full(1, d), full(d, 3 * d)] + dil_specs + [tok(D_QB), tok(D_QC), memb, memb,
                  full(LANES, d), full(D_QB, d), full(D_QC, d), full(d, d)],
        out_specs=tok(d),
        out_shape=jax.ShapeDtypeStruct((b, s, d), F32),
        scratch_shapes=[pltpu.VMEM((tm, LANES), F32)] * 4,
        compiler_params=pltpu.CompilerParams(dimension_semantics=("arbitrary", "arbitrary"),
                                             vmem_limit_bytes=VMEM_LIMIT),
        name="merge",
    )(x, g_mix, w_gate, *dil_args, ob, qc, km, vm, w_a, w_b, w_c, w_o)


def _mlp_kernel(x_ref, g_ref, w1_ref, w2_ref, out_ref, *, fchunk):
    x = x_ref[...]
    h = _rms_rows(x, g_ref[...]).astype(BF16)
    acc = x
    for c in range(w1_ref.shape[1] // fchunk):
        u = jnp.maximum(_dot(h, w1_ref[:, c * fchunk:(c + 1) * fchunk]), 0.0)
        acc = acc + _dot((u * u).astype(BF16), w2_ref[c * fchunk:(c + 1) * fchunk, :])
    out_ref[...] = acc


def _mlp(x, g_mlp, w_1, w_2, tm):
    b, s, d = x.shape
    f = w_1.shape[1]
    tok = pl.BlockSpec((None, tm, d), lambda i, j: (i, j, 0))
    full = lambda *shape: pl.BlockSpec(shape, lambda i, j: (0,) * len(shape), pipeline_mode=pl.Buffered(1))
    return pl.pallas_call(
        functools.partial(_mlp_kernel, fchunk=min(1024, f)),
        grid=(b, s // tm),
        in_specs=[tok, full(1, d), full(d, f), full(f, d)],
        out_specs=tok,
        out_shape=jax.ShapeDtypeStruct((b, s, d), F32),
        compiler_params=pltpu.CompilerParams(dimension_semantics=("arbitrary", "arbitrary"),
                                             vmem_limit_bytes=VMEM_LIMIT),
        name="mlp",
    )(x, g_mlp, w_1, w_2)


def _rotary_tables(positions):
    inv = jnp.power(jnp.float32(ROPE_THETA), -jnp.arange(ROT_HALF, dtype=F32) / ROT_HALF)
    ang = positions.astype(F32)[..., None] * inv
    cos, sin = jnp.cos(ang), jnp.sin(ang)
    rest = HEAD_DIM - 2 * ROT_HALF
    ones = jnp.ones(cos.shape[:-1] + (rest,), F32)
    zeros = jnp.zeros(cos.shape[:-1] + (rest,), F32)
    z8 = jnp.zeros_like(sin)
    head = lambda parts: jnp.tile(jnp.concatenate(parts, axis=-1), (1, 1, LANES // HEAD_DIM))
    return (head([cos, cos, ones]), head([-sin, z8, zeros]), head([z8, sin, zeros]),
            cos.transpose(0, 2, 1), sin.transpose(0, 2, 1))


def _block_diag_mean(width):
    r = jnp.arange(width) // HEAD_DIM
    return jnp.where(r[:, None] == r[None, :], 1.0 / HEAD_DIM, 0.0).astype(BF16)


def _layer(x, mem, tables, g_mix, g_mem, w_in, g_qa, g_ka, g_qb, g_kb, g_qc, g_kc,
           w_mem_kv, w_a, w_b, w_c, w_o, g_mlp, w_1, w_2):
    b, s, d = x.shape
    tm = min(256, s)
    tm_wide = min(512, s)
    cos, slo, shi, cos_t, sin_t = tables
    bd = _block_diag_mean(D_QA)
    scale = HEAD_DIM ** -0.5

    offs, acc = [], 0
    for w in (D_QA, D_QA, D_QA, D_QB, D_KVB, D_KVB, D_QI, HEAD_DIM, IDX_HEADS, D_QC):
        offs.append((acc, acc + w))
        acc += w
    seg = lambda k: w_in[:, offs[k][0]:offs[k][1]]
    w_std = jnp.concatenate([seg(0), seg(1), seg(2), seg(4), seg(7), seg(7), seg(9)], axis=1).astype(BF16)
    w_t = jnp.concatenate([seg(3), seg(6), seg(5), seg(8), jnp.zeros((d, _R_END - _R_WI - IDX_HEADS), w_in.dtype)],
                          axis=1).T.astype(BF16)
    w_gate = w_in[:, acc:].astype(BF16)

    tile6 = lambda g: jnp.tile(g, D_QA // HEAD_DIM)
    hg = jnp.stack([tile6(g_qa) * scale, tile6(g_ka), tile6(g_kb), tile6(g_qc) * scale,
                    jnp.zeros(D_QA), jnp.zeros(D_QA), jnp.zeros(D_QA), jnp.zeros(D_QA)]).astype(F32)
    gqt = jnp.broadcast_to((g_qb * (scale * LOG2E))[:, None], (HEAD_DIM, LANES)).astype(F32)
    shift = (HEAD_DIM * scale * LOG2E * 1.02) * jnp.max(jnp.abs(g_qb)) * jnp.max(jnp.abs(g_kb))
    shift = jnp.reshape(shift, (1,)).astype(F32)

    km, vm = _memkv(mem, g_mem[None, :], w_mem_kv.astype(BF16), bd[:D_QC, :D_QC],
                    jnp.tile(g_kc, MEM_HEADS)[None, :])
    (q0, q1, q2, k0, k1, k2, v0, v1, v2, kb, ki, qc, qbt, qit, vta, wit) = _inproj(
        x, g_mix[None, :], w_std, w_t, cos, slo, shi, cos_t, sin_t, bd, hg, gqt, tm_wide)
    dil = [_dilated(q, k, v, g) for g, (q, k, v) in enumerate(((q0, k0, v0), (q1, k1, v1), (q2, k2, v2)))]
    ob = _dsa(shift, qit, wit, ki, qbt, kb, vta)
    x = _merge(x, g_mix[None, :], w_gate, dil, ob, qc, km, vm, w_a.astype(BF16), w_b.astype(BF16),
               w_c.astype(BF16), w_o.astype(BF16), tm_wide)
    return _mlp(x, g_mlp[None, :], w_1.astype(BF16), w_2.astype(BF16), tm_wide)


def kernel(x, mem, positions, g_mix, g_mem, w_in, g_qa, g_ka, g_qb, g_kb, g_qc, g_kc, w_mem_kv, w_a, w_b, w_c, w_o, g_mlp, w_1, w_2):
    tables = _rotary_tables(positions)
    for i in range(g_mix.shape[0]):
        x = _layer(x, mem, tables, g_mix[i], g_mem[i], w_in[i], g_qa[i], g_ka[i], g_qb[i], g_kb[i], g_qc[i],
                   g_kc[i], w_mem_kv[i], w_a[i], w_b[i], w_c[i], w_o[i], g_mlp[i], w_1[i], w_2[i])
    return x
```

```python
import functools
import math

import jax
import jax.numpy as jnp
from jax import lax
from jax.experimental import pallas as pl
from jax.experimental.pallas import tpu as pltpu

F32 = jnp.float32
BF16 = jnp.bfloat16
I32 = jnp.int32
I16 = jnp.int16

LANES = 128
HEAD_DIM = 64
ROT_HALF = 8
ROPE_THETA = 500000.0
EPS = 1e-6
DIL_GROUPS = ((128, 1), (512, 4), (2048, 16))
DIL_SPAN = 128
N_DIL_HEADS = 6
DSA_Q_HEADS = 6
DSA_KV_HEADS = 2
DSA_GROUP = DSA_Q_HEADS // DSA_KV_HEADS
DSA_TOPK_MAX = 256
IDX_HEADS = 8
MEM_HEADS = 4
D_QA = N_DIL_HEADS * HEAD_DIM
D_QB = DSA_Q_HEADS * HEAD_DIM
D_KVB = DSA_KV_HEADS * HEAD_DIM
D_QI = IDX_HEADS * HEAD_DIM
D_QC = MEM_HEADS * HEAD_DIM
V_AUG = 80
LOG2E = math.log2(math.e)
NEG_BIG = -1e30
INT_MIN = -2147483648
I16_MIN = -32768
I16_MAX = 32767
FOLD_MAX_MEMBERS = 2
BF16_ONE_BITS = 0x3F80
HI_NEG_INF = -32640
SAFE_SHIFT_LOG2 = 55.0
N_ACC = 4
MERGE_ROW_PARTS = 2
VMEM_LIMIT = 56 * 1024 * 1024

_CONTRACT_LAST = (((1,), (1,)), ((), ()))


def _dot(a, b):
    return jnp.dot(a, b, preferred_element_type=F32)


def _dot_nt(a, b):
    return lax.dot_general(a, b, _CONTRACT_LAST, preferred_element_type=F32)


def _tile_lanes(a, reps):
    return a if reps == 1 else jnp.concatenate([a] * reps, axis=1)


def _low_half(shape):
    return (lax.broadcasted_iota(I32, shape, 1) % LANES) < HEAD_DIM


def _rms_rows(x, g):
    ms = jnp.mean(x * x, axis=-1, keepdims=True)
    return x * lax.rsqrt(ms + EPS) * g


def _norm_heads(p, bd, gain):
    sq = p * p
    hi = sq.astype(BF16)
    lo = (sq - hi.astype(F32)).astype(BF16)
    ms = _dot(hi, bd) + _dot(lo, bd)
    return p * lax.rsqrt(ms + EPS) * gain


def _rotary(y, cos, sin_lo, sin_hi):
    w = y.shape[1]
    reps = w // LANES
    c = _tile_lanes(cos, reps)
    a = _tile_lanes(sin_lo, reps)
    b = _tile_lanes(sin_hi, reps)
    return y * c + pltpu.roll(y, w - ROT_HALF, 1) * a + pltpu.roll(y, ROT_HALF, 1) * b


def _rotary_t(blk, cos_t, sin_t):
    x1, x2 = blk[:ROT_HALF], blk[ROT_HALF:2 * ROT_HALF]
    return jnp.concatenate([x1 * cos_t - x2 * sin_t, x2 * cos_t + x1 * sin_t, blk[2 * ROT_HALF:]], axis=0)


def _memkv_kernel(mem_ref, g_ref, w_ref, bd_ref, gk_ref, k_ref, v_ref):
    h = _rms_rows(mem_ref[...], g_ref[...]).astype(BF16)
    kv = _dot(h, w_ref[...])
    k_ref[...] = _norm_heads(kv[:, :D_QC], bd_ref[...], gk_ref[...]).astype(BF16)
    v_ref[...] = kv[:, D_QC:].astype(BF16)


def _memkv(mem, g_mem, w_mem_kv, bd, gk):
    b, m, d = mem.shape
    full = lambda *shape: pl.BlockSpec(shape, lambda i: (0,) * len(shape))
    return pl.pallas_call(
        _memkv_kernel,
        grid=(b,),
        in_specs=[pl.BlockSpec((None, m, d), lambda i: (i, 0, 0)), full(1, d), full(d, 2 * D_QC),
                  full(D_QC, D_QC), full(1, D_QC)],
        out_specs=[pl.BlockSpec((None, m, D_QC), lambda i: (i, 0, 0))] * 2,
        out_shape=[jax.ShapeDtypeStruct((b, m, D_QC), BF16)] * 2,
        name="memkv",
    )(mem, g_mem, w_mem_kv, bd, gk)


_C_QA, _C_KA, _C_VA, _C_KB, _C_KI, _C_QC, _C_END = 0, 384, 768, 1152, 1280, 1408, 1664
_R_QB, _R_QI, _R_VB, _R_WI, _R_END = 0, 384, 896, 1024, 1040


def _inproj_kernel(x_ref, g_ref, w_ref, wt_ref, cos_ref, slo_ref, shi_ref, cost_ref, sint_ref, bd_ref, hg_ref, gqt_ref,
                   q0_ref, q1_ref, q2_ref, k0_ref, k1_ref, k2_ref, v0_ref, v1_ref, v2_ref,
                   kb_ref, ki_ref, qc_ref, qbt_ref, qit_ref, vta_ref, wit_ref, dil_ref):
    tm = x_ref.shape[0]
    h = _rms_rows(x_ref[...], g_ref[...]).astype(BF16)
    cos, slo, shi = cos_ref[...], slo_ref[...], shi_ref[...]
    bd = bd_ref[...]

    def proj(a, b):
        return _dot(h, w_ref[:, a:b])

    def norm_rot(a, b, gain_row):
        w = b - a
        y = _norm_heads(proj(a, b), bd[:w, :w], hg_ref[gain_row:gain_row + 1, :w])
        return _rotary(y, cos, slo, shi)

    def store_dilated(y, out_refs):
        for g, ((_, dil), out_ref) in enumerate(zip(DIL_GROUPS, out_refs)):
            yg = y[:, g * LANES:(g + 1) * LANES]
            if dil == 1:
                out_ref[...] = yg.astype(BF16)
            else:
                dil_ref[...] = yg
                for r in range(dil):
                    out_ref[:, r * LANES:(r + 1) * LANES] = dil_ref[pl.ds(r, tm // dil, stride=dil), :].astype(BF16)

    store_dilated(norm_rot(_C_QA, _C_KA, 0), (q0_ref, q1_ref, q2_ref))
    store_dilated(norm_rot(_C_KA, _C_VA, 1), (k0_ref, k1_ref, k2_ref))
    store_dilated(proj(_C_VA, _C_KB), (v0_ref, v1_ref, v2_ref))
    kb_ref[...] = norm_rot(_C_KB, _C_KI, 2).astype(BF16)
    ki_ref[...] = _rotary(proj(_C_KI, _C_QC), cos, slo, shi)[:, :HEAD_DIM].astype(BF16)
    qc_ref[...] = _norm_heads(proj(_C_QC, _C_END), bd[:D_QC, :D_QC], hg_ref[3:4, :D_QC]).astype(BF16)

    pt = _dot_nt(wt_ref[...], h)
    cos_t, sin_t = cost_ref[...], sint_ref[...]
    gq = _tile_lanes(gqt_ref[...], tm // LANES)
    for hd in range(DSA_Q_HEADS):
        blk = pt[_R_QB + hd * HEAD_DIM:_R_QB + (hd + 1) * HEAD_DIM]
        ms = jnp.mean(blk * blk, axis=0, keepdims=True)
        qbt_ref[hd * HEAD_DIM:(hd + 1) * HEAD_DIM, :] = _rotary_t(blk * lax.rsqrt(ms + EPS) * gq, cos_t, sin_t).astype(BF16)
    for hd in range(IDX_HEADS):
        blk = pt[_R_QI + hd * HEAD_DIM:_R_QI + (hd + 1) * HEAD_DIM]
        qit_ref[hd * HEAD_DIM:(hd + 1) * HEAD_DIM, :] = _rotary_t(blk, cos_t, sin_t).astype(BF16)
    vt = pt[_R_VB:_R_WI].astype(BF16)
    pad = jnp.where(lax.broadcasted_iota(I32, (V_AUG - HEAD_DIM, tm), 0) == 0, 1.0, 0.0).astype(BF16)
    vta_ref[...] = jnp.concatenate([vt[:HEAD_DIM], pad, vt[HEAD_DIM:], pad], axis=0)
    wit_ref[...] = pt[_R_WI:_R_WI + IDX_HEADS] * ((IDX_HEADS ** -0.5) * (HEAD_DIM ** -0.5))


def _inproj(x, g_mix, w_std, w_t, cos, slo, shi, cos_t, sin_t, bd, hg, gqt, tm):
    b, s, d = x.shape
    tok = lambda w: pl.BlockSpec((None, tm, w), lambda i, j: (i, j, 0))
    tok_t = lambda r: pl.BlockSpec((None, r, tm), lambda i, j: (i, 0, j))
    full = lambda *shape: pl.BlockSpec(shape, lambda i, j: (0,) * len(shape))
    dils = [dl for _, dl in DIL_GROUPS] * 3
    std = ((D_KVB, BF16), (HEAD_DIM, BF16), (D_QC, BF16))
    tr = ((D_QB, BF16), (D_QI, BF16), (DSA_KV_HEADS * V_AUG, BF16), (IDX_HEADS, F32))
    return pl.pallas_call(
        _inproj_kernel,
        grid=(b, s // tm),
        in_specs=[tok(d), full(1, d), full(d, _C_END), full(_R_END, d), tok(LANES), tok(LANES), tok(LANES),
                  tok_t(ROT_HALF), tok_t(ROT_HALF), full(D_QA, D_QA), full(8, D_QA), full(HEAD_DIM, LANES)],
        out_specs=[pl.BlockSpec((None, tm // dl, dl * LANES), lambda i, j: (i, j, 0)) for dl in dils]
        + [tok(w) for w, _ in std] + [tok_t(r) for r, _ in tr],
        out_shape=[jax.ShapeDtypeStruct((b, s // dl, dl * LANES), BF16) for dl in dils]
        + [jax.ShapeDtypeStruct((b, s, w), t) for w, t in std]
        + [jax.ShapeDtypeStruct((b, r, s), t) for r, t in tr],
        scratch_shapes=[pltpu.VMEM((tm, LANES), F32)],
        compiler_params=pltpu.CompilerParams(dimension_semantics=("arbitrary", "arbitrary"),
                                             vmem_limit_bytes=VMEM_LIMIT),
        name="inproj",
    )(x, g_mix, w_std, w_t, cos, slo, shi, cos_t, sin_t, bd, hg, gqt)


def _dilated_kernel(q_ref, k_ref, v_ref, kp_ref, vp_ref, o_ref, l_ref, *, nsub):
    n = pl.program_id(2)
    sp = DIL_SPAN
    low = _low_half((sp, LANES))
    rq = lax.broadcasted_iota(I32, (2 * sp, 2 * sp), 0) % sp
    kj = lax.broadcasted_iota(I32, (2 * sp, 2 * sp), 1)
    dist = sp + rq - kj
    band = (dist >= 0) & (dist <= sp)
    kmin = jnp.where(n > 0, 0, sp)
    scores = []
    for i in range(nsub):
        q = q_ref[i * sp:(i + 1) * sp, :]
        kprev = kp_ref[...] if i == 0 else k_ref[(i - 1) * sp:i * sp, :]
        mask = band & (kj >= kmin) if i == 0 else band
        kk = jnp.concatenate([kprev, k_ref[i * sp:(i + 1) * sp, :]], axis=0)
        zero = jnp.zeros_like(q)
        qs = jnp.concatenate([jnp.where(low, q, zero), jnp.where(low, zero, q)], axis=0)
        scores.append(jnp.where(mask, _dot_nt(qs, kk), -jnp.inf))
    probs = []
    for i, s in enumerate(scores):
        m = jnp.max(s, axis=-1, keepdims=True)
        e = jnp.exp(s - m)
        den = jnp.sum(e, axis=-1, keepdims=True)
        probs.append((e / den).astype(BF16))
        lse = jnp.broadcast_to(m + jnp.log(den), (2 * sp, LANES))
        l_ref[i * sp:(i + 1) * sp, :] = jnp.where(low, lse[:sp], lse[sp:])
    for i, p in enumerate(probs):
        vprev = vp_ref[...] if i == 0 else v_ref[(i - 1) * sp:i * sp, :]
        o2 = _dot(p, jnp.concatenate([vprev, v_ref[i * sp:(i + 1) * sp, :]], axis=0))
        o_ref[i * sp:(i + 1) * sp, :] = jnp.where(low, o2[:sp], o2[sp:])


def _dilated(q, k, v, group):
    b, m, width = q.shape
    dilation = width // LANES
    tb = min(512, m)
    nsub = tb // DIL_SPAN
    cur = pl.BlockSpec((None, tb, LANES), lambda i, r, n: (i, n, r))
    prev = pl.BlockSpec((None, DIL_SPAN, LANES), lambda i, r, n: (i, jnp.maximum(n * nsub - 1, 0), r))
    return pl.pallas_call(
        functools.partial(_dilated_kernel, nsub=nsub),
        grid=(b, dilation, m // tb),
        in_specs=[cur, cur, cur, prev, prev],
        out_specs=[cur, cur],
        out_shape=[jax.ShapeDtypeStruct((b, m, width), F32)] * 2,
        compiler_params=pltpu.CompilerParams(dimension_semantics=("arbitrary",) * 3),
        name=f"dilated_g{group}",
    )(q, k, v, k, v)


def _dsa_kernel(shift_ref, qit_ref, wit_ref, ki_ref, qbt_ref, kb_ref, vta_ref, o_ref,
                hi_ref, lo_ref, sel_ref, fmax_ref, fmin_ref, blo_ref, tcnt_ref, tsel_ref, qic_ref, qbm_ref, m_ref, acc_ref,
                *, tq, ck, rb, topk):
    i = pl.program_id(1)
    nrows = (i + 1) * tq
    nchunk = lax.div(nrows + (ck - 1), ck)
    nblk = lax.div(nrows, rb)
    nblk_pad = nchunk * (ck // rb)

    rows = lax.broadcasted_iota(I32, (2 * HEAD_DIM, tq), 0)
    for j in range(DSA_Q_HEADS):
        g = j // DSA_GROUP
        q = qbt_ref[j * HEAD_DIM:(j + 1) * HEAD_DIM, :]
        q2 = jnp.concatenate([q, q], axis=0)
        own = (rows >= g * HEAD_DIM) & (rows < (g + 1) * HEAD_DIM)
        qbm_ref[:, j * tq:(j + 1) * tq] = jnp.where(own, q2, jnp.zeros_like(q2))
    for h in range(IDX_HEADS):
        qic_ref[:, h * tq:(h + 1) * tq] = qit_ref[h * HEAD_DIM:(h + 1) * HEAD_DIM, :]

    kpos = lax.broadcasted_iota(I32, (ck, tq), 0)
    qpos = i * tq + lax.broadcasted_iota(I32, (ck, tq), 1)

    def index_chunk(c, carry):
        off = pl.multiple_of(c * ck, ck)
        sc = _dot(ki_ref[pl.ds(off, ck), :], qic_ref[...])
        acc = None
        for h in range(IDX_HEADS):
            t = jnp.maximum(sc[:, h * tq:(h + 1) * tq], 0.0) * wit_ref[h:h + 1, :]
            acc = t if acc is None else acc + t
        acc = jnp.where(kpos + off <= qpos, acc, -jnp.inf)
        bits = lax.bitcast_convert_type(acc, I32)
        key = jnp.where(bits < 0, INT_MIN - bits, bits)
        hi_ref[pl.ds(off, ck), :] = lax.shift_right_arithmetic(key, 16).astype(I16)
        lo_ref[pl.ds(off, ck), :] = (key ^ 0x8000).astype(I16)
        return carry

    def for_each_chunk(fn):
        def pair(t, carry):
            fn(2 * t, carry)
            return fn(2 * t + 1, carry)
        lax.fori_loop(0, lax.div(nchunk, 2), pair, 0)

        @pl.when(lax.rem(nchunk, 2) == 1)
        def _():
            fn(nchunk - 1, 0)

    for_each_chunk(index_chunk)


    def count(refs, pred, nb=nblk, emit=None):
        def body(r, accs):
            base = pl.multiple_of(r * rb, rb)
            blks = [ref[pl.ds(base, rb), :] for ref in refs]
            if emit is not None:
                emit(blks, base)
            accs = list(accs)
            for u in range(rb // 16):
                hit = pred([blk[u * 16:(u + 1) * 16, :] for blk in blks], base + u * 16)
                accs[u % N_ACC] = accs[u % N_ACC] + jnp.where(hit, jnp.int16(1), jnp.int16(0))
            return tuple(accs)
        accs = lax.fori_loop(0, nb, body, tuple(jnp.zeros((16, tq), I16) for _ in range(N_ACC)))
        tot = accs[0].astype(I32)
        for a in accs[1:]:
            tot = tot + a.astype(I32)
        return jnp.sum(tot, axis=0, keepdims=True)

    def rows16(v):
        return jnp.broadcast_to(v, (16, tq)).astype(I16)

    def bisect(ref, want):
        def step(it, u):
            uc = u | lax.shift_left(jnp.int32(1), 15 - it)
            cand = rows16(uc - 32768)
            cnt = count([ref], lambda b, p0: b[0] >= cand)
            return jnp.where(cnt >= want, uc, u)
        return lax.fori_loop(0, 16, step, jnp.zeros((1, tq), I32)) - 32768

    a32 = jnp.maximum(bisect(hi_ref, topk), HI_NEG_INF)
    a16 = rows16(a32)
    a_blk = jnp.broadcast_to(a32, (rb, tq)).astype(I16)

    fmax_ref[...] = jnp.full(fmax_ref.shape, I16_MIN, I16)
    fmin_ref[...] = jnp.full(fmin_ref.shape, I16_MIN, I16)

    def fold_block(r, carry):
        accs, most = carry
        base = pl.multiple_of(r * rb, rb)
        hi, lo = hi_ref[pl.ds(base, rb), :], lo_ref[pl.ds(base, rb), :]
        top = jnp.full((16, tq), I16_MIN, I16)
        low = jnp.full((16, tq), I16_MAX, I16)
        members = jnp.zeros((16, tq), I16)
        accs = list(accs)
        for u in range(rb // 16):
            h, l = hi[u * 16:(u + 1) * 16, :], lo[u * 16:(u + 1) * 16, :]
            member = h == a16
            up, down = jnp.where(member, l, jnp.int16(I16_MIN)), jnp.where(member, l, jnp.int16(I16_MAX))
            top = jnp.where(up > top, up, top)
            low = jnp.where(down < low, down, low)
            members = members + jnp.where(member, jnp.int16(1), jnp.int16(0))
            accs[u % N_ACC] = accs[u % N_ACC] + jnp.where(h > a16, jnp.int16(1), jnp.int16(0))
        row = pl.multiple_of(r * 16, 16)
        fmax_ref[pl.ds(row, 16), :] = top
        fmin_ref[pl.ds(row, 16), :] = jnp.where(members >= 2, low, jnp.int16(I16_MIN))
        return tuple(accs), jnp.where(members > most, members, most)

    accs, most = lax.fori_loop(0, nblk, fold_block, (tuple(jnp.zeros((16, tq), I16) for _ in range(N_ACC)),
                                                     jnp.zeros((16, tq), I16)))
    n_hi = accs[0].astype(I32)
    for a in accs[1:]:
        n_hi = n_hi + a.astype(I32)
    n_hi = jnp.sum(n_hi, axis=0, keepdims=True)
    want_lo = topk - n_hi

    def folded_step(it, u):
        uc = u | lax.shift_left(jnp.int32(1), 15 - it)
        cand = rows16(uc - 32768)
        accs = [jnp.zeros((16, tq), I16) for _ in range(N_ACC)]
        for r in range(fmax_ref.shape[0] // 16):
            for k, ref in enumerate((fmax_ref, fmin_ref)):
                hit = ref[r * 16:(r + 1) * 16, :] >= cand
                accs[(2 * r + k) % N_ACC] = accs[(2 * r + k) % N_ACC] + jnp.where(hit, jnp.int16(1), jnp.int16(0))
        tot = accs[0].astype(I32)
        for a in accs[1:]:
            tot = tot + a.astype(I32)
        return jnp.where(jnp.sum(tot, axis=0, keepdims=True) >= want_lo, uc, u)

    blo_ref[...] = jnp.broadcast_to(lax.fori_loop(0, 16, folded_step, jnp.zeros((1, tq), I32)) - 32768, blo_ref.shape)

    @pl.when(jnp.max(most.astype(I32)) > FOLD_MAX_MEMBERS)
    def _():
        def bucket(r, carry):
            base = pl.multiple_of(r * rb, rb)
            sel_ref[pl.ds(base, rb), :] = jnp.where(hi_ref[pl.ds(base, rb), :] == a_blk, lo_ref[pl.ds(base, rb), :],
                                                    jnp.int16(I16_MIN))
            return carry
        lax.fori_loop(0, nblk, bucket, 0)
        blo_ref[...] = jnp.broadcast_to(bisect(sel_ref, want_lo), blo_ref.shape)

    b32 = blo_ref[0:1, :]
    b32 = jnp.where(a32 == HI_NEG_INF, jnp.maximum(b32, I16_MIN + 1), b32)
    b16 = rows16(b32)
    b_blk = jnp.broadcast_to(b32, (rb, tq)).astype(I16)

    def emit_selection(blks, base):
        sel = (blks[0] > a_blk) | ((blks[0] == a_blk) & (blks[1] >= b_blk))
        sel_ref[pl.ds(base, rb), :] = jnp.where(sel, jnp.int16(BF16_ONE_BITS), jnp.int16(0))

    n_ge = count([hi_ref, lo_ref], lambda b, p0: (b[0] > a16) | ((b[0] == a16) & (b[1] >= b16)),
                 nb=nblk_pad, emit=emit_selection)

    need = n_ge > topk

    @pl.when(jnp.max(jnp.where(need, 1, 0)) > 0)
    def _():
        nb_max = tcnt_ref.shape[0]
        none = jnp.int32(2 ** 30)

        def tied(hi, lo, a, b):
            return (hi == a) & (lo == b)

        tcnt_ref[...] = jnp.zeros(tcnt_ref.shape, I32)

        def tie_block(r, carry):
            base = pl.multiple_of(r * rb, rb)
            hi, lo = hi_ref[pl.ds(base, rb), :], lo_ref[pl.ds(base, rb), :]
            acc = jnp.zeros((16, tq), I16)
            for u in range(rb // 16):
                acc = acc + jnp.where(tied(hi[u * 16:(u + 1) * 16, :], lo[u * 16:(u + 1) * 16, :], a16, b16),
                                      jnp.int16(1), jnp.int16(0))
            tcnt_ref[r] = jnp.broadcast_to(jnp.sum(acc.astype(I32), axis=0, keepdims=True), (8, tq))
            return carry

        lax.fori_loop(0, nblk, tie_block, 0)

        per_block = [tcnt_ref[r][0:1, :] for r in range(nb_max)]
        n_tied = per_block[0]
        for c in per_block[1:]:
            n_tied = n_tied + c
        want = jnp.where(need, topk - (n_ge - n_tied), none)
        last = jnp.full((1, tq), nb_max, I32)
        before = jnp.zeros((1, tq), I32)
        prefix = jnp.zeros((1, tq), I32)
        for r, c in enumerate(per_block):
            reached = (last == nb_max) & (prefix + c >= want)
            last = jnp.where(reached, r, last)
            before = jnp.where(reached, prefix, before)
            prefix = prefix + c
        want_here = want - before

        tsel_ref[...] = jnp.zeros(tsel_ref.shape, I16)

        def pick_block(r, carry):
            base = pl.multiple_of(r * rb, rb)
            here = jnp.broadcast_to(jnp.where(last == r, 1, 0), (rb, tq)).astype(I16) != 0
            hit = tied(hi_ref[pl.ds(base, rb), :], lo_ref[pl.ds(base, rb), :], a_blk, b_blk) & here
            tsel_ref[...] = tsel_ref[...] + jnp.where(hit, jnp.int16(1), jnp.int16(0))
            return carry

        lax.fori_loop(0, nblk, pick_block, 0)

        in_block = lax.broadcasted_iota(I32, (rb, tq), 0).astype(I16)
        block_bits = (rb - 1).bit_length()

        def row_step(it, e):
            ec = e | lax.shift_left(jnp.int32(1), block_bits - 1 - it)
            hit = (tsel_ref[...] != 0) & (in_block < jnp.broadcast_to(ec, (rb, tq)).astype(I16))
            ones = jnp.where(hit, jnp.int16(1), jnp.int16(0))
            acc = ones[0:16, :]
            for u in range(1, rb // 16):
                acc = acc + ones[u * 16:(u + 1) * 16, :]
            below = jnp.sum(acc.astype(I32), axis=0, keepdims=True)
            return jnp.where(below < want_here, ec, e)

        e_last = lax.fori_loop(0, block_bits, row_step, jnp.zeros((1, tq), I32))
        x = jnp.where(last < nb_max, jnp.minimum(last * rb + e_last, I16_MAX), I16_MAX)

        def demote(r, carry):
            base = pl.multiple_of(r * rb, rb)
            pos = (base + lax.broadcasted_iota(I32, (rb, tq), 0)).astype(I16)
            drop = (tied(hi_ref[pl.ds(base, rb), :], lo_ref[pl.ds(base, rb), :], a_blk, b_blk)
                    & (pos > jnp.broadcast_to(x, (rb, tq)).astype(I16)))
            sel_ref[pl.ds(base, rb), :] = jnp.where(drop, jnp.int16(0), sel_ref[pl.ds(base, rb), :])
            return carry

        lax.fori_loop(0, nblk, demote, 0)

    shift = shift_ref[0]

    @pl.when(shift < SAFE_SHIFT_LOG2)
    def _():
        acc_ref[...] = jnp.zeros(acc_ref.shape, F32)

        def attend_chunk(c, carry):
            off = pl.multiple_of(c * ck, ck)
            kch = kb_ref[pl.ds(off, ck), :]
            msk = lax.bitcast_convert_type(sel_ref[pl.ds(off, ck), :], BF16)
            for g in range(DSA_KV_HEADS):
                sc = _dot(kch, qbm_ref[:, g * DSA_GROUP * tq:(g + 1) * DSA_GROUP * tq])
                p = jnp.exp2(sc - shift).astype(BF16) * _tile_lanes(msk, DSA_GROUP)
                acc_ref[g] += _dot(vta_ref[g * V_AUG:(g + 1) * V_AUG, pl.ds(off, ck)], p)
            return carry

        for_each_chunk(attend_chunk)

    @pl.when(shift >= SAFE_SHIFT_LOG2)
    def _():
        m_ref[...] = jnp.full(m_ref.shape, NEG_BIG, F32)
        acc_ref[...] = jnp.zeros(acc_ref.shape, F32)

        def attend_chunk(c, carry):
            off = pl.multiple_of(c * ck, ck)
            kch = kb_ref[pl.ds(off, ck), :]
            sel = sel_ref[pl.ds(off, ck), :].astype(I32) != 0
            for j in range(DSA_Q_HEADS):
                g = j // DSA_GROUP
                cols = slice((j % DSA_GROUP) * tq, (j % DSA_GROUP + 1) * tq)
                sc = jnp.where(sel, _dot(kch, qbm_ref[:, j * tq:(j + 1) * tq]), NEG_BIG)
                m_prev = m_ref[j]
                m_new = jnp.maximum(m_prev, jnp.max(sc, axis=0, keepdims=True))
                p = jnp.where(sel, jnp.exp2(sc - m_new), 0.0).astype(BF16)
                acc_ref[g, :, cols] = jnp.exp2(m_prev - m_new) * acc_ref[g, :, cols] + _dot(
                    vta_ref[g * V_AUG:(g + 1) * V_AUG, pl.ds(off, ck)], p)
                m_ref[j] = m_new
            return carry

        lax.fori_loop(0, nchunk, attend_chunk, 0)

    for pair in range(DSA_Q_HEADS // 2):
        halves = []
        for j in (2 * pair, 2 * pair + 1):
            a = acc_ref[j // DSA_GROUP, :, (j % DSA_GROUP) * tq:(j % DSA_GROUP + 1) * tq]
            halves.append(a[:HEAD_DIM] / a[HEAD_DIM:HEAD_DIM + 1])
        o_ref[:, pair * LANES:(pair + 1) * LANES] = jnp.concatenate(halves, axis=0).T.astype(BF16)


def _dsa(shift, qit, wit, ki, qbt, kb, vta):
    b, _, s = qbt.shape
    tq = min(256, s)
    ck = min(512, s)
    rb = 256
    topk = min(DSA_TOPK_MAX, s // 4)
    qt = lambda w: pl.BlockSpec((None, w, tq), lambda i, j: (i, 0, j))
    seq = lambda w: pl.BlockSpec((None, s, w), lambda i, j: (i, 0, 0))
    return pl.pallas_call(
        functools.partial(_dsa_kernel, tq=tq, ck=ck, rb=rb, topk=topk),
        grid=(b, s // tq),
        in_specs=[pl.BlockSpec(memory_space=pltpu.SMEM), qt(D_QI), qt(IDX_HEADS), seq(HEAD_DIM), qt(D_QB), seq(D_KVB),
                  pl.BlockSpec((None, DSA_KV_HEADS * V_AUG, s), lambda i, j: (i, 0, 0))],
        out_specs=pl.BlockSpec((None, tq, D_QB), lambda i, j: (i, j, 0)),
        out_shape=jax.ShapeDtypeStruct((b, s, D_QB), BF16),
        scratch_shapes=[
            pltpu.VMEM((s, tq), I16),
            pltpu.VMEM((s, tq), I16),
            pltpu.VMEM((s, tq), I16),
            pltpu.VMEM((s // rb * 16, tq), I16),
            pltpu.VMEM((s // rb * 16, tq), I16),
            pltpu.VMEM((8, tq), I32),
            pltpu.VMEM((s // rb, 8, tq), I32),
            pltpu.VMEM((rb, tq), I16),
            pltpu.VMEM((HEAD_DIM, IDX_HEADS * tq), BF16),
            pltpu.VMEM((2 * HEAD_DIM, DSA_Q_HEADS * tq), BF16),
            pltpu.VMEM((DSA_Q_HEADS, 1, tq), F32),
            pltpu.VMEM((DSA_KV_HEADS, V_AUG, DSA_GROUP * tq), F32),
        ],
        compiler_params=pltpu.CompilerParams(dimension_semantics=("arbitrary", "arbitrary"),
                                             vmem_limit_bytes=VMEM_LIMIT),
        name="dsa",
    )(shift, qit, wit, ki, qbt, kb, vta)


def _merge_kernel(x_ref, g_ref, wg_ref, o0_ref, l0_ref, o1_ref, l1_ref, o2_ref, l2_ref, ob_ref, qc_ref,
                  km_ref, vm_ref, wa_ref, wb_ref, wc_ref, wo_ref, out_ref, *stage_refs):
    tm, d = x_ref.shape

    def token_major(ref, stage_ref):
        dil = ref.shape[1] // LANES
        if dil == 1:
            return ref[...]
        for r in range(dil):
            stage_ref[pl.ds(r, tm // dil, stride=dil), :] = ref[:, r * LANES:(r + 1) * LANES]
        return stage_ref[...]

    lses = (l0_ref[...], token_major(l1_ref, stage_refs[0]), token_major(l2_ref, stage_refs[1]))
    outs = (o0_ref[...], token_major(o1_ref, stage_refs[2]), token_major(o2_ref, stage_refs[3]))

    def rows_part(rows):
        n = rows.stop - rows.start
        x = x_ref[rows, :]
        h = _rms_rows(x, g_ref[...]).astype(BF16)

        def gate(k):
            z = _dot(h, wg_ref[:, k * d:(k + 1) * d])
            return 1.0 / (1.0 + jnp.exp(-z))

        l0, l1, l2 = (t[rows, :] for t in lses)
        o0, o1, o2 = (t[rows, :] for t in outs)
        mx = jnp.maximum(jnp.maximum(l0, l1), l2)
        e0, e1, e2 = jnp.exp(l0 - mx), jnp.exp(l1 - mx), jnp.exp(l2 - mx)
        oa = (e0 * o0 + e1 * o1 + e2 * o2) / (e0 + e1 + e2)
        merged = gate(0) * _dot(oa.astype(BF16), wa_ref[...])

        merged = merged + gate(1) * _dot(ob_ref[rows, :], wb_ref[...])

        low = _low_half((n, LANES))
        cols = []
        for mcol in range(MEM_HEADS // 2):
            q = qc_ref[rows, mcol * LANES:(mcol + 1) * LANES]
            km = km_ref[:, mcol * LANES:(mcol + 1) * LANES]
            vm = vm_ref[:, mcol * LANES:(mcol + 1) * LANES]
            zero = jnp.zeros_like(q)
            heads = []
            for qh in (jnp.where(low, q, zero), jnp.where(low, zero, q)):
                s = _dot_nt(qh, km)
                e = jnp.exp(s - jnp.max(s, axis=-1, keepdims=True))
                p = e / jnp.sum(e, axis=-1, keepdims=True)
                heads.append(_dot(p.astype(BF16), vm))
            cols.append(jnp.where(low, heads[0], heads[1]))
        oc = jnp.concatenate(cols, axis=1)
        merged = merged + gate(2) * _dot(oc.astype(BF16), wc_ref[...])

        out_ref[rows, :] = x + _dot(merged.astype(BF16), wo_ref[...])

    parts = MERGE_ROW_PARTS if tm % (8 * MERGE_ROW_PARTS) == 0 else 1
    for part in range(parts):
        rows_part(slice(part * (tm // parts), (part + 1) * (tm // parts)))


def _merge(x, g_mix, w_gate, dil, ob, qc, km, vm, w_a, w_b, w_c, w_o, tm):
    b, s, d = x.shape
    mlen = km.shape[1]
    tok = lambda w: pl.BlockSpec((None, tm, w), lambda i, j: (i, j, 0))
    full = lambda *shape: pl.BlockSpec(shape, lambda i, j: (0,) * len(shape))
    memb = pl.BlockSpec((None, mlen, D_QC), lambda i, j: (i, 0, 0))
    dil_args = [t for pair in dil for t in pair]
    dil_specs = [pl.BlockSpec((None, tm // dl, dl * LANES), lambda i, j: (i, j, 0))
                 for _, dl in DIL_GROUPS for _ in range(2)]
    return pl.pallas_call(
        _merge_kernel,
        grid=(b, s // tm),
        in_specs=[tok(d), full(1, d), full(d, 3 * d)] + dil_specs + [tok(D_QB), tok(D_QC), memb, memb,
                  full(LANES, d), full(D_QB, d), full(D_QC, d), full(d, d)],
        out_specs=tok(d),
        out_shape=jax.ShapeDtypeStruct((b, s, d), F32),
        scratch_shapes=[pltpu.VMEM((tm, LANES), F32)] * 4,
        compiler_params=pltpu.CompilerParams(dimension_semantics=("arbitrary", "arbitrary"),
                                             vmem_limit_bytes=VMEM_LIMIT),
        name="merge",
    )(x, g_mix, w_gate, *dil_args, ob, qc, km, vm, w_a, w_b, w_c, w_o)


def _mlp_kernel(x_ref, g_ref, w1_ref, w2_ref, out_ref, *, fchunk):
    x = x_ref[...]
    h = _rms_rows(x, g_ref[...]).astype(BF16)
    acc = x
    for c in range(w1_ref.shape[1] // fchunk):
        u = jnp.maximum(_dot(h, w1_ref[:, c * fchunk:(c + 1) * fchunk]), 0.0)
        acc = acc + _dot((u * u).astype(BF16), w2_ref[c * fchunk:(c + 1) * fchunk, :])
    out_ref[...] = acc


def _mlp(x, g_mlp, w_1, w_2, tm):
    b, s, d = x.shape
    f = w_1.shape[1]
    tok = pl.BlockSpec((None, tm, d), lambda i, j: (i, j, 0))
    full = lambda *shape: pl.BlockSpec(shape, lambda i, j: (0,) * len(shape), pipeline_mode=pl.Buffered(1))
    return pl.pallas_call(
        functools.partial(_mlp_kernel, fchunk=min(1024, f)),
        grid=(b, s // tm),
        in_specs=[tok, full(1, d), full(d, f), full(f, d)],
        out_specs=tok,
        out_shape=jax.ShapeDtypeStruct((b, s, d), F32),
        compiler_params=pltpu.CompilerParams(dimension_semantics=("arbitrary", "arbitrary"),
                                             vmem_limit_bytes=VMEM_LIMIT),
        name="mlp",
    )(x, g_mlp, w_1, w_2)


def _rotary_tables(positions):
    inv = jnp.power(jnp.float32(ROPE_THETA), -jnp.arange(ROT_HALF, dtype=F32) / ROT_HALF)
    ang = positions.astype(F32)[..., None] * inv
    cos, sin = jnp.cos(ang), jnp.sin(ang)
    rest = HEAD_DIM - 2 * ROT_HALF
    ones = jnp.ones(cos.shape[:-1] + (rest,), F32)
    zeros = jnp.zeros(cos.shape[:-1] + (rest,), F32)
    z8 = jnp.zeros_like(sin)
    head = lambda parts: jnp.tile(jnp.concatenate(parts, axis=-1), (1, 1, LANES // HEAD_DIM))
    return (head([cos, cos, ones]), head([-sin, z8, zeros]), head([z8, sin, zeros]),
            cos.transpose(0, 2, 1), sin.transpose(0, 2, 1))


def _block_diag_mean(width):
    r = jnp.arange(width) // HEAD_DIM
    return jnp.where(r[:, None] == r[None, :], 1.0 / HEAD_DIM, 0.0).astype(BF16)


def _layer(x, mem, tables, g_mix, g_mem, w_in, g_qa, g_ka, g_qb, g_kb, g_qc, g_kc,
           w_mem_kv, w_a, w_b, w_c, w_o, g_mlp, w_1, w_2):
    b, s, d = x.shape
    tm = min(256, s)
    tm_wide = min(512, s)
    cos, slo, shi, cos_t, sin_t = tables
    bd = _block_diag_mean(D_QA)
    scale = HEAD_DIM ** -0.5

    offs, acc = [], 0
    for w in (D_QA, D_QA, D_QA, D_QB, D_KVB, D_KVB, D_QI, HEAD_DIM, IDX_HEADS, D_QC):
        offs.append((acc, acc + w))
        acc += w
    seg = lambda k: w_in[:, offs[k][0]:offs[k][1]]
    w_std = jnp.concatenate([seg(0), seg(1), seg(2), seg(4), seg(7), seg(7), seg(9)], axis=1).astype(BF16)
    w_t = jnp.concatenate([seg(3), seg(6), seg(5), seg(8), jnp.zeros((d, _R_END - _R_WI - IDX_HEADS), w_in.dtype)],
                          axis=1).T.astype(BF16)
    w_gate = w_in[:, acc:].astype(BF16)

    tile6 = lambda g: jnp.tile(g, D_QA // HEAD_DIM)
    hg = jnp.stack([tile6(g_qa) * scale, tile6(g_ka), tile6(g_kb), tile6(g_qc) * scale,
                    jnp.zeros(D_QA), jnp.zeros(D_QA), jnp.zeros(D_QA), jnp.zeros(D_QA)]).astype(F32)
    gqt = jnp.broadcast_to((g_qb * (scale * LOG2E))[:, None], (HEAD_DIM, LANES)).astype(F32)
    shift = (HEAD_DIM * scale * LOG2E * 1.02) * jnp.max(jnp.abs(g_qb)) * jnp.max(jnp.abs(g_kb))
    shift = jnp.reshape(shift, (1,)).astype(F32)

    km, vm = _memkv(mem, g_mem[None, :], w_mem_kv.astype(BF16), bd[:D_QC, :D_QC],
                    jnp.tile(g_kc, MEM_HEADS)[None, :])
    (q0, q1, q2, k0, k1, k2, v0, v1, v2, kb, ki, qc, qbt, qit, vta, wit) = _inproj(
        x, g_mix[None, :], w_std, w_t, cos, slo, shi, cos_t, sin_t, bd, hg, gqt, tm_wide)
    dil = [_dilated(q, k, v, g) for g, (q, k, v) in enumerate(((q0, k0, v0), (q1, k1, v1), (q2, k2, v2)))]
    ob = _dsa(shift, qit, wit, ki, qbt, kb, vta)
    x = _merge(x, g_mix[None, :], w_gate, dil, ob, qc, km, vm, w_a.astype(BF16), w_b.astype(BF16),
               w_c.astype(BF16), w_o.astype(BF16), tm_wide)
    return _mlp(x, g_mlp[None, :], w_1.astype(BF16), w_2.astype(BF16), tm_wide)


def kernel(x, mem, positions, g_mix, g_mem, w_in, g_qa, g_ka, g_qb, g_kb, g_qc, g_kc, w_mem_kv, w_a, w_b, w_c, w_o, g_mlp, w_1, w_2):
    tables = _rotary_tables(positions)
    for i in range(g_mix.shape[0]):
        x = _layer(x, mem, tables, g_mix[i], g_mem[i], w_in[i], g_qa[i], g_ka[i], g_qb[i], g_kb[i], g_qc[i],
                   g_kc[i], w_mem_kv[i], w_a[i], w_b[i], w_c[i], w_o[i], g_mlp[i], w_1[i], w_2[i])
    return x
```

```python
import functools
import math

import jax
import jax.numpy as jnp
from jax import lax
from jax.experimental import pallas as pl
from jax.experimental.pallas import tpu as pltpu

F32 = jnp.float32
BF16 = jnp.bfloat16
I32 = jnp.int32
I16 = jnp.int16

LANES = 128
HEAD_DIM = 64
ROT_HALF = 8
ROPE_THETA = 500000.0
EPS = 1e-6
DIL_GROUPS = ((128, 1), (512, 4), (2048, 16))
DIL_SPAN = 128
N_DIL_HEADS = 6
DSA_Q_HEADS = 6
DSA_KV_HEADS = 2
DSA_GROUP = DSA_Q_HEADS // DSA_KV_HEADS
DSA_TOPK_MAX = 256
IDX_HEADS = 8
MEM_HEADS = 4
D_QA = N_DIL_HEADS * HEAD_DIM
D_QB = DSA_Q_HEADS * HEAD_DIM
D_KVB = DSA_KV_HEADS * HEAD_DIM
D_QI = IDX_HEADS * HEAD_DIM
D_QC = MEM_HEADS * HEAD_DIM
V_AUG = 80
LOG2E = math.log2(math.e)
NEG_BIG = -1e30
INT_MIN = -2147483648
I16_MIN = -32768
I16_MAX = 32767
FOLD_MAX_MEMBERS = 2
BF16_ONE_BITS = 0x3F80
HI_NEG_INF = -32640
SAFE_SHIFT_LOG2 = 55.0
SHIFT_MARGIN = 1.02
TOKEN_TILE = 512
N_ACC = 4
MERGE_ROW_PARTS = 2
VMEM_LIMIT = 56 * 1024 * 1024

_CONTRACT_LAST = (((1,), (1,)), ((), ()))


def _dot(a, b):
    return jnp.dot(a, b, preferred_element_type=F32)


def _dot_nt(a, b):
    return lax.dot_general(a, b, _CONTRACT_LAST, preferred_element_type=F32)


def _tile_lanes(a, reps):
    return a if reps == 1 else jnp.concatenate([a] * reps, axis=1)


def _low_half(shape):
    return (lax.broadcasted_iota(I32, shape, 1) % LANES) < HEAD_DIM


def _rms_rows(x, g):
    ms = jnp.mean(x * x, axis=-1, keepdims=True)
    return x * lax.rsqrt(ms + EPS) * g


def _norm_heads(p, bd, gain):
    sq = p * p
    hi = sq.astype(BF16)
    lo = (sq - hi.astype(F32)).astype(BF16)
    ms = _dot(hi, bd) + _dot(lo, bd)
    return p * lax.rsqrt(ms + EPS) * gain


def _rotary(y, cos, sin_lo, sin_hi):
    w = y.shape[1]
    reps = w // LANES
    c = _tile_lanes(cos, reps)
    a = _tile_lanes(sin_lo, reps)
    b = _tile_lanes(sin_hi, reps)
    return y * c + pltpu.roll(y, w - ROT_HALF, 1) * a + pltpu.roll(y, ROT_HALF, 1) * b


def _rotary_t(blk, cos_t, sin_t):
    x1, x2 = blk[:ROT_HALF], blk[ROT_HALF:2 * ROT_HALF]
    return jnp.concatenate([x1 * cos_t - x2 * sin_t, x2 * cos_t + x1 * sin_t, blk[2 * ROT_HALF:]], axis=0)


def _memkv_kernel(mem_ref, g_ref, w_ref, bd_ref, gk_ref, k_ref, v_ref):
    h = _rms_rows(mem_ref[...], g_ref[...]).astype(BF16)
    kv = _dot(h, w_ref[...])
    k_ref[...] = _norm_heads(kv[:, :D_QC], bd_ref[...], gk_ref[...]).astype(BF16)
    v_ref[...] = kv[:, D_QC:].astype(BF16)


def _memkv(mem, g_mem, w_mem_kv, bd, gk):
    b, m, d = mem.shape
    full = lambda *shape: pl.BlockSpec(shape, lambda i: (0,) * len(shape))
    return pl.pallas_call(
        _memkv_kernel,
        grid=(b,),
        in_specs=[pl.BlockSpec((None, m, d), lambda i: (i, 0, 0)), full(1, d), full(d, 2 * D_QC),
                  full(D_QC, D_QC), full(1, D_QC)],
        out_specs=[pl.BlockSpec((None, m, D_QC), lambda i: (i, 0, 0))] * 2,
        out_shape=[jax.ShapeDtypeStruct((b, m, D_QC), BF16)] * 2,
        name="memkv",
    )(mem, g_mem, w_mem_kv, bd, gk)


_C_QA, _C_KA, _C_VA, _C_KB, _C_KI, _C_QC, _C_END = 0, 384, 768, 1152, 1280, 1408, 1664
_R_QB, _R_QI, _R_VB, _R_WI, _R_END = 0, 384, 896, 1024, 1040


def _inproj_kernel(x_ref, g_ref, w_ref, wt_ref, cost_ref, sint_ref, bd_ref, hg_ref, gqt_ref,
                   q0_ref, q1_ref, q2_ref, k0_ref, k1_ref, k2_ref, v0_ref, v1_ref, v2_ref,
                   kb_ref, ki_ref, qc_ref, qbt_ref, qit_ref, vta_ref, wit_ref, dil_ref):
    tm = x_ref.shape[0]
    h = _rms_rows(x_ref[...], g_ref[...]).astype(BF16)
    bd = bd_ref[...]

    cos_t, sin_t = cost_ref[...], sint_ref[...]
    rest = HEAD_DIM - 2 * ROT_HALF
    one_r, zero_r, zero_h = jnp.ones((rest, tm), F32), jnp.zeros((rest, tm), F32), jnp.zeros((ROT_HALF, tm), F32)
    cos = jnp.concatenate([cos_t, cos_t, one_r] * (LANES // HEAD_DIM), axis=0).T
    slo = jnp.concatenate([-sin_t, zero_h, zero_r] * (LANES // HEAD_DIM), axis=0).T
    shi = jnp.concatenate([zero_h, sin_t, zero_r] * (LANES // HEAD_DIM), axis=0).T

    def proj(a, b):
        return _dot(h, w_ref[:, a:b])

    def norm_rot(a, b, gain_row):
        w = b - a
        y = _norm_heads(proj(a, b), bd[:w, :w], hg_ref[gain_row:gain_row + 1, :w])
        return _rotary(y, cos, slo, shi)

    def store_dilated(y, out_refs):
        for g, ((_, dil), out_ref) in enumerate(zip(DIL_GROUPS, out_refs)):
            yg = y[:, g * LANES:(g + 1) * LANES]
            if dil == 1:
                out_ref[...] = yg.astype(BF16)
            else:
                dil_ref[...] = yg
                for r in range(dil):
                    out_ref[:, r * LANES:(r + 1) * LANES] = dil_ref[pl.ds(r, tm // dil, stride=dil), :].astype(BF16)

    store_dilated(norm_rot(_C_QA, _C_KA, 0), (q0_ref, q1_ref, q2_ref))
    store_dilated(norm_rot(_C_KA, _C_VA, 1), (k0_ref, k1_ref, k2_ref))
    store_dilated(proj(_C_VA, _C_KB), (v0_ref, v1_ref, v2_ref))
    kb_ref[...] = norm_rot(_C_KB, _C_KI, 2).astype(BF16)
    ki_ref[...] = _rotary(proj(_C_KI, _C_QC), cos, slo, shi)[:, :HEAD_DIM].astype(BF16)
    qc_ref[...] = _norm_heads(proj(_C_QC, _C_END), bd[:D_QC, :D_QC], hg_ref[3:4, :D_QC]).astype(BF16)

    pt = _dot_nt(wt_ref[...], h)
    gq =_tile_lanes(gqt_ref[...], tm // LANES)
    for hd in range(DSA_Q_HEADS):
        blk = pt[_R_QB + hd * HEAD_DIM:_R_QB + (hd + 1) * HEAD_DIM]
        ms = jnp.mean(blk * blk, axis=0, keepdims=True)
        qbt_ref[hd * HEAD_DIM:(hd + 1) * HEAD_DIM, :] = _rotary_t(blk * lax.rsqrt(ms + EPS) * gq, cos_t, sin_t).astype(BF16)
    for hd in range(IDX_HEADS):
        blk = pt[_R_QI + hd * HEAD_DIM:_R_QI + (hd + 1) * HEAD_DIM]
        qit_ref[hd * HEAD_DIM:(hd + 1) * HEAD_DIM, :] = _rotary_t(blk, cos_t, sin_t).astype(BF16)
    vt = pt[_R_VB:_R_WI].astype(BF16)
    pad = jnp.where(lax.broadcasted_iota(I32, (V_AUG - HEAD_DIM, tm), 0) == 0, 1.0, 0.0).astype(BF16)
    vta_ref[...] = jnp.concatenate([vt[:HEAD_DIM], pad, vt[HEAD_DIM:], pad], axis=0)
    wit_ref[...] = pt[_R_WI:_R_WI + IDX_HEADS] * ((IDX_HEADS ** -0.5) * (HEAD_DIM ** -0.5))


def _inproj(x, g_mix, w_std, w_t, cos_t, sin_t, bd, hg, gqt, tm):
    b, s, d = x.shape
    tok = lambda w: pl.BlockSpec((None, tm, w), lambda i, j: (i, j, 0))
    tok_t = lambda r: pl.BlockSpec((None, r, tm), lambda i, j: (i, 0, j))
    full = lambda *shape: pl.BlockSpec(shape, lambda i, j: (0,) * len(shape))
    dils = [dl for _, dl in DIL_GROUPS] * 3
    std = ((D_KVB, BF16), (HEAD_DIM, BF16), (D_QC, BF16))
    tr = ((D_QB, BF16), (D_QI, BF16), (DSA_KV_HEADS * V_AUG, BF16), (IDX_HEADS, F32))
    return pl.pallas_call(
        _inproj_kernel,
        grid=(b, s // tm),
        in_specs=[tok(d), full(1, d), full(d, _C_END), full(_R_END, d),
                  tok_t(ROT_HALF), tok_t(ROT_HALF), full(D_QA, D_QA), full(8, D_QA), full(HEAD_DIM, LANES)],
        out_specs=[pl.BlockSpec((None, tm // dl, dl * LANES), lambda i, j: (i, j, 0)) for dl in dils]
        + [tok(w) for w, _ in std] + [tok_t(r) for r, _ in tr],
        out_shape=[jax.ShapeDtypeStruct((b, s // dl, dl * LANES), BF16) for dl in dils]
        + [jax.ShapeDtypeStruct((b, s, w), t) for w, t in std]
        + [jax.ShapeDtypeStruct((b, r, s), t) for r, t in tr],
        scratch_shapes=[pltpu.VMEM((tm, LANES), F32)],
        compiler_params=pltpu.CompilerParams(dimension_semantics=("arbitrary", "arbitrary"),
                                             vmem_limit_bytes=VMEM_LIMIT),
        name="inproj",
    )(x, g_mix, w_std, w_t, cos_t, sin_t, bd, hg, gqt)


def _dilated_kernel(q_ref, k_ref, v_ref, kp_ref, vp_ref, o_ref, l_ref, *, nsub):
    n = pl.program_id(2)
    sp = DIL_SPAN
    low = _low_half((sp, LANES))
    rq = lax.broadcasted_iota(I32, (2 * sp, 2 * sp), 0) % sp
    kj = lax.broadcasted_iota(I32, (2 * sp, 2 * sp), 1)
    dist = sp + rq - kj
    band = (dist >= 0) & (dist <= sp)
    kmin = jnp.where(n > 0, 0, sp)
    scores = []
    for i in range(nsub):
        q = q_ref[i * sp:(i + 1) * sp, :]
        kprev = kp_ref[...] if i == 0 else k_ref[(i - 1) * sp:i * sp, :]
        mask = band & (kj >= kmin) if i == 0 else band
        kk = jnp.concatenate([kprev, k_ref[i * sp:(i + 1) * sp, :]], axis=0)
        zero = jnp.zeros_like(q)
        qs = jnp.concatenate([jnp.where(low, q, zero), jnp.where(low, zero, q)], axis=0)
        scores.append(jnp.where(mask, _dot_nt(qs, kk), -jnp.inf))
    probs = []
    for i, s in enumerate(scores):
        m = jnp.max(s, axis=-1, keepdims=True)
        e = jnp.exp(s - m)
        den = jnp.sum(e, axis=-1, keepdims=True)
        probs.append((e / den).astype(BF16))
        lse = jnp.broadcast_to(m + jnp.log(den), (2 * sp, LANES))
        l_ref[i * sp:(i + 1) * sp, :] = jnp.where(low, lse[:sp], lse[sp:])
    for i, p in enumerate(probs):
        vprev = vp_ref[...] if i == 0 else v_ref[(i - 1) * sp:i * sp, :]
        o2 = _dot(p, jnp.concatenate([vprev, v_ref[i * sp:(i + 1) * sp, :]], axis=0))
        o_ref[i * sp:(i + 1) * sp, :] = jnp.where(low, o2[:sp], o2[sp:])


def _dilated(q, k, v, group):
    b, m, width = q.shape
    dilation = width // LANES
    tb = min(512, m)
    nsub = tb // DIL_SPAN
    cur = pl.BlockSpec((None, tb, LANES), lambda i, r, n: (i, n, r))
    prev = pl.BlockSpec((None, DIL_SPAN, LANES), lambda i, r, n: (i, jnp.maximum(n * nsub - 1, 0), r))
    return pl.pallas_call(
        functools.partial(_dilated_kernel, nsub=nsub),
        grid=(b, dilation, m // tb),
        in_specs=[cur, cur, cur, prev, prev],
        out_specs=[cur, cur],
        out_shape=[jax.ShapeDtypeStruct((b, m, width), F32)] * 2,
        compiler_params=pltpu.CompilerParams(dimension_semantics=("arbitrary",) * 3),
        name=f"dilated_g{group}",
    )(q, k, v, k, v)


def _dsa_kernel(shift_ref, qit_ref, wit_ref, ki_ref, qbt_ref, kb_ref, vta_ref, o_ref,
                hi_ref, lo_ref, sel_ref, fmax_ref, fmin_ref, blo_ref, tcnt_ref, tsel_ref, qic_ref, qbm_ref, m_ref, acc_ref,
                *, tq, ck, rb, topk):
    i = pl.program_id(1)
    nrows = (i + 1) * tq
    nchunk = lax.div(nrows + (ck - 1), ck)
    nblk = lax.div(nrows, rb)
    nblk_pad = nchunk * (ck // rb)

    rows = lax.broadcasted_iota(I32, (2 * HEAD_DIM, tq), 0)
    for j in range(DSA_Q_HEADS):
        g = j // DSA_GROUP
        q = qbt_ref[j * HEAD_DIM:(j + 1) * HEAD_DIM, :]
        q2 = jnp.concatenate([q, q], axis=0)
        own = (rows >= g * HEAD_DIM) & (rows < (g + 1) * HEAD_DIM)
        qbm_ref[:, j * tq:(j + 1) * tq] = jnp.where(own, q2, jnp.zeros_like(q2))
    for h in range(IDX_HEADS):
        qic_ref[:, h * tq:(h + 1) * tq] = qit_ref[h * HEAD_DIM:(h + 1) * HEAD_DIM, :]

    kpos = lax.broadcasted_iota(I32, (ck, tq), 0)
    qpos = i * tq + lax.broadcasted_iota(I32, (ck, tq), 1)

    def index_chunk(c, carry):
        off = pl.multiple_of(c * ck, ck)
        sc = _dot(ki_ref[pl.ds(off, ck), :], qic_ref[...])
        acc = None
        for h in range(IDX_HEADS):
            t = jnp.maximum(sc[:, h * tq:(h + 1) * tq], 0.0) * wit_ref[h:h + 1, :]
            acc = t if acc is None else acc + t
        acc = jnp.where(kpos + off <= qpos, acc, -jnp.inf)
        bits = lax.bitcast_convert_type(acc, I32)
        key = jnp.where(bits < 0, INT_MIN - bits, bits)
        hi_ref[pl.ds(off, ck), :] = lax.shift_right_arithmetic(key, 16).astype(I16)
        lo_ref[pl.ds(off, ck), :] = (key ^ 0x8000).astype(I16)
        return carry

    def for_each_chunk(fn):
        def pair(t, carry):
            fn(2 * t, carry)
            return fn(2 * t + 1, carry)
        lax.fori_loop(0, lax.div(nchunk, 2), pair, 0)

        @pl.when(lax.rem(nchunk, 2) == 1)
        def _():
            fn(nchunk - 1, 0)

    for_each_chunk(index_chunk)


    def count(refs, pred, nb=nblk, emit=None):
        def body(r, accs):
            base = pl.multiple_of(r * rb, rb)
            blks = [ref[pl.ds(base, rb), :] for ref in refs]
            if emit is not None:
                emit(blks, base)
            accs = list(accs)
            for u in range(rb // 16):
                hit = pred([blk[u * 16:(u + 1) * 16, :] for blk in blks], base + u * 16)
                accs[u % N_ACC] = accs[u % N_ACC] + jnp.where(hit, jnp.int16(1), jnp.int16(0))
            return tuple(accs)
        accs = lax.fori_loop(0, nb, body, tuple(jnp.zeros((16, tq), I16) for _ in range(N_ACC)))
        tot = accs[0].astype(I32)
        for a in accs[1:]:
            tot = tot + a.astype(I32)
        return jnp.sum(tot, axis=0, keepdims=True)

    def rows16(v):
        return jnp.broadcast_to(v, (16, tq)).astype(I16)

    def bisect(ref, want):
        def step(it, u):
            uc = u | lax.shift_left(jnp.int32(1), 15 - it)
            cand = rows16(uc - 32768)
            cnt = count([ref], lambda b, p0: b[0] >= cand)
            return jnp.where(cnt >= want, uc, u)
        return lax.fori_loop(0, 16, step, jnp.zeros((1, tq), I32)) - 32768

    a32 = jnp.maximum(bisect(hi_ref, topk), HI_NEG_INF)
    a16 = rows16(a32)
    a_blk = jnp.broadcast_to(a32, (rb, tq)).astype(I16)

    fmax_ref[...] = jnp.full(fmax_ref.shape, I16_MIN, I16)
    fmin_ref[...] = jnp.full(fmin_ref.shape, I16_MIN, I16)

    def fold_block(r, carry):
        accs, most = carry
        base = pl.multiple_of(r * rb, rb)
        hi, lo = hi_ref[pl.ds(base, rb), :], lo_ref[pl.ds(base, rb), :]
        top = jnp.full((16, tq), I16_MIN, I16)
        low = jnp.full((16, tq), I16_MAX, I16)
        members = jnp.zeros((16, tq), I16)
        accs = list(accs)
        for u in range(rb // 16):
            h, l = hi[u * 16:(u + 1) * 16, :], lo[u * 16:(u + 1) * 16, :]
            member = h == a16
            up, down = jnp.where(member, l, jnp.int16(I16_MIN)), jnp.where(member, l, jnp.int16(I16_MAX))
            top = jnp.where(up > top, up, top)
            low = jnp.where(down < low, down, low)
            members = members + jnp.where(member, jnp.int16(1), jnp.int16(0))
            accs[u % N_ACC] = accs[u % N_ACC] + jnp.where(h > a16, jnp.int16(1), jnp.int16(0))
        row = pl.multiple_of(r * 16, 16)
        fmax_ref[pl.ds(row, 16), :] = top
        fmin_ref[pl.ds(row, 16), :] = jnp.where(members >= 2, low, jnp.int16(I16_MIN))
        return tuple(accs), jnp.where(members > most, members, most)

    accs, most = lax.fori_loop(0, nblk, fold_block, (tuple(jnp.zeros((16, tq), I16) for _ in range(N_ACC)),
                                                     jnp.zeros((16, tq), I16)))
    n_hi = accs[0].astype(I32)
    for a in accs[1:]:
        n_hi = n_hi + a.astype(I32)
    n_hi = jnp.sum(n_hi, axis=0, keepdims=True)
    want_lo = topk - n_hi

    def folded_step(it, u):
        uc = u | lax.shift_left(jnp.int32(1), 15 - it)
        cand = rows16(uc - 32768)
        accs = [jnp.zeros((16, tq), I16) for _ in range(N_ACC)]
        for r in range(fmax_ref.shape[0] // 16):
            for k, ref in enumerate((fmax_ref, fmin_ref)):
                hit = ref[r * 16:(r + 1) * 16, :] >= cand
                accs[(2 * r + k) % N_ACC] = accs[(2 * r + k) % N_ACC] + jnp.where(hit, jnp.int16(1), jnp.int16(0))
        tot = accs[0].astype(I32)
        for a in accs[1:]:
            tot = tot + a.astype(I32)
        return jnp.where(jnp.sum(tot, axis=0, keepdims=True) >= want_lo, uc, u)

    blo_ref[...] = jnp.broadcast_to(lax.fori_loop(0, 16, folded_step, jnp.zeros((1, tq), I32)) - 32768, blo_ref.shape)

    @pl.when(jnp.max(most.astype(I32)) > FOLD_MAX_MEMBERS)
    def _():
        def bucket(r, carry):
            base = pl.multiple_of(r * rb, rb)
            sel_ref[pl.ds(base, rb), :] = jnp.where(hi_ref[pl.ds(base, rb), :] == a_blk, lo_ref[pl.ds(base, rb), :],
                                                    jnp.int16(I16_MIN))
            return carry
        lax.fori_loop(0, nblk, bucket, 0)
        blo_ref[...] = jnp.broadcast_to(bisect(sel_ref, want_lo), blo_ref.shape)

    b32 = blo_ref[0:1, :]
    b32 = jnp.where(a32 == HI_NEG_INF, jnp.maximum(b32, I16_MIN + 1), b32)
    b16 = rows16(b32)
    b_blk = jnp.broadcast_to(b32, (rb, tq)).astype(I16)

    def emit_selection(blks, base):
        sel = (blks[0] > a_blk) | ((blks[0] == a_blk) & (blks[1] >= b_blk))
        sel_ref[pl.ds(base, rb), :] = jnp.where(sel, jnp.int16(BF16_ONE_BITS), jnp.int16(0))

    n_ge = count([hi_ref, lo_ref], lambda b, p0: (b[0] > a16) | ((b[0] == a16) & (b[1] >= b16)),
                 nb=nblk_pad, emit=emit_selection)

    need = n_ge > topk

    @pl.when(jnp.max(jnp.where(need, 1, 0)) > 0)
    def _():
        nb_max = tcnt_ref.shape[0]
        none = jnp.int32(2 ** 30)

        def tied(hi, lo, a, b):
            return (hi == a) & (lo == b)

        tcnt_ref[...] = jnp.zeros(tcnt_ref.shape, I32)

        def tie_block(r, carry):
            base = pl.multiple_of(r * rb, rb)
            hi, lo = hi_ref[pl.ds(base, rb), :], lo_ref[pl.ds(base, rb), :]
            acc = jnp.zeros((16, tq), I16)
            for u in range(rb // 16):
                acc = acc + jnp.where(tied(hi[u * 16:(u + 1) * 16, :], lo[u * 16:(u + 1) * 16, :], a16, b16),
                                      jnp.int16(1), jnp.int16(0))
            tcnt_ref[r] = jnp.broadcast_to(jnp.sum(acc.astype(I32), axis=0, keepdims=True), (8, tq))
            return carry

        lax.fori_loop(0, nblk, tie_block, 0)

        per_block = [tcnt_ref[r][0:1, :] for r in range(nb_max)]
        n_tied = per_block[0]
        for c in per_block[1:]:
            n_tied = n_tied + c
        want = jnp.where(need, topk - (n_ge - n_tied), none)
        last = jnp.full((1, tq), nb_max, I32)
        before = jnp.zeros((1, tq), I32)
        prefix = jnp.zeros((1, tq), I32)
        for r, c in enumerate(per_block):
            reached = (last == nb_max) & (prefix + c >= want)
            last = jnp.where(reached, r, last)
            before = jnp.where(reached, prefix, before)
            prefix = prefix + c
        want_here = want - before

        tsel_ref[...] = jnp.zeros(tsel_ref.shape, I16)

        def pick_block(r, carry):
            base = pl.multiple_of(r * rb, rb)
            here = jnp.broadcast_to(jnp.where(last == r, 1, 0), (rb, tq)).astype(I16) != 0
            hit = tied(hi_ref[pl.ds(base, rb), :], lo_ref[pl.ds(base, rb), :], a_blk, b_blk) & here
            tsel_ref[...] = tsel_ref[...] + jnp.where(hit, jnp.int16(1), jnp.int16(0))
            return carry

        lax.fori_loop(0, nblk, pick_block, 0)

        in_block = lax.broadcasted_iota(I32, (rb, tq), 0).astype(I16)
        block_bits = (rb - 1).bit_length()

        def row_step(it, e):
            ec = e | lax.shift_left(jnp.int32(1), block_bits - 1 - it)
            hit = (tsel_ref[...] != 0) & (in_block < jnp.broadcast_to(ec, (rb, tq)).astype(I16))
            ones = jnp.where(hit, jnp.int16(1), jnp.int16(0))
            acc = ones[0:16, :]
            for u in range(1, rb // 16):
                acc = acc + ones[u * 16:(u + 1) * 16, :]
            below = jnp.sum(acc.astype(I32), axis=0, keepdims=True)
            return jnp.where(below < want_here, ec, e)

        e_last = lax.fori_loop(0, block_bits, row_step, jnp.zeros((1, tq), I32))
        x = jnp.where(last < nb_max, jnp.minimum(last * rb + e_last, I16_MAX), I16_MAX)

        def demote(r, carry):
            base = pl.multiple_of(r * rb, rb)
            pos = (base + lax.broadcasted_iota(I32, (rb, tq), 0)).astype(I16)
            drop = (tied(hi_ref[pl.ds(base, rb), :], lo_ref[pl.ds(base, rb), :], a_blk, b_blk)
                    & (pos > jnp.broadcast_to(x, (rb, tq)).astype(I16)))
            sel_ref[pl.ds(base, rb), :] = jnp.where(drop, jnp.int16(0), sel_ref[pl.ds(base, rb), :])
            return carry

        lax.fori_loop(0, nblk, demote, 0)

    shift = shift_ref[0]

    @pl.when(shift < SAFE_SHIFT_LOG2)
    def _():
        acc_ref[...] = jnp.zeros(acc_ref.shape, F32)

        def attend_chunk(c, carry):
            off = pl.multiple_of(c * ck, ck)
            kch = kb_ref[pl.ds(off, ck), :]
            msk = lax.bitcast_convert_type(sel_ref[pl.ds(off, ck), :], BF16)
            for g in range(DSA_KV_HEADS):
                sc = _dot(kch, qbm_ref[:, g * DSA_GROUP * tq:(g + 1) * DSA_GROUP * tq])
                p = jnp.exp2(sc - shift).astype(BF16) * _tile_lanes(msk, DSA_GROUP)
                acc_ref[g] += _dot(vta_ref[g * V_AUG:(g + 1) * V_AUG, pl.ds(off, ck)], p)
            return carry

        for_each_chunk(attend_chunk)

    @pl.when(shift >= SAFE_SHIFT_LOG2)
    def _():
        m_ref[...] = jnp.full(m_ref.shape, NEG_BIG, F32)
        acc_ref[...] = jnp.zeros(acc_ref.shape, F32)

        def attend_chunk(c, carry):
            off = pl.multiple_of(c * ck, ck)
            kch = kb_ref[pl.ds(off, ck), :]
            sel = sel_ref[pl.ds(off, ck), :].astype(I32) != 0
            for j in range(DSA_Q_HEADS):
                g = j // DSA_GROUP
                cols = slice((j % DSA_GROUP) * tq, (j % DSA_GROUP + 1) * tq)
                sc = jnp.where(sel, _dot(kch, qbm_ref[:, j * tq:(j + 1) * tq]), NEG_BIG)
                m_prev = m_ref[j]
                m_new = jnp.maximum(m_prev, jnp.max(sc, axis=0, keepdims=True))
                p = jnp.where(sel, jnp.exp2(sc - m_new), 0.0).astype(BF16)
                acc_ref[g, :, cols] = jnp.exp2(m_prev - m_new) * acc_ref[g, :, cols] + _dot(
                    vta_ref[g * V_AUG:(g + 1) * V_AUG, pl.ds(off, ck)], p)
                m_ref[j] = m_new
            return carry

        lax.fori_loop(0, nchunk, attend_chunk, 0)

    for pair in range(DSA_Q_HEADS // 2):
        halves = []
        for j in (2 * pair, 2 * pair + 1):
            a = acc_ref[j // DSA_GROUP, :, (j % DSA_GROUP) * tq:(j % DSA_GROUP + 1) * tq]
            halves.append(a[:HEAD_DIM] / a[HEAD_DIM:HEAD_DIM + 1])
        o_ref[:, pair * LANES:(pair + 1) * LANES] = jnp.concatenate(halves, axis=0).T.astype(BF16)


def _dsa(shift, qit, wit, ki, qbt, kb, vta):
    b, _, s = qbt.shape
    tq = min(256, s)
    ck = min(512, s)
    rb = 256
    topk = min(DSA_TOPK_MAX, s // 4)
    qt = lambda w: pl.BlockSpec((None, w, tq), lambda i, j: (i, 0, j))
    seq = lambda w: pl.BlockSpec((None, s, w), lambda i, j: (i, 0, 0))
    return pl.pallas_call(
        functools.partial(_dsa_kernel, tq=tq, ck=ck, rb=rb, topk=topk),
        grid=(b, s // tq),
        in_specs=[pl.BlockSpec(memory_space=pltpu.SMEM), qt(D_QI), qt(IDX_HEADS), seq(HEAD_DIM), qt(D_QB), seq(D_KVB),
                  pl.BlockSpec((None, DSA_KV_HEADS * V_AUG, s), lambda i, j: (i, 0, 0))],
        out_specs=pl.BlockSpec((None, tq, D_QB), lambda i, j: (i, j, 0)),
        out_shape=jax.ShapeDtypeStruct((b, s, D_QB), BF16),
        scratch_shapes=[
            pltpu.VMEM((s, tq), I16),
            pltpu.VMEM((s, tq), I16),
            pltpu.VMEM((s, tq), I16),
            pltpu.VMEM((s // rb * 16, tq), I16),
            pltpu.VMEM((s // rb * 16, tq), I16),
            pltpu.VMEM((8, tq), I32),
            pltpu.VMEM((s // rb, 8, tq), I32),
            pltpu.VMEM((rb, tq), I16),
            pltpu.VMEM((HEAD_DIM, IDX_HEADS * tq), BF16),
            pltpu.VMEM((2 * HEAD_DIM, DSA_Q_HEADS * tq), BF16),
            pltpu.VMEM((DSA_Q_HEADS, 1, tq), F32),
            pltpu.VMEM((DSA_KV_HEADS, V_AUG, DSA_GROUP * tq), F32),
        ],
        compiler_params=pltpu.CompilerParams(dimension_semantics=("arbitrary", "arbitrary"),
                                             vmem_limit_bytes=VMEM_LIMIT),
        name="dsa",
    )(shift, qit, wit, ki, qbt, kb, vta)


def _merge_kernel(x_ref, g_ref, wg_ref, o0_ref, l0_ref, o1_ref, l1_ref, o2_ref, l2_ref, ob_ref, qc_ref,
                  km_ref, vm_ref, wa_ref, wb_ref, wc_ref, wo_ref, out_ref, *stage_refs):
    tm, d = x_ref.shape

    def token_major(ref, stage_ref):
        dil = ref.shape[1] // LANES
        if dil == 1:
            return ref[...]
        for r in range(dil):
            stage_ref[pl.ds(r, tm // dil, stride=dil), :] = ref[:, r * LANES:(r + 1) * LANES]
        return stage_ref[...]

    lses = (l0_ref[...], token_major(l1_ref, stage_refs[0]), token_major(l2_ref, stage_refs[1]))
    outs = (o0_ref[...], token_major(o1_ref, stage_refs[2]), token_major(o2_ref, stage_refs[3]))

    def rows_part(rows):
        n = rows.stop - rows.start
        x = x_ref[rows, :]
        h = _rms_rows(x, g_ref[...]).astype(BF16)

        def gate(k):
            z = _dot(h, wg_ref[:, k * d:(k + 1) * d])
            return 1.0 / (1.0 + jnp.exp(-z))

        l0, l1, l2 = (t[rows, :] for t in lses)
        o0, o1, o2 = (t[rows, :] for t in outs)
        mx = jnp.maximum(jnp.maximum(l0, l1), l2)
        e0, e1, e2 = jnp.exp(l0 - mx), jnp.exp(l1 - mx), jnp.exp(l2 - mx)
        oa = (e0 * o0 + e1 * o1 + e2 * o2) / (e0 + e1 + e2)
        merged = gate(0) * _dot(oa.astype(BF16), wa_ref[...])

        merged = merged + gate(1) * _dot(ob_ref[rows, :], wb_ref[...])

        low = _low_half((n, LANES))
        cols = []
        for mcol in range(MEM_HEADS // 2):
            q = qc_ref[rows, mcol * LANES:(mcol + 1) * LANES]
            km = km_ref[:, mcol * LANES:(mcol + 1) * LANES]
            vm = vm_ref[:, mcol * LANES:(mcol + 1) * LANES]
            zero = jnp.zeros_like(q)
            heads = []
            for qh in (jnp.where(low, q, zero), jnp.where(low, zero, q)):
                s = _dot_nt(qh, km)
                e = jnp.exp(s - jnp.max(s, axis=-1, keepdims=True))
                p = e / jnp.sum(e, axis=-1, keepdims=True)
                heads.append(_dot(p.astype(BF16), vm))
            cols.append(jnp.where(low, heads[0], heads[1]))
        oc = jnp.concatenate(cols, axis=1)
        merged = merged + gate(2) * _dot(oc.astype(BF16), wc_ref[...])

        out_ref[rows, :] = x + _dot(merged.astype(BF16), wo_ref[...])

    parts = MERGE_ROW_PARTS if tm % (8 * MERGE_ROW_PARTS) == 0 else 1
    for part in range(parts):
        rows_part(slice(part * (tm // parts), (part + 1) * (tm // parts)))


def _merge(x, g_mix, w_gate, dil, ob, qc, km, vm, w_a, w_b, w_c, w_o, tm):
    b, s, d = x.shape
    mlen = km.shape[1]
    tok = lambda w: pl.BlockSpec((None, tm, w), lambda i, j: (i, j, 0))
    full = lambda *shape: pl.BlockSpec(shape, lambda i, j: (0,) * len(shape))
    memb = pl.BlockSpec((None, mlen, D_QC), lambda i, j: (i, 0, 0))
    dil_args = [t for pair in dil for t in pair]
    dil_specs = [pl.BlockSpec((None, tm // dl, dl * LANES), lambda i, j: (i, j, 0))
                 for _, dl in DIL_GROUPS for _ in range(2)]
    return pl.pallas_call(
        _merge_kernel,
        grid=(b, s // tm),
        in_specs=[tok(d), full(1, d), full(d, 3 * d)] + dil_specs + [tok(D_QB), tok(D_QC), memb, memb,
                  full(LANES, d), full(D_QB, d), full(D_QC, d), full(d, d)],
        out_specs=tok(d),
        out_shape=jax.ShapeDtypeStruct((b, s, d), F32),
        scratch_shapes=[pltpu.VMEM((tm, LANES), F32)] * 4,
        compiler_params=pltpu.CompilerParams(dimension_semantics=("arbitrary", "arbitrary"),
                                             vmem_limit_bytes=VMEM_LIMIT),
        name="merge",
    )(x, g_mix, w_gate, *dil_args, ob, qc, km, vm, w_a, w_b, w_c, w_o)


def _mlp_kernel(x_ref, g_ref, w1_ref, w2_ref, out_ref, *, fchunk):
    x = x_ref[...]
    h = _rms_rows(x, g_ref[...]).astype(BF16)
    acc = x
    for c in range(w1_ref.shape[1] // fchunk):
        u = jnp.maximum(_dot(h, w1_ref[:, c * fchunk:(c + 1) * fchunk]), 0.0)
        acc = acc + _dot((u * u).astype(BF16), w2_ref[c * fchunk:(c + 1) * fchunk, :])
    out_ref[...] = acc


def _mlp(x, g_mlp, w_1, w_2, tm):
    b, s, d = x.shape
    f = w_1.shape[1]
    tok = pl.BlockSpec((None, tm, d), lambda i, j: (i, j, 0))
    full = lambda *shape: pl.BlockSpec(shape, lambda i, j: (0,) * len(shape), pipeline_mode=pl.Buffered(1))
    return pl.pallas_call(
        functools.partial(_mlp_kernel, fchunk=min(1024, f)),
        grid=(b, s // tm),
        in_specs=[tok, full(1, d), full(d, f), full(f, d)],
        out_specs=tok,
        out_shape=jax.ShapeDtypeStruct((b, s, d), F32),
        compiler_params=pltpu.CompilerParams(dimension_semantics=("arbitrary", "arbitrary"),
                                             vmem_limit_bytes=VMEM_LIMIT),
        name="mlp",
    )(x, g_mlp, w_1, w_2)


def _rotary_tables(positions):
    inv = jnp.power(jnp.float32(ROPE_THETA), -jnp.arange(ROT_HALF, dtype=F32) / ROT_HALF)
    ang = positions.astype(F32)[:, None, :] * inv[None, :, None]
    return jnp.cos(ang), jnp.sin(ang)


def _block_diag_mean(width):
    r = jnp.arange(width) // HEAD_DIM
    return jnp.where(r[:, None] == r[None, :], 1.0 / HEAD_DIM, 0.0).astype(BF16)


def _layer(x, mem, tables, g_mix, g_mem, w_in, g_qa, g_ka, g_qb, g_kb, g_qc, g_kc,
           w_mem_kv, w_a, w_b, w_c, w_o, g_mlp, w_1, w_2):
    b, s, d = x.shape
    tm = min(TOKEN_TILE, s)
    cos_t, sin_t = tables
    bd = _block_diag_mean(D_QA)
    scale = HEAD_DIM ** -0.5

    offs, acc = [], 0
    for w in (D_QA, D_QA, D_QA, D_QB, D_KVB, D_KVB, D_QI, HEAD_DIM, IDX_HEADS, D_QC):
        offs.append((acc, acc + w))
        acc += w
    seg = lambda k: w_in[:, offs[k][0]:offs[k][1]]
    w_std = jnp.concatenate([seg(0), seg(1), seg(2), seg(4), seg(7), seg(7), seg(9)], axis=1).astype(BF16)
    w_t = jnp.concatenate([seg(3), seg(6), seg(5), seg(8), jnp.zeros((d, _R_END - _R_WI - IDX_HEADS), w_in.dtype)],
                          axis=1).T.astype(BF16)
    w_gate = w_in[:, acc:].astype(BF16)

    tile6 = lambda g: jnp.tile(g, D_QA // HEAD_DIM)
    hg = jnp.stack([tile6(g_qa) * scale, tile6(g_ka), tile6(g_kb), tile6(g_qc) * scale,
                    jnp.zeros(D_QA), jnp.zeros(D_QA), jnp.zeros(D_QA), jnp.zeros(D_QA)]).astype(F32)
    gqt = jnp.broadcast_to((g_qb * (scale * LOG2E))[:, None], (HEAD_DIM, LANES)).astype(F32)
    shift = (HEAD_DIM * scale * LOG2E * SHIFT_MARGIN) * jnp.max(jnp.abs(g_qb)) * jnp.max(jnp.abs(g_kb))
    shift = jnp.reshape(shift, (1,)).astype(F32)

    km, vm = _memkv(mem, g_mem[None, :], w_mem_kv.astype(BF16), bd[:D_QC, :D_QC],
                    jnp.tile(g_kc, MEM_HEADS)[None, :])
    (q0, q1, q2, k0, k1, k2, v0, v1, v2, kb, ki, qc, qbt, qit, vta, wit) = _inproj(
        x, g_mix[None, :], w_std, w_t, cos_t, sin_t, bd, hg, gqt, tm)
    dil = [_dilated(q, k, v, g) for g, (q, k, v) in enumerate(((q0, k0, v0), (q1, k1, v1), (q2, k2, v2)))]
    ob = _dsa(shift, qit, wit, ki, qbt, kb, vta)
    x = _merge(x, g_mix[None, :], w_gate, dil, ob, qc, km, vm, w_a.astype(BF16), w_b.astype(BF16),
               w_c.astype(BF16), w_o.astype(BF16), tm)
    return _mlp(x, g_mlp[None, :], w_1.astype(BF16), w_2.astype(BF16), tm)


def kernel(x, mem, positions, g_mix, g_mem, w_in, g_qa, g_ka, g_qb, g_kb, g_qc, g_kc, w_mem_kv, w_a, w_b, w_c, w_o, g_mlp, w_1, w_2):
    tables = _rotary_tables(positions)
    for i in range(g_mix.shape[0]):
        x = _layer(x, mem, tables, g_mix[i], g_mem[i], w_in[i], g_qa[i], g_ka[i], g_qb[i], g_kb[i], g_qc[i],
                   g_kc[i], w_mem_kv[i], w_a[i], w_b[i], w_c[i], w_o[i], g_mlp[i], w_1[i], w_2[i])
    return x
```

```python
import functools
import math

import jax
import jax.numpy as jnp
from jax import lax
from jax.experimental import pallas as pl
from jax.experimental.pallas import tpu as pltpu

F32 = jnp.float32
BF16 = jnp.bfloat16
I32 = jnp.int32
I16 = jnp.int16

LANES = 128
HEAD_DIM = 64
ROT_HALF = 8
ROPE_THETA = 500000.0
EPS = 1e-6
DIL_GROUPS = ((128, 1), (512, 4), (2048, 16))
DIL_SPAN = 128
N_DIL_HEADS = 6
DSA_Q_HEADS = 6
DSA_KV_HEADS = 2
DSA_GROUP = DSA_Q_HEADS // DSA_KV_HEADS
DSA_TOPK_MAX = 256
IDX_HEADS = 8
MEM_HEADS = 4
D_QA = N_DIL_HEADS * HEAD_DIM
D_QB = DSA_Q_HEADS * HEAD_DIM
D_KVB = DSA_KV_HEADS * HEAD_DIM
D_QI = IDX_HEADS * HEAD_DIM
D_QC = MEM_HEADS * HEAD_DIM
V_AUG = 80
LOG2E = math.log2(math.e)
NEG_BIG = -1e30
INT_MIN = -2147483648
I16_MIN = -32768
I16_MAX = 32767
FOLD_MAX_MEMBERS = 2
BF16_ONE_BITS = 0x3F80
HI_NEG_INF = -32640
SAFE_SHIFT_LOG2 = 55.0
SHIFT_MARGIN = 1.02
TOKEN_TILE = 512
N_ACC = 4
MERGE_ROW_PARTS = 2
VMEM_LIMIT = 56 * 1024 * 1024

_CONTRACT_LAST = (((1,), (1,)), ((), ()))


def _dot(a, b):
    return jnp.dot(a, b, preferred_element_type=F32)


def _dot_nt(a, b):
    return lax.dot_general(a, b, _CONTRACT_LAST, preferred_element_type=F32)


def _tile_lanes(a, reps):
    return a if reps == 1 else jnp.concatenate([a] * reps, axis=1)


def _low_half(shape):
    return (lax.broadcasted_iota(I32, shape, 1) % LANES) < HEAD_DIM


def _rms_rows(x, g):
    ms = jnp.mean(x * x, axis=-1, keepdims=True)
    return x * lax.rsqrt(ms + EPS) * g


def _norm_heads(p, bd, gain):
    sq = p * p
    hi = sq.astype(BF16)
    lo = (sq - hi.astype(F32)).astype(BF16)
    ms = _dot(hi, bd) + _dot(lo, bd)
    return p * lax.rsqrt(ms + EPS) * gain


def _rotary(y, cos, sin_lo, sin_hi):
    w = y.shape[1]
    reps = w // LANES
    c = _tile_lanes(cos, reps)
    a = _tile_lanes(sin_lo, reps)
    b = _tile_lanes(sin_hi, reps)
    return y * c + pltpu.roll(y, w - ROT_HALF, 1) * a + pltpu.roll(y, ROT_HALF, 1) * b


def _rotary_t(blk, cos_t, sin_t):
    x1, x2 = blk[:ROT_HALF], blk[ROT_HALF:2 * ROT_HALF]
    return jnp.concatenate([x1 * cos_t - x2 * sin_t, x2 * cos_t + x1 * sin_t, blk[2 * ROT_HALF:]], axis=0)


def _memkv_kernel(mem_ref, g_ref, w_ref, bd_ref, gk_ref, k_ref, v_ref):
    h = _rms_rows(mem_ref[...], g_ref[...]).astype(BF16)
    kv = _dot(h, w_ref[...])
    k_ref[...] = _norm_heads(kv[:, :D_QC], bd_ref[...], gk_ref[...]).astype(BF16)
    v_ref[...] = kv[:, D_QC:].astype(BF16)


def _memkv(mem, g_mem, w_mem_kv, bd, gk):
    b, m, d = mem.shape
    full = lambda *shape: pl.BlockSpec(shape, lambda i: (0,) * len(shape))
    return pl.pallas_call(
        _memkv_kernel,
        grid=(b,),
        in_specs=[pl.BlockSpec((None, m, d), lambda i: (i, 0, 0)), full(1, d), full(d, 2 * D_QC),
                  full(D_QC, D_QC), full(1, D_QC)],
        out_specs=[pl.BlockSpec((None, m, D_QC), lambda i: (i, 0, 0))] * 2,
        out_shape=[jax.ShapeDtypeStruct((b, m, D_QC), BF16)] * 2,
        name="memkv",
    )(mem, g_mem, w_mem_kv, bd, gk)


_C_QA, _C_KA, _C_VA, _C_KB, _C_KI, _C_QC, _C_END = 0, 384, 768, 1152, 1280, 1408, 1664
_R_QB, _R_QI, _R_VB, _R_WI, _R_END = 0, 384, 896, 1024, 1040


def _inproj_kernel(x_ref, g_ref, w_ref, wt_ref, cost_ref, sint_ref, bd_ref, hg_ref, gqt_ref,
                   q0_ref, q1_ref, q2_ref, k0_ref, k1_ref, k2_ref, v0_ref, v1_ref, v2_ref,
                   kb_ref, ki_ref, qc_ref, qbt_ref, qit_ref, vta_ref, wit_ref, dil_ref):
    tm = x_ref.shape[0]
    h = _rms_rows(x_ref[...], g_ref[...]).astype(BF16)
    bd = bd_ref[...]

    cos_t, sin_t = cost_ref[...], sint_ref[...]
    rest = HEAD_DIM - 2 * ROT_HALF
    one_r, zero_r, zero_h = jnp.ones((rest, tm), F32), jnp.zeros((rest, tm), F32), jnp.zeros((ROT_HALF, tm), F32)
    cos = jnp.concatenate([cos_t, cos_t, one_r] * (LANES // HEAD_DIM), axis=0).T
    slo = jnp.concatenate([-sin_t, zero_h, zero_r] * (LANES // HEAD_DIM), axis=0).T
    shi = jnp.concatenate([zero_h, sin_t, zero_r] * (LANES // HEAD_DIM), axis=0).T

    def proj(a, b):
        return _dot(h, w_ref[:, a:b])

    def norm_rot(a, b, gain_row):
        w = b - a
        y = _norm_heads(proj(a, b), bd[:w, :w], hg_ref[gain_row:gain_row + 1, :w])
        return _rotary(y, cos, slo, shi)

    def store_dilated(y, out_refs):
        for g, ((_, dil), out_ref) in enumerate(zip(DIL_GROUPS, out_refs)):
            yg = y[:, g * LANES:(g + 1) * LANES]
            if dil == 1:
                out_ref[...] = yg.astype(BF16)
            else:
                dil_ref[...] = yg
                for r in range(dil):
                    out_ref[:, r * LANES:(r + 1) * LANES] = dil_ref[pl.ds(r, tm // dil, stride=dil), :].astype(BF16)

    store_dilated(norm_rot(_C_QA, _C_KA, 0), (q0_ref, q1_ref, q2_ref))
    store_dilated(norm_rot(_C_KA, _C_VA, 1), (k0_ref, k1_ref, k2_ref))
    store_dilated(proj(_C_VA, _C_KB), (v0_ref, v1_ref, v2_ref))
    kb_ref[...] = norm_rot(_C_KB, _C_KI, 2).astype(BF16)
    ki_ref[...] = _rotary(proj(_C_KI, _C_QC), cos, slo, shi)[:, :HEAD_DIM].astype(BF16)
    qc_ref[...] = _norm_heads(proj(_C_QC, _C_END), bd[:D_QC, :D_QC], hg_ref[3:4, :D_QC]).astype(BF16)

    pt = _dot_nt(wt_ref[...], h)
    gq =_tile_lanes(gqt_ref[...], tm // LANES)
    for hd in range(DSA_Q_HEADS):
        blk = pt[_R_QB + hd * HEAD_DIM:_R_QB + (hd + 1) * HEAD_DIM]
        ms = jnp.mean(blk * blk, axis=0, keepdims=True)
        qbt_ref[hd * HEAD_DIM:(hd + 1) * HEAD_DIM, :] = _rotary_t(blk * lax.rsqrt(ms + EPS) * gq, cos_t, sin_t).astype(BF16)
    for hd in range(IDX_HEADS):
        blk = pt[_R_QI + hd * HEAD_DIM:_R_QI + (hd + 1) * HEAD_DIM]
        qit_ref[hd * HEAD_DIM:(hd + 1) * HEAD_DIM, :] = _rotary_t(blk, cos_t, sin_t).astype(BF16)
    vt = pt[_R_VB:_R_WI].astype(BF16)
    pad = jnp.where(lax.broadcasted_iota(I32, (V_AUG - HEAD_DIM, tm), 0) == 0, 1.0, 0.0).astype(BF16)
    vta_ref[...] = jnp.concatenate([vt[:HEAD_DIM], pad, vt[HEAD_DIM:], pad], axis=0)
    wit_ref[...] = pt[_R_WI:_R_WI + IDX_HEADS] * ((IDX_HEADS ** -0.5) * (HEAD_DIM ** -0.5))


def _inproj(x, g_mix, w_std, w_t, cos_t, sin_t, bd, hg, gqt, tm):
    b, s, d = x.shape
    tok = lambda w: pl.BlockSpec((None, tm, w), lambda i, j: (i, j, 0))
    tok_t = lambda r: pl.BlockSpec((None, r, tm), lambda i, j: (i, 0, j))
    full = lambda *shape: pl.BlockSpec(shape, lambda i, j: (0,) * len(shape))
    dils = [dl for _, dl in DIL_GROUPS] * 3
    std = ((D_KVB, BF16), (HEAD_DIM, BF16), (D_QC, BF16))
    tr = ((D_QB, BF16), (D_QI, BF16), (DSA_KV_HEADS * V_AUG, BF16), (IDX_HEADS, F32))
    return pl.pallas_call(
        _inproj_kernel,
        grid=(b, s // tm),
        in_specs=[tok(d), full(1, d), full(d, _C_END), full(_R_END, d),
                  tok_t(ROT_HALF), tok_t(ROT_HALF), full(D_QA, D_QA), full(8, D_QA), full(HEAD_DIM, LANES)],
        out_specs=[pl.BlockSpec((None, tm // dl, dl * LANES), lambda i, j: (i, j, 0)) for dl in dils]
        + [tok(w) for w, _ in std] + [tok_t(r) for r, _ in tr],
        out_shape=[jax.ShapeDtypeStruct((b, s // dl, dl * LANES), BF16) for dl in dils]
        + [jax.ShapeDtypeStruct((b, s, w), t) for w, t in std]
        + [jax.ShapeDtypeStruct((b, r, s), t) for r, t in tr],
        scratch_shapes=[pltpu.VMEM((tm, LANES), F32)],
        compiler_params=pltpu.CompilerParams(dimension_semantics=("arbitrary", "arbitrary"),
                                             vmem_limit_bytes=VMEM_LIMIT),
        name="inproj",
    )(x, g_mix, w_std, w_t, cos_t, sin_t, bd, hg, gqt)


def _dilated_kernel(q_ref, k_ref, v_ref, kp_ref, vp_ref, o_ref, l_ref, *, nsub):
    n = pl.program_id(2)
    sp = DIL_SPAN
    low = _low_half((sp, LANES))
    rq = lax.broadcasted_iota(I32, (2 * sp, 2 * sp), 0) % sp
    kj = lax.broadcasted_iota(I32, (2 * sp, 2 * sp), 1)
    dist = sp + rq - kj
    band = (dist >= 0) & (dist <= sp)
    kmin = jnp.where(n > 0, 0, sp)
    scores = []
    for i in range(nsub):
        q = q_ref[i * sp:(i + 1) * sp, :]
        kprev = kp_ref[...] if i == 0 else k_ref[(i - 1) * sp:i * sp, :]
        mask = band & (kj >= kmin) if i == 0 else band
        kk = jnp.concatenate([kprev, k_ref[i * sp:(i + 1) * sp, :]], axis=0)
        zero = jnp.zeros_like(q)
        qs = jnp.concatenate([jnp.where(low, q, zero), jnp.where(low, zero, q)], axis=0)
        scores.append(jnp.where(mask, _dot_nt(qs, kk), -jnp.inf))
    probs = []
    for i, s in enumerate(scores):
        m = jnp.max(s, axis=-1, keepdims=True)
        e = jnp.exp(s - m)
        den = jnp.sum(e, axis=-1, keepdims=True)
        probs.append((e / den).astype(BF16))
        lse = jnp.broadcast_to(m + jnp.log(den), (2 * sp, LANES))
        l_ref[i * sp:(i + 1) * sp, :] = jnp.where(low, lse[:sp], lse[sp:])
    for i, p in enumerate(probs):
        vprev = vp_ref[...] if i == 0 else v_ref[(i - 1) * sp:i * sp, :]
        o2 = _dot(p, jnp.concatenate([vprev, v_ref[i * sp:(i + 1) * sp, :]], axis=0))
        o_ref[i * sp:(i + 1) * sp, :] = jnp.where(low, o2[:sp], o2[sp:])


def _dilated(q, k, v, group):
    b, m, width = q.shape
    dilation = width // LANES
    tb = min(512, m)
    nsub = tb // DIL_SPAN
    cur = pl.BlockSpec((None, tb, LANES), lambda i, r, n: (i, n, r))
    prev = pl.BlockSpec((None, DIL_SPAN, LANES), lambda i, r, n: (i, jnp.maximum(n * nsub - 1, 0), r))
    return pl.pallas_call(
        functools.partial(_dilated_kernel, nsub=nsub),
        grid=(b, dilation, m // tb),
        in_specs=[cur, cur, cur, prev, prev],
        out_specs=[cur, cur],
        out_shape=[jax.ShapeDtypeStruct((b, m, width), F32)] * 2,
        compiler_params=pltpu.CompilerParams(dimension_semantics=("arbitrary",) * 3),
        name=f"dilated_g{group}",
    )(q, k, v, k, v)


def _dsa_kernel(shift_ref, qit_ref, wit_ref, ki_ref, qbt_ref, kb_ref, vta_ref, o_ref,
                hi_ref, lo_ref, sel_ref, fmax_ref, fmin_ref, blo_ref, tcnt_ref, tsel_ref, qic_ref, qbm_ref, m_ref, acc_ref,
                *, tq, ck, rb, topk):
    i = pl.program_id(1)
    nrows = (i + 1) * tq
    nfull = lax.div(nrows, ck)
    nblk = lax.div(nrows, rb)

    rows = lax.broadcasted_iota(I32, (2 * HEAD_DIM, tq), 0)
    for j in range(DSA_Q_HEADS):
        g = j // DSA_GROUP
        q = qbt_ref[j * HEAD_DIM:(j + 1) * HEAD_DIM, :]
        q2 = jnp.concatenate([q, q], axis=0)
        own = (rows >= g * HEAD_DIM) & (rows < (g + 1) * HEAD_DIM)
        qbm_ref[:, j * tq:(j + 1) * tq] = jnp.where(own, q2, jnp.zeros_like(q2))
    for h in range(IDX_HEADS):
        qic_ref[:, h * tq:(h + 1) * tq] = qit_ref[h * HEAD_DIM:(h + 1) * HEAD_DIM, :]

    def index_chunk(off, size):
        sc = _dot(ki_ref[pl.ds(off, size), :], qic_ref[...])
        acc = None
        for h in range(IDX_HEADS):
            t = jnp.maximum(sc[:, h * tq:(h + 1) * tq], 0.0) * wit_ref[h:h + 1, :]
            acc = t if acc is None else acc + t
        kpos = off + lax.broadcasted_iota(I32, (size, tq), 0)
        qpos = i * tq + lax.broadcasted_iota(I32, (size, tq), 1)
        acc = jnp.where(kpos <= qpos, acc, -jnp.inf)
        bits = lax.bitcast_convert_type(acc, I32)
        key = jnp.where(bits < 0, INT_MIN - bits, bits)
        hi_ref[pl.ds(off, size), :] = lax.shift_right_arithmetic(key, 16).astype(I16)
        lo_ref[pl.ds(off, size), :] = (key ^ 0x8000).astype(I16)

    def for_each_chunk(fn):
        def pair(t, carry):
            fn(pl.multiple_of(2 * t * ck, ck), ck)
            fn(pl.multiple_of((2 * t + 1) * ck, ck), ck)
            return carry
        lax.fori_loop(0, lax.div(nfull, 2), pair, 0)

        @pl.when(lax.rem(nfull, 2) == 1)
        def _():
            fn(pl.multiple_of((nfull - 1) * ck, ck), ck)

        @pl.when(nfull * ck < nrows)
        def _():
            fn(pl.multiple_of(nfull * ck, ck), ck // 2)

    for_each_chunk(index_chunk)


    def count(refs, pred, nb=nblk, emit=None):
        def body(r, accs):
            base = pl.multiple_of(r * rb, rb)
            blks = [ref[pl.ds(base, rb), :] for ref in refs]
            if emit is not None:
                emit(blks, base)
            accs = list(accs)
            for u in range(rb // 16):
                hit = pred([blk[u * 16:(u + 1) * 16, :] for blk in blks], base + u * 16)
                accs[u % N_ACC] = accs[u % N_ACC] + jnp.where(hit, jnp.int16(1), jnp.int16(0))
            return tuple(accs)
        accs = lax.fori_loop(0, nb, body, tuple(jnp.zeros((16, tq), I16) for _ in range(N_ACC)))
        tot = accs[0].astype(I32)
        for a in accs[1:]:
            tot = tot + a.astype(I32)
        return jnp.sum(tot, axis=0, keepdims=True)

    def rows16(v):
        return jnp.broadcast_to(v, (16, tq)).astype(I16)

    def bisect(ref, want):
        def step(it, u):
            uc = u | lax.shift_left(jnp.int32(1), 15 - it)
            cand = rows16(uc - 32768)
            cnt = count([ref], lambda b, p0: b[0] >= cand)
            return jnp.where(cnt >= want, uc, u)
        return lax.fori_loop(0, 16, step, jnp.zeros((1, tq), I32)) - 32768

    a32 = jnp.maximum(bisect(hi_ref, topk), HI_NEG_INF)
    a16 = rows16(a32)
    a_blk = jnp.broadcast_to(a32, (rb, tq)).astype(I16)

    fmax_ref[...] = jnp.full(fmax_ref.shape, I16_MIN, I16)
    fmin_ref[...] = jnp.full(fmin_ref.shape, I16_MIN, I16)

    def fold_block(r, carry):
        accs, most = carry
        base = pl.multiple_of(r * rb, rb)
        hi, lo = hi_ref[pl.ds(base, rb), :], lo_ref[pl.ds(base, rb), :]
        top = jnp.full((16, tq), I16_MIN, I16)
        low = jnp.full((16, tq), I16_MAX, I16)
        members = jnp.zeros((16, tq), I16)
        accs = list(accs)
        for u in range(rb // 16):
            h, l = hi[u * 16:(u + 1) * 16, :], lo[u * 16:(u + 1) * 16, :]
            member = h == a16
            up, down = jnp.where(member, l, jnp.int16(I16_MIN)), jnp.where(member, l, jnp.int16(I16_MAX))
            top = jnp.where(up > top, up, top)
            low = jnp.where(down < low, down, low)
            members = members + jnp.where(member, jnp.int16(1), jnp.int16(0))
            accs[u % N_ACC] = accs[u % N_ACC] + jnp.where(h > a16, jnp.int16(1), jnp.int16(0))
        row = pl.multiple_of(r * 16, 16)
        fmax_ref[pl.ds(row, 16), :] = top
        fmin_ref[pl.ds(row, 16), :] = jnp.where(members >= 2, low, jnp.int16(I16_MIN))
        return tuple(accs), jnp.where(members > most, members, most)

    accs, most = lax.fori_loop(0, nblk, fold_block, (tuple(jnp.zeros((16, tq), I16) for _ in range(N_ACC)),
                                                     jnp.zeros((16, tq), I16)))
    n_hi = accs[0].astype(I32)
    for a in accs[1:]:
        n_hi = n_hi + a.astype(I32)
    n_hi = jnp.sum(n_hi, axis=0, keepdims=True)
    want_lo = topk - n_hi

    def folded_step(it, u):
        uc = u | lax.shift_left(jnp.int32(1), 15 - it)
        cand = rows16(uc - 32768)
        accs = [jnp.zeros((16, tq), I16) for _ in range(N_ACC)]
        for r in range(fmax_ref.shape[0] // 16):
            for k, ref in enumerate((fmax_ref, fmin_ref)):
                hit = ref[r * 16:(r + 1) * 16, :] >= cand
                accs[(2 * r + k) % N_ACC] = accs[(2 * r + k) % N_ACC] + jnp.where(hit, jnp.int16(1), jnp.int16(0))
        tot = accs[0].astype(I32)
        for a in accs[1:]:
            tot = tot + a.astype(I32)
        return jnp.where(jnp.sum(tot, axis=0, keepdims=True) >= want_lo, uc, u)

    blo_ref[...] = jnp.broadcast_to(lax.fori_loop(0, 16, folded_step, jnp.zeros((1, tq), I32)) - 32768, blo_ref.shape)

    @pl.when(jnp.max(most.astype(I32)) > FOLD_MAX_MEMBERS)
    def _():
        def bucket(r, carry):
            base = pl.multiple_of(r * rb, rb)
            sel_ref[pl.ds(base, rb), :] = jnp.where(hi_ref[pl.ds(base, rb), :] == a_blk, lo_ref[pl.ds(base, rb), :],
                                                    jnp.int16(I16_MIN))
            return carry
        lax.fori_loop(0, nblk, bucket, 0)
        blo_ref[...] = jnp.broadcast_to(bisect(sel_ref, want_lo), blo_ref.shape)

    b32 = blo_ref[0:1, :]
    b32 = jnp.where(a32 == HI_NEG_INF, jnp.maximum(b32, I16_MIN + 1), b32)
    b16 = rows16(b32)
    b_blk = jnp.broadcast_to(b32, (rb, tq)).astype(I16)

    def emit_selection(blks, base):
        sel = (blks[0] > a_blk) | ((blks[0] == a_blk) & (blks[1] >= b_blk))
        sel_ref[pl.ds(base, rb), :] = jnp.where(sel, jnp.int16(BF16_ONE_BITS), jnp.int16(0))

    n_ge = count([hi_ref, lo_ref], lambda b, p0: (b[0] > a16) | ((b[0] == a16) & (b[1] >= b16)),
                 emit=emit_selection)

    need = n_ge > topk

    @pl.when(jnp.max(jnp.where(need, 1, 0)) > 0)
    def _():
        nb_max = tcnt_ref.shape[0]
        none = jnp.int32(2 ** 30)

        def tied(hi, lo, a, b):
            return (hi == a) & (lo == b)

        tcnt_ref[...] = jnp.zeros(tcnt_ref.shape, I32)

        def tie_block(r, carry):
            base = pl.multiple_of(r * rb, rb)
            hi, lo = hi_ref[pl.ds(base, rb), :], lo_ref[pl.ds(base, rb), :]
            acc = jnp.zeros((16, tq), I16)
            for u in range(rb // 16):
                acc = acc + jnp.where(tied(hi[u * 16:(u + 1) * 16, :], lo[u * 16:(u + 1) * 16, :], a16, b16),
                                      jnp.int16(1), jnp.int16(0))
            tcnt_ref[r] = jnp.broadcast_to(jnp.sum(acc.astype(I32), axis=0, keepdims=True), (8, tq))
            return carry

        lax.fori_loop(0, nblk, tie_block, 0)

        per_block = [tcnt_ref[r][0:1, :] for r in range(nb_max)]
        n_tied = per_block[0]
        for c in per_block[1:]:
            n_tied = n_tied + c
        want = jnp.where(need, topk - (n_ge - n_tied), none)
        last = jnp.full((1, tq), nb_max, I32)
        before = jnp.zeros((1, tq), I32)
        prefix = jnp.zeros((1, tq), I32)
        for r, c in enumerate(per_block):
            reached = (last == nb_max) & (prefix + c >= want)
            last = jnp.where(reached, r, last)
            before = jnp.where(reached, prefix, before)
            prefix = prefix + c
        want_here = want - before

        tsel_ref[...] = jnp.zeros(tsel_ref.shape, I16)

        def pick_block(r, carry):
            base = pl.multiple_of(r * rb, rb)
            here = jnp.broadcast_to(jnp.where(last == r, 1, 0), (rb, tq)).astype(I16) != 0
            hit = tied(hi_ref[pl.ds(base, rb), :], lo_ref[pl.ds(base, rb), :], a_blk, b_blk) & here
            tsel_ref[...] = tsel_ref[...] + jnp.where(hit, jnp.int16(1), jnp.int16(0))
            return carry

        lax.fori_loop(0, nblk, pick_block, 0)

        in_block = lax.broadcasted_iota(I32, (rb, tq), 0).astype(I16)
        block_bits = (rb - 1).bit_length()

        def row_step(it, e):
            ec = e | lax.shift_left(jnp.int32(1), block_bits - 1 - it)
            hit = (tsel_ref[...] != 0) & (in_block < jnp.broadcast_to(ec, (rb, tq)).astype(I16))
            ones = jnp.where(hit, jnp.int16(1), jnp.int16(0))
            acc = ones[0:16, :]
            for u in range(1, rb // 16):
                acc = acc + ones[u * 16:(u + 1) * 16, :]
            below = jnp.sum(acc.astype(I32), axis=0, keepdims=True)
            return jnp.where(below < want_here, ec, e)

        e_last = lax.fori_loop(0, block_bits, row_step, jnp.zeros((1, tq), I32))
        x = jnp.where(last < nb_max, jnp.minimum(last * rb + e_last, I16_MAX), I16_MAX)

        def demote(r, carry):
            base = pl.multiple_of(r * rb, rb)
            pos = (base + lax.broadcasted_iota(I32, (rb, tq), 0)).astype(I16)
            drop = (tied(hi_ref[pl.ds(base, rb), :], lo_ref[pl.ds(base, rb), :], a_blk, b_blk)
                    & (pos > jnp.broadcast_to(x, (rb, tq)).astype(I16)))
            sel_ref[pl.ds(base, rb), :] = jnp.where(drop, jnp.int16(0), sel_ref[pl.ds(base, rb), :])
            return carry

        lax.fori_loop(0, nblk, demote, 0)

    shift = shift_ref[0]

    @pl.when(shift < SAFE_SHIFT_LOG2)
    def _():
        acc_ref[...] = jnp.zeros(acc_ref.shape, F32)

        def attend_chunk(off, size):
            kch = kb_ref[pl.ds(off, size), :]
            msk = lax.bitcast_convert_type(sel_ref[pl.ds(off, size), :], BF16)
            for g in range(DSA_KV_HEADS):
                sc = _dot(kch, qbm_ref[:, g * DSA_GROUP * tq:(g + 1) * DSA_GROUP * tq])
                p = jnp.exp2(sc - shift).astype(BF16) * _tile_lanes(msk, DSA_GROUP)
                acc_ref[g] += _dot(vta_ref[g * V_AUG:(g + 1) * V_AUG, pl.ds(off, size)], p)

        for_each_chunk(attend_chunk)

    @pl.when(shift >= SAFE_SHIFT_LOG2)
    def _():
        m_ref[...] = jnp.full(m_ref.shape, NEG_BIG, F32)
        acc_ref[...] = jnp.zeros(acc_ref.shape, F32)

        def attend_block(r, carry):
            off = pl.multiple_of(r * rb, rb)
            kch = kb_ref[pl.ds(off, rb), :]
            sel = sel_ref[pl.ds(off, rb), :].astype(I32) != 0
            for j in range(DSA_Q_HEADS):
                g = j // DSA_GROUP
                cols = slice((j % DSA_GROUP) * tq, (j % DSA_GROUP + 1) * tq)
                sc = jnp.where(sel, _dot(kch, qbm_ref[:, j * tq:(j + 1) * tq]), NEG_BIG)
                m_prev = m_ref[j]
                m_new = jnp.maximum(m_prev, jnp.max(sc, axis=0, keepdims=True))
                p = jnp.where(sel, jnp.exp2(sc - m_new), 0.0).astype(BF16)
                acc_ref[g, :, cols] = jnp.exp2(m_prev - m_new) * acc_ref[g, :, cols] + _dot(
                    vta_ref[g * V_AUG:(g + 1) * V_AUG, pl.ds(off, rb)], p)
                m_ref[j] = m_new
            return carry

        lax.fori_loop(0, nblk, attend_block, 0)

    for pair in range(DSA_Q_HEADS // 2):
        halves = []
        for j in (2 * pair, 2 * pair + 1):
            a = acc_ref[j // DSA_GROUP, :, (j % DSA_GROUP) * tq:(j % DSA_GROUP + 1) * tq]
            halves.append(a[:HEAD_DIM] / a[HEAD_DIM:HEAD_DIM + 1])
        o_ref[:, pair * LANES:(pair + 1) * LANES] = jnp.concatenate(halves, axis=0).T.astype(BF16)


def _dsa(shift, qit, wit, ki, qbt, kb, vta):
    b, _, s = qbt.shape
    tq = min(256, s)
    ck = min(512, s)
    rb = 256
    topk = min(DSA_TOPK_MAX, s // 4)
    qt = lambda w: pl.BlockSpec((None, w, tq), lambda i, j: (i, 0, j))
    seq = lambda w: pl.BlockSpec((None, s, w), lambda i, j: (i, 0, 0))
    return pl.pallas_call(
        functools.partial(_dsa_kernel, tq=tq, ck=ck, rb=rb, topk=topk),
        grid=(b, s // tq),
        in_specs=[pl.BlockSpec(memory_space=pltpu.SMEM), qt(D_QI), qt(IDX_HEADS), seq(HEAD_DIM), qt(D_QB), seq(D_KVB),
                  pl.BlockSpec((None, DSA_KV_HEADS * V_AUG, s), lambda i, j: (i, 0, 0))],
        out_specs=pl.BlockSpec((None, tq, D_QB), lambda i, j: (i, j, 0)),
        out_shape=jax.ShapeDtypeStruct((b, s, D_QB), BF16),
        scratch_shapes=[
            pltpu.VMEM((s, tq), I16),
            pltpu.VMEM((s, tq), I16),
            pltpu.VMEM((s, tq), I16),
            pltpu.VMEM((s // rb * 16, tq), I16),
            pltpu.VMEM((s // rb * 16, tq), I16),
            pltpu.VMEM((8, tq), I32),
            pltpu.VMEM((s // rb, 8, tq), I32),
            pltpu.VMEM((rb, tq), I16),
            pltpu.VMEM((HEAD_DIM, IDX_HEADS * tq), BF16),
            pltpu.VMEM((2 * HEAD_DIM, DSA_Q_HEADS * tq), BF16),
            pltpu.VMEM((DSA_Q_HEADS, 1, tq), F32),
            pltpu.VMEM((DSA_KV_HEADS, V_AUG, DSA_GROUP * tq), F32),
        ],
        compiler_params=pltpu.CompilerParams(dimension_semantics=("arbitrary", "arbitrary"),
                                             vmem_limit_bytes=VMEM_LIMIT),
        name="dsa",
    )(shift, qit, wit, ki, qbt, kb, vta)


def _merge_kernel(x_ref, g_ref, wg_ref, o0_ref, l0_ref, o1_ref, l1_ref, o2_ref, l2_ref, ob_ref, qc_ref,
                  km_ref, vm_ref, wa_ref, wb_ref, wc_ref, wo_ref, out_ref, *stage_refs):
    tm, d = x_ref.shape

    def token_major(ref, stage_ref):
        dil = ref.shape[1] // LANES
        if dil == 1:
            return ref[...]
        for r in range(dil):
            stage_ref[pl.ds(r, tm // dil, stride=dil), :] = ref[:, r * LANES:(r + 1) * LANES]
        return stage_ref[...]

    lses = (l0_ref[...], token_major(l1_ref, stage_refs[0]), token_major(l2_ref, stage_refs[1]))
    outs = (o0_ref[...], token_major(o1_ref, stage_refs[2]), token_major(o2_ref, stage_refs[3]))

    def rows_part(rows):
        n = rows.stop - rows.start
        x = x_ref[rows, :]
        h = _rms_rows(x, g_ref[...]).astype(BF16)

        def gate(k):
            z = _dot(h, wg_ref[:, k * d:(k + 1) * d])
            return 1.0 / (1.0 + jnp.exp(-z))

        l0, l1, l2 = (t[rows, :] for t in lses)
        o0, o1, o2 = (t[rows, :] for t in outs)
        mx = jnp.maximum(jnp.maximum(l0, l1), l2)
        e0, e1, e2 = jnp.exp(l0 - mx), jnp.exp(l1 - mx), jnp.exp(l2 - mx)
        oa = (e0 * o0 + e1 * o1 + e2 * o2) / (e0 + e1 + e2)
        merged = gate(0) * _dot(oa.astype(BF16), wa_ref[...])

        merged = merged + gate(1) * _dot(ob_ref[rows, :], wb_ref[...])

        low = _low_half((n, LANES))
        cols = []
        for mcol in range(MEM_HEADS // 2):
            q = qc_ref[rows, mcol * LANES:(mcol + 1) * LANES]
            km = km_ref[:, mcol * LANES:(mcol + 1) * LANES]
            vm = vm_ref[:, mcol * LANES:(mcol + 1) * LANES]
            zero = jnp.zeros_like(q)
            heads = []
            for qh in (jnp.where(low, q, zero), jnp.where(low, zero, q)):
                s = _dot_nt(qh, km)
                e = jnp.exp(s - jnp.max(s, axis=-1, keepdims=True))
                p = e / jnp.sum(e, axis=-1, keepdims=True)
                heads.append(_dot(p.astype(BF16), vm))
            cols.append(jnp.where(low, heads[0], heads[1]))
        oc = jnp.concatenate(cols, axis=1)
        merged = merged + gate(2) * _dot(oc.astype(BF16), wc_ref[...])

        out_ref[rows, :] = x + _dot(merged.astype(BF16), wo_ref[...])

    parts = MERGE_ROW_PARTS if tm % (8 * MERGE_ROW_PARTS) == 0 else 1
    for part in range(parts):
        rows_part(slice(part * (tm // parts), (part + 1) * (tm // parts)))


def _merge(x, g_mix, w_gate, dil, ob, qc, km, vm, w_a, w_b, w_c, w_o, tm):
    b, s, d = x.shape
    mlen = km.shape[1]
    tok = lambda w: pl.BlockSpec((None, tm, w), lambda i, j: (i, j, 0))
    full = lambda *shape: pl.BlockSpec(shape, lambda i, j: (0,) * len(shape))
    memb = pl.BlockSpec((None, mlen, D_QC), lambda i, j: (i, 0, 0))
    dil_args = [t for pair in dil for t in pair]
    dil_specs = [pl.BlockSpec((None, tm // dl, dl * LANES), lambda i, j: (i, j, 0))
                 for _, dl in DIL_GROUPS for _ in range(2)]
    return pl.pallas_call(
        _merge_kernel,
        grid=(b, s // tm),
        in_specs=[tok(d), full(1, d), full(d, 3 * d)] + dil_specs + [tok(D_QB), tok(D_QC), memb, memb,
                  full(LANES, d), full(D_QB, d), full(D_QC, d), full(d, d)],
        out_specs=tok(d),
        out_shape=jax.ShapeDtypeStruct((b, s, d), F32),
        scratch_shapes=[pltpu.VMEM((tm, LANES), F32)] * 4,
        compiler_params=pltpu.CompilerParams(dimension_semantics=("arbitrary", "arbitrary"),
                                             vmem_limit_bytes=VMEM_LIMIT),
        name="merge",
    )(x, g_mix, w_gate, *dil_args, ob, qc, km, vm, w_a, w_b, w_c, w_o)


def _mlp_kernel(x_ref, g_ref, w1_ref, w2_ref, out_ref, *, fchunk):
    x = x_ref[...]
    h = _rms_rows(x, g_ref[...]).astype(BF16)
    acc = x
    for c in range(w1_ref.shape[1] // fchunk):
        u = jnp.maximum(_dot(h, w1_ref[:, c * fchunk:(c + 1) * fchunk]), 0.0)
        acc = acc + _dot((u * u).astype(BF16), w2_ref[c * fchunk:(c + 1) * fchunk, :])
    out_ref[...] = acc


def _mlp(x, g_mlp, w_1, w_2, tm):
    b, s, d = x.shape
    f = w_1.shape[1]
    tok = pl.BlockSpec((None, tm, d), lambda i, j: (i, j, 0))
    full = lambda *shape: pl.BlockSpec(shape, lambda i, j: (0,) * len(shape), pipeline_mode=pl.Buffered(1))
    return pl.pallas_call(
        functools.partial(_mlp_kernel, fchunk=min(1024, f)),
        grid=(b, s // tm),
        in_specs=[tok, full(1, d), full(d, f), full(f, d)],
        out_specs=tok,
        out_shape=jax.ShapeDtypeStruct((b, s, d), F32),
        compiler_params=pltpu.CompilerParams(dimension_semantics=("arbitrary", "arbitrary"),
                                             vmem_limit_bytes=VMEM_LIMIT),
        name="mlp",
    )(x, g_mlp, w_1, w_2)


def _rotary_tables(positions):
    inv = jnp.power(jnp.float32(ROPE_THETA), -jnp.arange(ROT_HALF, dtype=F32) / ROT_HALF)
    ang = positions.astype(F32)[:, None, :] * inv[None, :, None]
    return jnp.cos(ang), jnp.sin(ang)


def _block_diag_mean(width):
    r = jnp.arange(width) // HEAD_DIM
    return jnp.where(r[:, None] == r[None, :], 1.0 / HEAD_DIM, 0.0).astype(BF16)


def _layer(x, mem, tables, g_mix, g_mem, w_in, g_qa, g_ka, g_qb, g_kb, g_qc, g_kc,
           w_mem_kv, w_a, w_b, w_c, w_o, g_mlp, w_1, w_2):
    b, s, d = x.shape
    tm = min(TOKEN_TILE, s)
    cos_t, sin_t = tables
    bd = _block_diag_mean(D_QA)
    scale = HEAD_DIM ** -0.5

    offs, acc = [], 0
    for w in (D_QA, D_QA, D_QA, D_QB, D_KVB, D_KVB, D_QI, HEAD_DIM, IDX_HEADS, D_QC):
        offs.append((acc, acc + w))
        acc += w
    seg = lambda k: w_in[:, offs[k][0]:offs[k][1]]
    w_std = jnp.concatenate([seg(0), seg(1), seg(2), seg(4), seg(7), seg(7), seg(9)], axis=1).astype(BF16)
    w_t = jnp.concatenate([seg(3), seg(6), seg(5), seg(8), jnp.zeros((d, _R_END - _R_WI - IDX_HEADS), w_in.dtype)],
                          axis=1).T.astype(BF16)
    w_gate = w_in[:, acc:].astype(BF16)

    tile6 = lambda g: jnp.tile(g, D_QA // HEAD_DIM)
    hg = jnp.stack([tile6(g_qa) * scale, tile6(g_ka), tile6(g_kb), tile6(g_qc) * scale,
                    jnp.zeros(D_QA), jnp.zeros(D_QA), jnp.zeros(D_QA), jnp.zeros(D_QA)]).astype(F32)
    gqt = jnp.broadcast_to((g_qb * (scale * LOG2E))[:, None], (HEAD_DIM, LANES)).astype(F32)
    shift = (HEAD_DIM * scale * LOG2E * SHIFT_MARGIN) * jnp.max(jnp.abs(g_qb)) * jnp.max(jnp.abs(g_kb))
    shift = jnp.reshape(shift, (1,)).astype(F32)

    km, vm = _memkv(mem, g_mem[None, :], w_mem_kv.astype(BF16), bd[:D_QC, :D_QC],
                    jnp.tile(g_kc, MEM_HEADS)[None, :])
    (q0, q1, q2, k0, k1, k2, v0, v1, v2, kb, ki, qc, qbt, qit, vta, wit) = _inproj(
        x, g_mix[None, :], w_std, w_t, cos_t, sin_t, bd, hg, gqt, tm)
    dil = [_dilated(q, k, v, g) for g, (q, k, v) in enumerate(((q0, k0, v0), (q1, k1, v1), (q2, k2, v2)))]
    ob = _dsa(shift, qit, wit, ki, qbt, kb, vta)
    x = _merge(x, g_mix[None, :], w_gate, dil, ob, qc, km, vm, w_a.astype(BF16), w_b.astype(BF16),
               w_c.astype(BF16), w_o.astype(BF16), tm)
    return _mlp(x, g_mlp[None, :], w_1.astype(BF16), w_2.astype(BF16), tm)


def kernel(x, mem, positions, g_mix, g_mem, w_in, g_qa, g_ka, g_qb, g_kb, g_qc, g_kc, w_mem_kv, w_a, w_b, w_c, w_o, g_mlp, w_1, w_2):
    tables = _rotary_tables(positions)
    for i in range(g_mix.shape[0]):
        x = _layer(x, mem, tables, g_mix[i], g_mem[i], w_in[i], g_qa[i], g_ka[i], g_qb[i], g_kb[i], g_qc[i],
                   g_kc[i], w_mem_kv[i], w_a[i], w_b[i], w_c[i], w_o[i], g_mlp[i], w_1[i], w_2[i])
    return x
```

```python
import functools
import math

import jax
import jax.numpy as jnp
from jax import lax
from jax.experimental import pallas as pl
from jax.experimental.pallas import tpu as pltpu

F32 = jnp.float32
BF16 = jnp.bfloat16
I32 = jnp.int32
I16 = jnp.int16

LANES = 128
HEAD_DIM = 64
ROT_HALF = 8
ROPE_THETA = 500000.0
EPS = 1e-6
DIL_GROUPS = ((128, 1), (512, 4), (2048, 16))
DIL_SPAN = 128
N_DIL_HEADS = 6
DSA_Q_HEADS = 6
DSA_KV_HEADS = 2
DSA_GROUP = DSA_Q_HEADS // DSA_KV_HEADS
DSA_TOPK_MAX = 256
IDX_HEADS = 8
MEM_HEADS = 4
D_QA = N_DIL_HEADS * HEAD_DIM
D_QB = DSA_Q_HEADS * HEAD_DIM
D_KVB = DSA_KV_HEADS * HEAD_DIM
D_QI = IDX_HEADS * HEAD_DIM
D_QC = MEM_HEADS * HEAD_DIM
V_AUG = 80
LOG2E = math.log2(math.e)
NEG_BIG = -1e30
INT_MIN = -2147483648
I16_MIN = -32768
I16_MAX = 32767
FOLD_MAX_MEMBERS = 2
BF16_ONE_BITS = 0x3F80
HI_NEG_INF = -32640
SAFE_SHIFT_LOG2 = 55.0
SHIFT_MARGIN = 1.02
TOKEN_TILE = 512
DIL_TILE = 1024
N_ACC = 4
MERGE_ROW_PARTS = 2
VMEM_LIMIT = 56 * 1024 * 1024

_CONTRACT_LAST = (((1,), (1,)), ((), ()))


def _dot(a, b):
    return jnp.dot(a, b, preferred_element_type=F32)


def _dot_nt(a, b):
    return lax.dot_general(a, b, _CONTRACT_LAST, preferred_element_type=F32)


def _tile_lanes(a, reps):
    return a if reps == 1 else jnp.concatenate([a] * reps, axis=1)


def _low_half(shape):
    return (lax.broadcasted_iota(I32, shape, 1) % LANES) < HEAD_DIM


def _rms_rows(x, g):
    ms = jnp.mean(x * x, axis=-1, keepdims=True)
    return x * lax.rsqrt(ms + EPS) * g


def _norm_heads(p, bd, gain):
    sq = p * p
    hi = sq.astype(BF16)
    lo = (sq - hi.astype(F32)).astype(BF16)
    pair = bd[:LANES, :LANES]
    cols = [_dot(hi[:, c:c + LANES], pair) + _dot(lo[:, c:c + LANES], pair) for c in range(0, p.shape[1], LANES)]
    ms = cols[0] if len(cols) == 1 else jnp.concatenate(cols, axis=1)
    return p * lax.rsqrt(ms + EPS) * gain


def _rotary(y, cos, sin_lo, sin_hi):
    w = y.shape[1]
    reps = w // LANES
    c = _tile_lanes(cos, reps)
    a = _tile_lanes(sin_lo, reps)
    b = _tile_lanes(sin_hi, reps)
    return y * c + pltpu.roll(y, w - ROT_HALF, 1) * a + pltpu.roll(y, ROT_HALF, 1) * b


def _rotary_t(blk, cos_t, sin_t):
    x1, x2 = blk[:ROT_HALF], blk[ROT_HALF:2 * ROT_HALF]
    return jnp.concatenate([x1 * cos_t - x2 * sin_t, x2 * cos_t + x1 * sin_t, blk[2 * ROT_HALF:]], axis=0)


def _memkv_kernel(mem_ref, g_ref, w_ref, bd_ref, gk_ref, k_ref, v_ref):
    h = _rms_rows(mem_ref[...], g_ref[...]).astype(BF16)
    kv = _dot(h, w_ref[...])
    k_ref[...] = _norm_heads(kv[:, :D_QC], bd_ref[...], gk_ref[...]).astype(BF16)
    v_ref[...] = kv[:, D_QC:].astype(BF16)


def _memkv(mem, g_mem, w_mem_kv, bd, gk):
    b, m, d = mem.shape
    full = lambda *shape: pl.BlockSpec(shape, lambda i: (0,) * len(shape))
    return pl.pallas_call(
        _memkv_kernel,
        grid=(b,),
        in_specs=[pl.BlockSpec((None, m, d), lambda i: (i, 0, 0)), full(1, d), full(d, 2 * D_QC),
                  full(D_QC, D_QC), full(1, D_QC)],
        out_specs=[pl.BlockSpec((None, m, D_QC), lambda i: (i, 0, 0))] * 2,
        out_shape=[jax.ShapeDtypeStruct((b, m, D_QC), BF16)] * 2,
        name="memkv",
    )(mem, g_mem, w_mem_kv, bd, gk)


_C_QA, _C_KA, _C_VA, _C_KB, _C_KI, _C_QC, _C_END = 0, 384, 768, 1152, 1280, 1408, 1664
_R_QB, _R_QI, _R_VB, _R_WI, _R_END = 0, 384, 896, 1024, 1040


def _inproj_kernel(x_ref, g_ref, w_ref, wt_ref, cost_ref, sint_ref, bd_ref, hg_ref, gqt_ref,
                   q0_ref, q1_ref, q2_ref, k0_ref, k1_ref, k2_ref, v0_ref, v1_ref, v2_ref,
                   kb_ref, ki_ref, qc_ref, qbt_ref, qit_ref, vta_ref, wit_ref, dil_ref):
    tm = x_ref.shape[0]
    h = _rms_rows(x_ref[...], g_ref[...]).astype(BF16)
    bd = bd_ref[...]

    cos_t, sin_t = cost_ref[...], sint_ref[...]
    rest = HEAD_DIM - 2 * ROT_HALF
    one_r, zero_r, zero_h = jnp.ones((rest, tm), F32), jnp.zeros((rest, tm), F32), jnp.zeros((ROT_HALF, tm), F32)
    cos = jnp.concatenate([cos_t, cos_t, one_r] * (LANES // HEAD_DIM), axis=0).T
    slo = jnp.concatenate([-sin_t, zero_h, zero_r] * (LANES // HEAD_DIM), axis=0).T
    shi = jnp.concatenate([zero_h, sin_t, zero_r] * (LANES // HEAD_DIM), axis=0).T

    def proj(a, b):
        return _dot(h, w_ref[:, a:b])

    def norm_rot(a, b, gain_row):
        w = b - a
        y = _norm_heads(proj(a, b), bd[:w, :w], hg_ref[gain_row:gain_row + 1, :w])
        return _rotary(y, cos, slo, shi)

    def store_dilated(y, out_refs):
        for g, ((_, dil), out_ref) in enumerate(zip(DIL_GROUPS, out_refs)):
            yg = y[:, g * LANES:(g + 1) * LANES]
            if dil == 1:
                out_ref[...] = yg.astype(BF16)
            else:
                dil_ref[...] = yg
                for r in range(dil):
                    out_ref[:, r * LANES:(r + 1) * LANES] = dil_ref[pl.ds(r, tm // dil, stride=dil), :].astype(BF16)

    store_dilated(norm_rot(_C_QA, _C_KA, 0), (q0_ref, q1_ref, q2_ref))
    store_dilated(norm_rot(_C_KA, _C_VA, 1), (k0_ref, k1_ref, k2_ref))
    store_dilated(proj(_C_VA, _C_KB), (v0_ref, v1_ref, v2_ref))
    kb_ref[...] = norm_rot(_C_KB, _C_KI, 2).astype(BF16)
    ki_ref[...] = _rotary(proj(_C_KI, _C_QC), cos, slo, shi)[:, :HEAD_DIM].astype(BF16)
    qc_ref[...] = _norm_heads(proj(_C_QC, _C_END), bd[:D_QC, :D_QC], hg_ref[3:4, :D_QC]).astype(BF16)

    pt = _dot_nt(wt_ref[...], h)
    gq =_tile_lanes(gqt_ref[...], tm // LANES)
    for hd in range(DSA_Q_HEADS):
        blk = pt[_R_QB + hd * HEAD_DIM:_R_QB + (hd + 1) * HEAD_DIM]
        ms = jnp.mean(blk * blk, axis=0, keepdims=True)
        qbt_ref[hd * HEAD_DIM:(hd + 1) * HEAD_DIM, :] = _rotary_t(blk * lax.rsqrt(ms + EPS) * gq, cos_t, sin_t).astype(BF16)
    for hd in range(IDX_HEADS):
        blk = pt[_R_QI + hd * HEAD_DIM:_R_QI + (hd + 1) * HEAD_DIM]
        qit_ref[hd * HEAD_DIM:(hd + 1) * HEAD_DIM, :] = _rotary_t(blk, cos_t, sin_t).astype(BF16)
    vt = pt[_R_VB:_R_WI].astype(BF16)
    pad = jnp.where(lax.broadcasted_iota(I32, (V_AUG - HEAD_DIM, tm), 0) == 0, 1.0, 0.0).astype(BF16)
    vta_ref[...] = jnp.concatenate([vt[:HEAD_DIM], pad, vt[HEAD_DIM:], pad], axis=0)
    wit_ref[...] = pt[_R_WI:_R_WI + IDX_HEADS] * ((IDX_HEADS ** -0.5) * (HEAD_DIM ** -0.5))


def _inproj(x, g_mix, w_std, w_t, cos_t, sin_t, bd, hg, gqt, tm):
    b, s, d = x.shape
    tok = lambda w: pl.BlockSpec((None, tm, w), lambda i, j: (i, j, 0))
    tok_t = lambda r: pl.BlockSpec((None, r, tm), lambda i, j: (i, 0, j))
    full = lambda *shape: pl.BlockSpec(shape, lambda i, j: (0,) * len(shape))
    dils = [dl for _, dl in DIL_GROUPS] * 3
    std = ((D_KVB, BF16), (HEAD_DIM, BF16), (D_QC, BF16))
    tr = ((D_QB, BF16), (D_QI, BF16), (DSA_KV_HEADS * V_AUG, BF16), (IDX_HEADS, F32))
    return pl.pallas_call(
        _inproj_kernel,
        grid=(b, s // tm),
        in_specs=[tok(d), full(1, d), full(d, _C_END), full(_R_END, d),
                  tok_t(ROT_HALF), tok_t(ROT_HALF), full(D_QA, D_QA), full(8, D_QA), full(HEAD_DIM, LANES)],
        out_specs=[pl.BlockSpec((None, tm // dl, dl * LANES), lambda i, j: (i, j, 0)) for dl in dils]
        + [tok(w) for w, _ in std] + [tok_t(r) for r, _ in tr],
        out_shape=[jax.ShapeDtypeStruct((b, s // dl, dl * LANES), BF16) for dl in dils]
        + [jax.ShapeDtypeStruct((b, s, w), t) for w, t in std]
        + [jax.ShapeDtypeStruct((b, r, s), t) for r, t in tr],
        scratch_shapes=[pltpu.VMEM((tm, LANES), F32)],
        compiler_params=pltpu.CompilerParams(dimension_semantics=("arbitrary", "arbitrary"),
                                             vmem_limit_bytes=VMEM_LIMIT),
        name="inproj",
    )(x, g_mix, w_std, w_t, cos_t, sin_t, bd, hg, gqt)


def _dilated_kernel(q_ref, k_ref, v_ref, kp_ref, vp_ref, o_ref, l_ref, *, nsub):
    n = pl.program_id(2)
    sp = DIL_SPAN
    low = _low_half((sp, LANES))
    rq = lax.broadcasted_iota(I32, (2 * sp, 2 * sp), 0) % sp
    kj = lax.broadcasted_iota(I32, (2 * sp, 2 * sp), 1)
    dist = sp + rq - kj
    band = (dist >= 0) & (dist <= sp)
    kmin = jnp.where(n > 0, 0, sp)
    scores = []
    for i in range(nsub):
        q = q_ref[i * sp:(i + 1) * sp, :]
        kprev = kp_ref[...] if i == 0 else k_ref[(i - 1) * sp:i * sp, :]
        mask = band & (kj >= kmin) if i == 0 else band
        kk = jnp.concatenate([kprev, k_ref[i * sp:(i + 1) * sp, :]], axis=0)
        zero = jnp.zeros_like(q)
        qs = jnp.concatenate([jnp.where(low, q, zero), jnp.where(low, zero, q)], axis=0)
        scores.append(jnp.where(mask, _dot_nt(qs, kk), -jnp.inf))
    probs = []
    for i, s in enumerate(scores):
        m = jnp.max(s, axis=-1, keepdims=True)
        e = jnp.exp(s - m)
        den = jnp.sum(e, axis=-1, keepdims=True)
        probs.append((e / den).astype(BF16))
        lse = jnp.broadcast_to(m + jnp.log(den), (2 * sp, LANES))
        l_ref[i * sp:(i + 1) * sp, :] = jnp.where(low, lse[:sp], lse[sp:])
    for i, p in enumerate(probs):
        vprev = vp_ref[...] if i == 0 else v_ref[(i - 1) * sp:i * sp, :]
        o2 = _dot(p, jnp.concatenate([vprev, v_ref[i * sp:(i + 1) * sp, :]], axis=0))
        o_ref[i * sp:(i + 1) * sp, :] = jnp.where(low, o2[:sp], o2[sp:])


def _dilated(q, k, v, group):
    b, m, width = q.shape
    dilation = width // LANES
    tb = min(DIL_TILE, m)
    nsub = tb // DIL_SPAN
    cur = pl.BlockSpec((None, tb, LANES), lambda i, r, n: (i, n, r))
    prev = pl.BlockSpec((None, DIL_SPAN, LANES), lambda i, r, n: (i, jnp.maximum(n * nsub - 1, 0), r))
    return pl.pallas_call(
        functools.partial(_dilated_kernel, nsub=nsub),
        grid=(b, dilation, m // tb),
        in_specs=[cur, cur, cur, prev, prev],
        out_specs=[cur, cur],
        out_shape=[jax.ShapeDtypeStruct((b, m, width), F32)] * 2,
        compiler_params=pltpu.CompilerParams(dimension_semantics=("arbitrary",) * 3),
        name=f"dilated_g{group}",
    )(q, k, v, k, v)


def _dsa_kernel(shift_ref, qit_ref, wit_ref, ki_ref, qbt_ref, kb_ref, vta_ref, o_ref,
                hi_ref, lo_ref, sel_ref, fmax_ref, fmin_ref, blo_ref, tcnt_ref, tsel_ref, qic_ref, qbm_ref, m_ref, acc_ref,
                *, tq, ck, rb, topk):
    i = pl.program_id(1)
    nrows = (i + 1) * tq
    nfull = lax.div(nrows, ck)
    nblk = lax.div(nrows, rb)

    rows = lax.broadcasted_iota(I32, (2 * HEAD_DIM, tq), 0)
    for j in range(DSA_Q_HEADS):
        g = j // DSA_GROUP
        q = qbt_ref[j * HEAD_DIM:(j + 1) * HEAD_DIM, :]
        q2 = jnp.concatenate([q, q], axis=0)
        own = (rows >= g * HEAD_DIM) & (rows < (g + 1) * HEAD_DIM)
        qbm_ref[:, j * tq:(j + 1) * tq] = jnp.where(own, q2, jnp.zeros_like(q2))
    for h in range(IDX_HEADS):
        qic_ref[:, h * tq:(h + 1) * tq] = qit_ref[h * HEAD_DIM:(h + 1) * HEAD_DIM, :]

    def index_chunk(off, size):
        sc = _dot(ki_ref[pl.ds(off, size), :], qic_ref[...])
        acc = None
        for h in range(IDX_HEADS):
            t = jnp.maximum(sc[:, h * tq:(h + 1) * tq], 0.0) * wit_ref[h:h + 1, :]
            acc = t if acc is None else acc + t
        kpos = off + lax.broadcasted_iota(I32, (size, tq), 0)
        qpos = i * tq + lax.broadcasted_iota(I32, (size, tq), 1)
        acc = jnp.where(kpos <= qpos, acc, -jnp.inf)
        bits = lax.bitcast_convert_type(acc, I32)
        key = jnp.where(bits < 0, INT_MIN - bits, bits)
        hi_ref[pl.ds(off, size), :] = lax.shift_right_arithmetic(key, 16).astype(I16)
        lo_ref[pl.ds(off, size), :] = (key ^ 0x8000).astype(I16)

    def for_each_chunk(fn):
        def pair(t, carry):
            fn(pl.multiple_of(2 * t * ck, ck), ck)
            fn(pl.multiple_of((2 * t + 1) * ck, ck), ck)
            return carry
        lax.fori_loop(0, lax.div(nfull, 2), pair, 0)

        @pl.when(lax.rem(nfull, 2) == 1)
        def _():
            fn(pl.multiple_of((nfull - 1) * ck, ck), ck)

        @pl.when(nfull * ck < nrows)
        def _():
            fn(pl.multiple_of(nfull * ck, ck), ck // 2)

    for_each_chunk(index_chunk)


    def count(refs, pred, nb=nblk, emit=None):
        def body(r, accs):
            base = pl.multiple_of(r * rb, rb)
            blks = [ref[pl.ds(base, rb), :] for ref in refs]
            if emit is not None:
                emit(blks, base)
            accs = list(accs)
            for u in range(rb // 16):
                hit = pred([blk[u * 16:(u + 1) * 16, :] for blk in blks], base + u * 16)
                accs[u % N_ACC] = accs[u % N_ACC] + jnp.where(hit, jnp.int16(1), jnp.int16(0))
            return tuple(accs)
        accs = lax.fori_loop(0, nb, body, tuple(jnp.zeros((16, tq), I16) for _ in range(N_ACC)))
        tot = accs[0].astype(I32)
        for a in accs[1:]:
            tot = tot + a.astype(I32)
        return jnp.sum(tot, axis=0, keepdims=True)

    def rows16(v):
        return jnp.broadcast_to(v, (16, tq)).astype(I16)

    def bisect(ref, want):
        def step(it, u):
            uc = u | lax.shift_left(jnp.int32(1), 15 - it)
            cand = rows16(uc - 32768)
            cnt = count([ref], lambda b, p0: b[0] >= cand)
            return jnp.where(cnt >= want, uc, u)
        return lax.fori_loop(0, 16, step, jnp.zeros((1, tq), I32)) - 32768

    a32 = jnp.maximum(bisect(hi_ref, topk), HI_NEG_INF)
    a16 = rows16(a32)
    a_blk = jnp.broadcast_to(a32, (rb, tq)).astype(I16)

    fmax_ref[...] = jnp.full(fmax_ref.shape, I16_MIN, I16)
    fmin_ref[...] = jnp.full(fmin_ref.shape, I16_MIN, I16)

    def fold_block(r, carry):
        accs, most = carry
        base = pl.multiple_of(r * rb, rb)
        hi, lo = hi_ref[pl.ds(base, rb), :], lo_ref[pl.ds(base, rb), :]
        top = jnp.full((16, tq), I16_MIN, I16)
        low = jnp.full((16, tq), I16_MAX, I16)
        members = jnp.zeros((16, tq), I16)
        accs = list(accs)
        for u in range(rb // 16):
            h, l = hi[u * 16:(u + 1) * 16, :], lo[u * 16:(u + 1) * 16, :]
            member = h == a16
            up, down = jnp.where(member, l, jnp.int16(I16_MIN)), jnp.where(member, l, jnp.int16(I16_MAX))
            top = jnp.where(up > top, up, top)
            low = jnp.where(down < low, down, low)
            members = members + jnp.where(member, jnp.int16(1), jnp.int16(0))
            accs[u % N_ACC] = accs[u % N_ACC] + jnp.where(h > a16, jnp.int16(1), jnp.int16(0))
        row = pl.multiple_of(r * 16, 16)
        fmax_ref[pl.ds(row, 16), :] = top
        fmin_ref[pl.ds(row, 16), :] = jnp.where(members >= 2, low, jnp.int16(I16_MIN))
        return tuple(accs), jnp.where(members > most, members, most)

    accs, most = lax.fori_loop(0, nblk, fold_block, (tuple(jnp.zeros((16, tq), I16) for _ in range(N_ACC)),
                                                     jnp.zeros((16, tq), I16)))
    n_hi = accs[0].astype(I32)
    for a in accs[1:]:
        n_hi = n_hi + a.astype(I32)
    n_hi = jnp.sum(n_hi, axis=0, keepdims=True)
    want_lo = topk - n_hi

    def folded_step(it, u):
        uc = u | lax.shift_left(jnp.int32(1), 15 - it)
        cand = rows16(uc - 32768)
        accs = [jnp.zeros((16, tq), I16) for _ in range(N_ACC)]
        for r in range(fmax_ref.shape[0] // 16):
            for k, ref in enumerate((fmax_ref, fmin_ref)):
                hit = ref[r * 16:(r + 1) * 16, :] >= cand
                accs[(2 * r + k) % N_ACC] = accs[(2 * r + k) % N_ACC] + jnp.where(hit, jnp.int16(1), jnp.int16(0))
        tot = accs[0].astype(I32)
        for a in accs[1:]:
            tot = tot + a.astype(I32)
        return jnp.where(jnp.sum(tot, axis=0, keepdims=True) >= want_lo, uc, u)

    blo_ref[...] = jnp.broadcast_to(lax.fori_loop(0, 16, folded_step, jnp.zeros((1, tq), I32)) - 32768, blo_ref.shape)

    @pl.when(jnp.max(most.astype(I32)) > FOLD_MAX_MEMBERS)
    def _():
        def bucket(r, carry):
            base = pl.multiple_of(r * rb, rb)
            sel_ref[pl.ds(base, rb), :] = jnp.where(hi_ref[pl.ds(base, rb), :] == a_blk, lo_ref[pl.ds(base, rb), :],
                                                    jnp.int16(I16_MIN))
            return carry
        lax.fori_loop(0, nblk, bucket, 0)
        blo_ref[...] = jnp.broadcast_to(bisect(sel_ref, want_lo), blo_ref.shape)

    b32 = blo_ref[0:1, :]
    b32 = jnp.where(a32 == HI_NEG_INF, jnp.maximum(b32, I16_MIN + 1), b32)
    b16 = rows16(b32)
    b_blk = jnp.broadcast_to(b32, (rb, tq)).astype(I16)

    def emit_selection(blks, base):
        sel = (blks[0] > a_blk) | ((blks[0] == a_blk) & (blks[1] >= b_blk))
        sel_ref[pl.ds(base, rb), :] = jnp.where(sel, jnp.int16(BF16_ONE_BITS), jnp.int16(0))

    n_ge = count([hi_ref, lo_ref], lambda b, p0: (b[0] > a16) | ((b[0] == a16) & (b[1] >= b16)),
                 emit=emit_selection)

    need = n_ge > topk

    @pl.when(jnp.max(jnp.where(need, 1, 0)) > 0)
    def _():
        nb_max = tcnt_ref.shape[0]
        none = jnp.int32(2 ** 30)

        def tied(hi, lo, a, b):
            return (hi == a) & (lo == b)

        tcnt_ref[...] = jnp.zeros(tcnt_ref.shape, I32)

        def tie_block(r, carry):
            base = pl.multiple_of(r * rb, rb)
            hi, lo = hi_ref[pl.ds(base, rb), :], lo_ref[pl.ds(base, rb), :]
            acc = jnp.zeros((16, tq), I16)
            for u in range(rb // 16):
                acc = acc + jnp.where(tied(hi[u * 16:(u + 1) * 16, :], lo[u * 16:(u + 1) * 16, :], a16, b16),
                                      jnp.int16(1), jnp.int16(0))
            tcnt_ref[r] = jnp.broadcast_to(jnp.sum(acc.astype(I32), axis=0, keepdims=True), (8, tq))
            return carry

        lax.fori_loop(0, nblk, tie_block, 0)

        per_block = [tcnt_ref[r][0:1, :] for r in range(nb_max)]
        n_tied = per_block[0]
        for c in per_block[1:]:
            n_tied = n_tied + c
        want = jnp.where(need, topk - (n_ge - n_tied), none)
        last = jnp.full((1, tq), nb_max, I32)
        before = jnp.zeros((1, tq), I32)
        prefix = jnp.zeros((1, tq), I32)
        for r, c in enumerate(per_block):
            reached = (last == nb_max) & (prefix + c >= want)
            last = jnp.where(reached, r, last)
            before = jnp.where(reached, prefix, before)
            prefix = prefix + c
        want_here = want - before

        tsel_ref[...] = jnp.zeros(tsel_ref.shape, I16)

        def pick_block(r, carry):
            base = pl.multiple_of(r * rb, rb)
            here = jnp.broadcast_to(jnp.where(last == r, 1, 0), (rb, tq)).astype(I16) != 0
            hit = tied(hi_ref[pl.ds(base, rb), :], lo_ref[pl.ds(base, rb), :], a_blk, b_blk) & here
            tsel_ref[...] = tsel_ref[...] + jnp.where(hit, jnp.int16(1), jnp.int16(0))
            return carry

        lax.fori_loop(0, nblk, pick_block, 0)

        in_block = lax.broadcasted_iota(I32, (rb, tq), 0).astype(I16)
        block_bits = (rb - 1).bit_length()

        def row_step(it, e):
            ec = e | lax.shift_left(jnp.int32(1), block_bits - 1 - it)
            hit = (tsel_ref[...] != 0) & (in_block < jnp.broadcast_to(ec, (rb, tq)).astype(I16))
            ones = jnp.where(hit, jnp.int16(1), jnp.int16(0))
            acc = ones[0:16, :]
            for u in range(1, rb // 16):
                acc = acc + ones[u * 16:(u + 1) * 16, :]
            below = jnp.sum(acc.astype(I32), axis=0, keepdims=True)
            return jnp.where(below < want_here, ec, e)

        e_last = lax.fori_loop(0, block_bits, row_step, jnp.zeros((1, tq), I32))
        x = jnp.where(last < nb_max, jnp.minimum(last * rb + e_last, I16_MAX), I16_MAX)

        def demote(r, carry):
            base = pl.multiple_of(r * rb, rb)
            pos = (base + lax.broadcasted_iota(I32, (rb, tq), 0)).astype(I16)
            drop = (tied(hi_ref[pl.ds(base, rb), :], lo_ref[pl.ds(base, rb), :], a_blk, b_blk)
                    & (pos > jnp.broadcast_to(x, (rb, tq)).astype(I16)))
            sel_ref[pl.ds(base, rb), :] = jnp.where(drop, jnp.int16(0), sel_ref[pl.ds(base, rb), :])
            return carry

        lax.fori_loop(0, nblk, demote, 0)

    shift = shift_ref[0]

    @pl.when(shift < SAFE_SHIFT_LOG2)
    def _():
        acc_ref[...] = jnp.zeros(acc_ref.shape, F32)

        def attend_chunk(off, size):
            kch = kb_ref[pl.ds(off, size), :]
            msk = lax.bitcast_convert_type(sel_ref[pl.ds(off, size), :], BF16)
            for g in range(DSA_KV_HEADS):
                sc = _dot(kch, qbm_ref[:, g * DSA_GROUP * tq:(g + 1) * DSA_GROUP * tq])
                p = jnp.exp2(sc - shift).astype(BF16) * _tile_lanes(msk, DSA_GROUP)
                acc_ref[g] += _dot(vta_ref[g * V_AUG:(g + 1) * V_AUG, pl.ds(off, size)], p)

        for_each_chunk(attend_chunk)

    @pl.when(shift >= SAFE_SHIFT_LOG2)
    def _():
        m_ref[...] = jnp.full(m_ref.shape, NEG_BIG, F32)
        acc_ref[...] = jnp.zeros(acc_ref.shape, F32)

        def attend_block(r, carry):
            off = pl.multiple_of(r * rb, rb)
            kch = kb_ref[pl.ds(off, rb), :]
            sel = sel_ref[pl.ds(off, rb), :].astype(I32) != 0
            for j in range(DSA_Q_HEADS):
                g = j // DSA_GROUP
                cols = slice((j % DSA_GROUP) * tq, (j % DSA_GROUP + 1) * tq)
                sc = jnp.where(sel, _dot(kch, qbm_ref[:, j * tq:(j + 1) * tq]), NEG_BIG)
                m_prev = m_ref[j]
                m_new = jnp.maximum(m_prev, jnp.max(sc, axis=0, keepdims=True))
                p = jnp.where(sel, jnp.exp2(sc - m_new), 0.0).astype(BF16)
                acc_ref[g, :, cols] = jnp.exp2(m_prev - m_new) * acc_ref[g, :, cols] + _dot(
                    vta_ref[g * V_AUG:(g + 1) * V_AUG, pl.ds(off, rb)], p)
                m_ref[j] = m_new
            return carry

        lax.fori_loop(0, nblk, attend_block, 0)

    for pair in range(DSA_Q_HEADS // 2):
        halves = []
        for j in (2 * pair, 2 * pair + 1):
            a = acc_ref[j // DSA_GROUP, :, (j % DSA_GROUP) * tq:(j % DSA_GROUP + 1) * tq]
            halves.append(a[:HEAD_DIM] / a[HEAD_DIM:HEAD_DIM + 1])
        o_ref[:, pair * LANES:(pair + 1) * LANES] = jnp.concatenate(halves, axis=0).T.astype(BF16)


def _dsa(shift, qit, wit, ki, qbt, kb, vta):
    b, _, s = qbt.shape
    tq = min(256, s)
    ck = min(512, s)
    rb = 256
    topk = min(DSA_TOPK_MAX, s // 4)
    qt = lambda w: pl.BlockSpec((None, w, tq), lambda i, j: (i, 0, j))
    seq = lambda w: pl.BlockSpec((None, s, w), lambda i, j: (i, 0, 0))
    return pl.pallas_call(
        functools.partial(_dsa_kernel, tq=tq, ck=ck, rb=rb, topk=topk),
        grid=(b, s // tq),
        in_specs=[pl.BlockSpec(memory_space=pltpu.SMEM), qt(D_QI), qt(IDX_HEADS), seq(HEAD_DIM), qt(D_QB), seq(D_KVB),
                  pl.BlockSpec((None, DSA_KV_HEADS * V_AUG, s), lambda i, j: (i, 0, 0))],
        out_specs=pl.BlockSpec((None, tq, D_QB), lambda i, j: (i, j, 0)),
        out_shape=jax.ShapeDtypeStruct((b, s, D_QB), BF16),
        scratch_shapes=[
            pltpu.VMEM((s, tq), I16),
            pltpu.VMEM((s, tq), I16),
            pltpu.VMEM((s, tq), I16),
            pltpu.VMEM((s // rb * 16, tq), I16),
            pltpu.VMEM((s // rb * 16, tq), I16),
            pltpu.VMEM((8, tq), I32),
            pltpu.VMEM((s // rb, 8, tq), I32),
            pltpu.VMEM((rb, tq), I16),
            pltpu.VMEM((HEAD_DIM, IDX_HEADS * tq), BF16),
            pltpu.VMEM((2 * HEAD_DIM, DSA_Q_HEADS * tq), BF16),
            pltpu.VMEM((DSA_Q_HEADS, 1, tq), F32),
            pltpu.VMEM((DSA_KV_HEADS, V_AUG, DSA_GROUP * tq), F32),
        ],
        compiler_params=pltpu.CompilerParams(dimension_semantics=("arbitrary", "arbitrary"),
                                             vmem_limit_bytes=VMEM_LIMIT),
        name="dsa",
    )(shift, qit, wit, ki, qbt, kb, vta)


def _merge_kernel(x_ref, g_ref, wg_ref, o0_ref, l0_ref, o1_ref, l1_ref, o2_ref, l2_ref, ob_ref, qc_ref,
                  km_ref, vm_ref, wa_ref, wb_ref, wc_ref, wo_ref, out_ref, *stage_refs):
    tm, d = x_ref.shape

    def token_major(ref, stage_ref):
        dil = ref.shape[1] // LANES
        if dil == 1:
            return ref[...]
        for r in range(dil):
            stage_ref[pl.ds(r, tm // dil, stride=dil), :] = ref[:, r * LANES:(r + 1) * LANES]
        return stage_ref[...]

    lses = (l0_ref[...], token_major(l1_ref, stage_refs[0]), token_major(l2_ref, stage_refs[1]))
    outs = (o0_ref[...], token_major(o1_ref, stage_refs[2]), token_major(o2_ref, stage_refs[3]))

    def rows_part(rows):
        n = rows.stop - rows.start
        x = x_ref[rows, :]
        h = _rms_rows(x, g_ref[...]).astype(BF16)

        def gate(k):
            z = _dot(h, wg_ref[:, k * d:(k + 1) * d])
            return 1.0 / (1.0 + jnp.exp(-z))

        l0, l1, l2 = (t[rows, :] for t in lses)
        o0, o1, o2 = (t[rows, :] for t in outs)
        mx = jnp.maximum(jnp.maximum(l0, l1), l2)
        e0, e1, e2 = jnp.exp(l0 - mx), jnp.exp(l1 - mx), jnp.exp(l2 - mx)
        oa = (e0 * o0 + e1 * o1 + e2 * o2) / (e0 + e1 + e2)
        merged = gate(0) * _dot(oa.astype(BF16), wa_ref[...])

        merged = merged + gate(1) * _dot(ob_ref[rows, :], wb_ref[...])

        low = _low_half((n, LANES))
        cols = []
        for mcol in range(MEM_HEADS // 2):
            q = qc_ref[rows, mcol * LANES:(mcol + 1) * LANES]
            km = km_ref[:, mcol * LANES:(mcol + 1) * LANES]
            vm = vm_ref[:, mcol * LANES:(mcol + 1) * LANES]
            zero = jnp.zeros_like(q)
            heads = []
            for qh in (jnp.where(low, q, zero), jnp.where(low, zero, q)):
                s = _dot_nt(qh, km)
                e = jnp.exp(s - jnp.max(s, axis=-1, keepdims=True))
                p = e / jnp.sum(e, axis=-1, keepdims=True)
                heads.append(_dot(p.astype(BF16), vm))
            cols.append(jnp.where(low, heads[0], heads[1]))
        oc = jnp.concatenate(cols, axis=1)
        merged = merged + gate(2) * _dot(oc.astype(BF16), wc_ref[...])

        out_ref[rows, :] = x + _dot(merged.astype(BF16), wo_ref[...])

    parts = MERGE_ROW_PARTS if tm % (8 * MERGE_ROW_PARTS) == 0 else 1
    for part in range(parts):
        rows_part(slice(part * (tm // parts), (part + 1) * (tm // parts)))


def _merge(x, g_mix, w_gate, dil, ob, qc, km, vm, w_a, w_b, w_c, w_o, tm):
    b, s, d = x.shape
    mlen = km.shape[1]
    tok = lambda w: pl.BlockSpec((None, tm, w), lambda i, j: (i, j, 0))
    full = lambda *shape: pl.BlockSpec(shape, lambda i, j: (0,) * len(shape))
    memb = pl.BlockSpec((None, mlen, D_QC), lambda i, j: (i, 0, 0))
    dil_args = [t for pair in dil for t in pair]
    dil_specs = [pl.BlockSpec((None, tm // dl, dl * LANES), lambda i, j: (i, j, 0))
                 for _, dl in DIL_GROUPS for _ in range(2)]
    return pl.pallas_call(
        _merge_kernel,
        grid=(b, s // tm),
        in_specs=[tok(d), full(1, d), full(d, 3 * d)] + dil_specs + [tok(D_QB), tok(D_QC), memb, memb,
                  full(LANES, d), full(D_QB, d), full(D_QC, d), full(d, d)],
        out_specs=tok(d),
        out_shape=jax.ShapeDtypeStruct((b, s, d), F32),
        scratch_shapes=[pltpu.VMEM((tm, LANES), F32)] * 4,
        compiler_params=pltpu.CompilerParams(dimension_semantics=("arbitrary", "arbitrary"),
                                             vmem_limit_bytes=VMEM_LIMIT),
        name="merge",
    )(x, g_mix, w_gate, *dil_args, ob, qc, km, vm, w_a, w_b, w_c, w_o)


def _mlp_kernel(x_ref, g_ref, w1_ref, w2_ref, out_ref, *, fchunk):
    x = x_ref[...]
    h = _rms_rows(x, g_ref[...]).astype(BF16)
    acc = x
    for c in range(w1_ref.shape[1] // fchunk):
        u = jnp.maximum(_dot(h, w1_ref[:, c * fchunk:(c + 1) * fchunk]), 0.0)
        acc = acc + _dot((u * u).astype(BF16), w2_ref[c * fchunk:(c + 1) * fchunk, :])
    out_ref[...] = acc


def _mlp(x, g_mlp, w_1, w_2, tm):
    b, s, d = x.shape
    f = w_1.shape[1]
    tok = pl.BlockSpec((None, tm, d), lambda i, j: (i, j, 0))
    full = lambda *shape: pl.BlockSpec(shape, lambda i, j: (0,) * len(shape), pipeline_mode=pl.Buffered(1))
    return pl.pallas_call(
        functools.partial(_mlp_kernel, fchunk=min(1024, f)),
        grid=(b, s // tm),
        in_specs=[tok, full(1, d), full(d, f), full(f, d)],
        out_specs=tok,
        out_shape=jax.ShapeDtypeStruct((b, s, d), F32),
        compiler_params=pltpu.CompilerParams(dimension_semantics=("arbitrary", "arbitrary"),
                                             vmem_limit_bytes=VMEM_LIMIT),
        name="mlp",
    )(x, g_mlp, w_1, w_2)


def _rotary_tables(positions):
    inv = jnp.power(jnp.float32(ROPE_THETA), -jnp.arange(ROT_HALF, dtype=F32) / ROT_HALF)
    ang = positions.astype(F32)[:, None, :] * inv[None, :, None]
    return jnp.cos(ang), jnp.sin(ang)


def _block_diag_mean(width):
    r = jnp.arange(width) // HEAD_DIM
    return jnp.where(r[:, None] == r[None, :], 1.0 / HEAD_DIM, 0.0).astype(BF16)


def _layer(x, mem, tables, g_mix, g_mem, w_in, g_qa, g_ka, g_qb, g_kb, g_qc, g_kc,
           w_mem_kv, w_a, w_b, w_c, w_o, g_mlp, w_1, w_2):
    b, s, d = x.shape
    tm = min(TOKEN_TILE, s)
    cos_t, sin_t = tables
    bd = _block_diag_mean(D_QA)
    scale = HEAD_DIM ** -0.5

    offs, acc = [], 0
    for w in (D_QA, D_QA, D_QA, D_QB, D_KVB, D_KVB, D_QI, HEAD_DIM, IDX_HEADS, D_QC):
        offs.append((acc, acc + w))
        acc += w
    seg = lambda k: w_in[:, offs[k][0]:offs[k][1]]
    w_std = jnp.concatenate([seg(0), seg(1), seg(2), seg(4), seg(7), seg(7), seg(9)], axis=1).astype(BF16)
    w_t = jnp.concatenate([seg(3), seg(6), seg(5), seg(8), jnp.zeros((d, _R_END - _R_WI - IDX_HEADS), w_in.dtype)],
                          axis=1).T.astype(BF16)
    w_gate = w_in[:, acc:].astype(BF16)

    tile6 = lambda g: jnp.tile(g, D_QA // HEAD_DIM)
    hg = jnp.stack([tile6(g_qa) * scale, tile6(g_ka), tile6(g_kb), tile6(g_qc) * scale,
                    jnp.zeros(D_QA), jnp.zeros(D_QA), jnp.zeros(D_QA), jnp.zeros(D_QA)]).astype(F32)
    gqt = jnp.broadcast_to((g_qb * (scale * LOG2E))[:, None], (HEAD_DIM, LANES)).astype(F32)
    shift = (HEAD_DIM * scale * LOG2E * SHIFT_MARGIN) * jnp.max(jnp.abs(g_qb)) * jnp.max(jnp.abs(g_kb))
    shift = jnp.reshape(shift, (1,)).astype(F32)

    km, vm = _memkv(mem, g_mem[None, :], w_mem_kv.astype(BF16), bd[:D_QC, :D_QC],
                    jnp.tile(g_kc, MEM_HEADS)[None, :])
    (q0, q1, q2, k0, k1, k2, v0, v1, v2, kb, ki, qc, qbt, qit, vta, wit) = _inproj(
        x, g_mix[None, :], w_std, w_t, cos_t, sin_t, bd, hg, gqt, tm)
    dil = [_dilated(q, k, v, g) for g, (q, k, v) in enumerate(((q0, k0, v0), (q1, k1, v1), (q2, k2, v2)))]
    ob = _dsa(shift, qit, wit, ki, qbt, kb, vta)
    x = _merge(x, g_mix[None, :], w_gate, dil, ob, qc, km, vm, w_a.astype(BF16), w_b.astype(BF16),
               w_c.astype(BF16), w_o.astype(BF16), tm)
    return _mlp(x, g_mlp[None, :], w_1.astype(BF16), w_2.astype(BF16), tm)


def kernel(x, mem, positions, g_mix, g_mem, w_in, g_qa, g_ka, g_qb, g_kb, g_qc, g_kc, w_mem_kv, w_a, w_b, w_c, w_o, g_mlp, w_1, w_2):
    tables = _rotary_tables(positions)
    for i in range(g_mix.shape[0]):
        x = _layer(x, mem, tables, g_mix[i], g_mem[i], w_in[i], g_qa[i], g_ka[i], g_qb[i], g_kb[i], g_qc[i],
                   g_kc[i], w_mem_kv[i], w_a[i], w_b[i], w_c[i], w_o[i], g_mlp[i], w_1[i], w_2[i])
    return x
```

```python
import functools
import math

import jax
import jax.numpy as jnp
from jax import lax
from jax.experimental import pallas as pl
from jax.experimental.pallas import tpu as pltpu

F32 = jnp.float32
BF16 = jnp.bfloat16
I32 = jnp.int32
I16 = jnp.int16

LANES = 128
HEAD_DIM = 64
ROT_HALF = 8
ROPE_THETA = 500000.0
EPS = 1e-6
DIL_GROUPS = ((128, 1), (512, 4), (2048, 16))
DIL_SPAN = 128
N_DIL_HEADS = 6
DSA_Q_HEADS = 6
DSA_KV_HEADS = 2
DSA_GROUP = DSA_Q_HEADS // DSA_KV_HEADS
DSA_TOPK_MAX = 256
IDX_HEADS = 8
MEM_HEADS = 4
D_QA = N_DIL_HEADS * HEAD_DIM
D_QB = DSA_Q_HEADS * HEAD_DIM
D_KVB = DSA_KV_HEADS * HEAD_DIM
D_QI = IDX_HEADS * HEAD_DIM
D_QC = MEM_HEADS * HEAD_DIM
V_AUG = 80
LOG2E = math.log2(math.e)
NEG_BIG = -1e30
INT_MIN = -2147483648
I16_MIN = -32768
I16_MAX = 32767
FOLD_MAX_MEMBERS = 2
BF16_ONE_BITS = 0x3F80
HI_NEG_INF = -32640
SAFE_SHIFT_LOG2 = 55.0
SHIFT_MARGIN = 1.02
TOKEN_TILE = 512
DIL_TILE = 1024
N_ACC = 4
MERGE_ROW_PARTS = 2
VMEM_LIMIT = 56 * 1024 * 1024

_CONTRACT_LAST = (((1,), (1,)), ((), ()))


def _dot(a, b):
    return jnp.dot(a, b, preferred_element_type=F32)


def _dot_nt(a, b):
    return lax.dot_general(a, b, _CONTRACT_LAST, preferred_element_type=F32)


def _tile_lanes(a, reps):
    return a if reps == 1 else jnp.concatenate([a] * reps, axis=1)


def _low_half(shape):
    return (lax.broadcasted_iota(I32, shape, 1) % LANES) < HEAD_DIM


def _rms_rows(x, g):
    ms = jnp.mean(x * x, axis=-1, keepdims=True)
    return x * lax.rsqrt(ms + EPS) * g


def _norm_heads(p, bd, gain):
    sq = p * p
    hi = sq.astype(BF16)
    lo = (sq - hi.astype(F32)).astype(BF16)
    pair = bd[:LANES, :LANES]
    cols = [_dot(hi[:, c:c + LANES], pair) + _dot(lo[:, c:c + LANES], pair) for c in range(0, p.shape[1], LANES)]
    ms = cols[0] if len(cols) == 1 else jnp.concatenate(cols, axis=1)
    return p * lax.rsqrt(ms + EPS) * gain


def _rotary(y, cos, sin_lo, sin_hi):
    w = y.shape[1]
    reps = w // LANES
    c = _tile_lanes(cos, reps)
    a = _tile_lanes(sin_lo, reps)
    b = _tile_lanes(sin_hi, reps)
    return y * c + pltpu.roll(y, w - ROT_HALF, 1) * a + pltpu.roll(y, ROT_HALF, 1) * b


def _rotary_t(blk, cos_t, sin_t):
    x1, x2 = blk[:ROT_HALF], blk[ROT_HALF:2 * ROT_HALF]
    return jnp.concatenate([x1 * cos_t - x2 * sin_t, x2 * cos_t + x1 * sin_t, blk[2 * ROT_HALF:]], axis=0)


def _memkv_kernel(mem_ref, g_ref, w_ref, bd_ref, gk_ref, k_ref, v_ref):
    h = _rms_rows(mem_ref[...], g_ref[...]).astype(BF16)
    kv = _dot(h, w_ref[...])
    k_ref[...] = _norm_heads(kv[:, :D_QC], bd_ref[...], gk_ref[...]).astype(BF16)
    v_ref[...] = kv[:, D_QC:].astype(BF16)


def _memkv(mem, g_mem, w_mem_kv, bd, gk):
    b, m, d = mem.shape
    full = lambda *shape: pl.BlockSpec(shape, lambda i: (0,) * len(shape))
    return pl.pallas_call(
        _memkv_kernel,
        grid=(b,),
        in_specs=[pl.BlockSpec((None, m, d), lambda i: (i, 0, 0)), full(1, d), full(d, 2 * D_QC),
                  full(D_QC, D_QC), full(1, D_QC)],
        out_specs=[pl.BlockSpec((None, m, D_QC), lambda i: (i, 0, 0))] * 2,
        out_shape=[jax.ShapeDtypeStruct((b, m, D_QC), BF16)] * 2,
        name="memkv",
    )(mem, g_mem, w_mem_kv, bd, gk)


_C_QA, _C_KA, _C_VA, _C_KB, _C_KI, _C_QC, _C_END = 0, 384, 768, 1152, 1280, 1408, 1664
_R_QB, _R_QI, _R_VB, _R_WI, _R_END = 0, 384, 896, 1024, 1040


def _inproj_kernel(x_ref, g_ref, w_ref, wt_ref, cost_ref, sint_ref, bd_ref, hg_ref, gqt_ref,
                   q0_ref, q1_ref, q2_ref, k0_ref, k1_ref, k2_ref, v0_ref, v1_ref, v2_ref,
                   kb_ref, ki_ref, qc_ref, qbt_ref, qit_ref, vta_ref, wit_ref, dil_ref):
    tm = x_ref.shape[0]
    h = _rms_rows(x_ref[...], g_ref[...]).astype(BF16)
    bd = bd_ref[...]

    cos_t, sin_t = cost_ref[...], sint_ref[...]
    rest = HEAD_DIM - 2 * ROT_HALF
    one_r, zero_r, zero_h = jnp.ones((rest, tm), F32), jnp.zeros((rest, tm), F32), jnp.zeros((ROT_HALF, tm), F32)
    cos = jnp.concatenate([cos_t, cos_t, one_r] * (LANES // HEAD_DIM), axis=0).T
    slo = jnp.concatenate([-sin_t, zero_h, zero_r] * (LANES // HEAD_DIM), axis=0).T
    shi = jnp.concatenate([zero_h, sin_t, zero_r] * (LANES // HEAD_DIM), axis=0).T

    def proj(a, b):
        return _dot(h, w_ref[:, a:b])

    def norm_rot(a, b, gain_row):
        w = b - a
        y = _norm_heads(proj(a, b), bd[:w, :w], hg_ref[gain_row:gain_row + 1, :w])
        return _rotary(y, cos, slo, shi)

    def store_dilated(y, out_refs):
        for g, ((_, dil), out_ref) in enumerate(zip(DIL_GROUPS, out_refs)):
            yg = y[:, g * LANES:(g + 1) * LANES]
            if dil == 1:
                out_ref[...] = yg.astype(BF16)
            else:
                dil_ref[...] = yg
                for r in range(dil):
                    out_ref[:, r * LANES:(r + 1) * LANES] = dil_ref[pl.ds(r, tm // dil, stride=dil), :].astype(BF16)

    store_dilated(norm_rot(_C_QA, _C_KA, 0), (q0_ref, q1_ref, q2_ref))
    store_dilated(norm_rot(_C_KA, _C_VA, 1), (k0_ref, k1_ref, k2_ref))
    store_dilated(proj(_C_VA, _C_KB), (v0_ref, v1_ref, v2_ref))
    kb_ref[...] = norm_rot(_C_KB, _C_KI, 2).astype(BF16)
    ki_ref[...] = _rotary(proj(_C_KI, _C_QC), cos, slo, shi)[:, :HEAD_DIM].astype(BF16)
    qc_ref[...] = _norm_heads(proj(_C_QC, _C_END), bd[:D_QC, :D_QC], hg_ref[3:4, :D_QC]).astype(BF16)

    pt = _dot_nt(wt_ref[...], h)
    gq =_tile_lanes(gqt_ref[...], tm // LANES)
    for hd in range(DSA_Q_HEADS):
        blk = pt[_R_QB + hd * HEAD_DIM:_R_QB + (hd + 1) * HEAD_DIM]
        ms = jnp.mean(blk * blk, axis=0, keepdims=True)
        qbt_ref[hd * HEAD_DIM:(hd + 1) * HEAD_DIM, :] = _rotary_t(blk * lax.rsqrt(ms + EPS) * gq, cos_t, sin_t).astype(BF16)
    for hd in range(IDX_HEADS):
        blk = pt[_R_QI + hd * HEAD_DIM:_R_QI + (hd + 1) * HEAD_DIM]
        qit_ref[hd * HEAD_DIM:(hd + 1) * HEAD_DIM, :] = _rotary_t(blk, cos_t, sin_t).astype(BF16)
    vt = pt[_R_VB:_R_WI].astype(BF16)
    pad = jnp.where(lax.broadcasted_iota(I32, (V_AUG - HEAD_DIM, tm), 0) == 0, 1.0, 0.0).astype(BF16)
    vta_ref[...] = jnp.concatenate([vt[:HEAD_DIM], pad, vt[HEAD_DIM:], pad], axis=0)
    wit_ref[...] = pt[_R_WI:_R_WI + IDX_HEADS] * ((IDX_HEADS ** -0.5) * (HEAD_DIM ** -0.5))


def _inproj(x, g_mix, w_std, w_t, cos_t, sin_t, bd, hg, gqt, tm):
    b, s, d = x.shape
    tok = lambda w: pl.BlockSpec((None, tm, w), lambda i, j: (i, j, 0))
    tok_t = lambda r: pl.BlockSpec((None, r, tm), lambda i, j: (i, 0, j))
    full = lambda *shape: pl.BlockSpec(shape, lambda i, j: (0,) * len(shape))
    dils = [dl for _, dl in DIL_GROUPS] * 3
    std = ((D_KVB, BF16), (HEAD_DIM, BF16), (D_QC, BF16))
    tr = ((D_QB, BF16), (D_QI, BF16), (DSA_KV_HEADS * V_AUG, BF16), (IDX_HEADS, F32))
    return pl.pallas_call(
        _inproj_kernel,
        grid=(b, s // tm),
        in_specs=[tok(d), full(1, d), full(d, _C_END), full(_R_END, d),
                  tok_t(ROT_HALF), tok_t(ROT_HALF), full(D_QA, D_QA), full(8, D_QA), full(HEAD_DIM, LANES)],
        out_specs=[pl.BlockSpec((None, tm // dl, dl * LANES), lambda i, j: (i, j, 0)) for dl in dils]
        + [tok(w) for w, _ in std] + [tok_t(r) for r, _ in tr],
        out_shape=[jax.ShapeDtypeStruct((b, s // dl, dl * LANES), BF16) for dl in dils]
        + [jax.ShapeDtypeStruct((b, s, w), t) for w, t in std]
        + [jax.ShapeDtypeStruct((b, r, s), t) for r, t in tr],
        scratch_shapes=[pltpu.VMEM((tm, LANES), F32)],
        compiler_params=pltpu.CompilerParams(dimension_semantics=("arbitrary", "arbitrary"),
                                             vmem_limit_bytes=VMEM_LIMIT),
        name="inproj",
    )(x, g_mix, w_std, w_t, cos_t, sin_t, bd, hg, gqt)


def _dilated_kernel(q_ref, k_ref, v_ref, kp_ref, vp_ref, o_ref, l_ref, *, nsub):
    n = pl.program_id(2)
    sp = DIL_SPAN
    low = _low_half((sp, LANES))
    rq = lax.broadcasted_iota(I32, (2 * sp, 2 * sp), 0) % sp
    kj = lax.broadcasted_iota(I32, (2 * sp, 2 * sp), 1)
    dist = sp + rq - kj
    band = (dist >= 0) & (dist <= sp)
    kmin = jnp.where(n > 0, 0, sp)
    scores = []
    for i in range(nsub):
        q = q_ref[i * sp:(i + 1) * sp, :]
        kprev = kp_ref[...] if i == 0 else k_ref[(i - 1) * sp:i * sp, :]
        mask = band & (kj >= kmin) if i == 0 else band
        kk = jnp.concatenate([kprev, k_ref[i * sp:(i + 1) * sp, :]], axis=0)
        zero = jnp.zeros_like(q)
        qs = jnp.concatenate([jnp.where(low, q, zero), jnp.where(low, zero, q)], axis=0)
        scores.append(jnp.where(mask, _dot_nt(qs, kk), -jnp.inf))
    probs = []
    for i, s in enumerate(scores):
        m = jnp.max(s, axis=-1, keepdims=True)
        e = jnp.exp(s - m)
        den = jnp.sum(e, axis=-1, keepdims=True)
        probs.append((e / den).astype(BF16))
        lse = jnp.broadcast_to(m + jnp.log(den), (2 * sp, LANES))
        l_ref[i * sp:(i + 1) * sp, :] = jnp.where(low, lse[:sp], lse[sp:])
    for i, p in enumerate(probs):
        vprev = vp_ref[...] if i == 0 else v_ref[(i - 1) * sp:i * sp, :]
        o2 = _dot(p, jnp.concatenate([vprev, v_ref[i * sp:(i + 1) * sp, :]], axis=0))
        o_ref[i * sp:(i + 1) * sp, :] = jnp.where(low, o2[:sp], o2[sp:])


def _dilated(q, k, v, group):
    b, m, width = q.shape
    dilation = width // LANES
    tb = min(DIL_TILE, m)
    nsub = tb // DIL_SPAN
    cur = pl.BlockSpec((None, tb, LANES), lambda i, r, n: (i, n, r))
    prev = pl.BlockSpec((None, DIL_SPAN, LANES), lambda i, r, n: (i, jnp.maximum(n * nsub - 1, 0), r))
    return pl.pallas_call(
        functools.partial(_dilated_kernel, nsub=nsub),
        grid=(b, dilation, m // tb),
        in_specs=[cur, cur, cur, prev, prev],
        out_specs=[cur, cur],
        out_shape=[jax.ShapeDtypeStruct((b, m, width), F32)] * 2,
        compiler_params=pltpu.CompilerParams(dimension_semantics=("arbitrary",) * 3),
        name=f"dilated_g{group}",
    )(q, k, v, k, v)


def _dsa_kernel(shift_ref, qit_ref, wit_ref, ki_ref, qbt_ref, kb_ref, vta_ref, o_ref,
                hi_ref, lo_ref, sel_ref, fmax_ref, fmin_ref, blo_ref, tcnt_ref, tsel_ref, qic_ref, qbm_ref, m_ref, acc_ref,
                *, tq, ck, rb, topk):
    i = pl.program_id(1)
    nrows = (i + 1) * tq
    nfull = lax.div(nrows, ck)
    nblk = lax.div(nrows, rb)

    rows = lax.broadcasted_iota(I32, (2 * HEAD_DIM, tq), 0)
    for j in range(DSA_Q_HEADS):
        g = j // DSA_GROUP
        q = qbt_ref[j * HEAD_DIM:(j + 1) * HEAD_DIM, :]
        q2 = jnp.concatenate([q, q], axis=0)
        own = (rows >= g * HEAD_DIM) & (rows < (g + 1) * HEAD_DIM)
        qbm_ref[:, j * tq:(j + 1) * tq] = jnp.where(own, q2, jnp.zeros_like(q2))
    for h in range(IDX_HEADS):
        qic_ref[:, h * tq:(h + 1) * tq] = qit_ref[h * HEAD_DIM:(h + 1) * HEAD_DIM, :]

    def index_chunk(off, size):
        sc = _dot(ki_ref[pl.ds(off, size), :], qic_ref[...])
        acc = None
        for h in range(IDX_HEADS):
            t = jnp.maximum(sc[:, h * tq:(h + 1) * tq], 0.0) * wit_ref[h:h + 1, :]
            acc = t if acc is None else acc + t
        kpos = off + lax.broadcasted_iota(I32, (size, tq), 0)
        qpos = i * tq + lax.broadcasted_iota(I32, (size, tq), 1)
        acc = jnp.where(kpos <= qpos, acc, -jnp.inf)
        bits = lax.bitcast_convert_type(acc, I32)
        key = jnp.where(bits < 0, INT_MIN - bits, bits)
        hi_ref[pl.ds(off, size), :] = lax.shift_right_arithmetic(key, 16).astype(I16)
        lo_ref[pl.ds(off, size), :] = (key ^ 0x8000).astype(I16)

    def for_each_chunk(fn):
        def pair(t, carry):
            fn(pl.multiple_of(2 * t * ck, ck), ck)
            fn(pl.multiple_of((2 * t + 1) * ck, ck), ck)
            return carry
        lax.fori_loop(0, lax.div(nfull, 2), pair, 0)

        @pl.when(lax.rem(nfull, 2) == 1)
        def _():
            fn(pl.multiple_of((nfull - 1) * ck, ck), ck)

        @pl.when(nfull * ck < nrows)
        def _():
            fn(pl.multiple_of(nfull * ck, ck), ck // 2)

    for_each_chunk(index_chunk)


    def count(refs, pred, nb=nblk, emit=None):
        def body(r, accs):
            base = pl.multiple_of(r * rb, rb)
            blks = [ref[pl.ds(base, rb), :] for ref in refs]
            if emit is not None:
                emit(blks, base)
            accs = list(accs)
            for u in range(rb // 16):
                hit = pred([blk[u * 16:(u + 1) * 16, :] for blk in blks], base + u * 16)
                accs[u % N_ACC] = accs[u % N_ACC] + jnp.where(hit, jnp.int16(1), jnp.int16(0))
            return tuple(accs)
        accs = lax.fori_loop(0, nb, body, tuple(jnp.zeros((16, tq), I16) for _ in range(N_ACC)))
        tot = accs[0].astype(I32)
        for a in accs[1:]:
            tot = tot + a.astype(I32)
        return jnp.sum(tot, axis=0, keepdims=True)

    def rows16(v):
        return jnp.broadcast_to(v, (16, tq)).astype(I16)

    def bisect(ref, want):
        def step(it, u):
            uc = u | lax.shift_left(jnp.int32(1), 15 - it)
            cand = rows16(uc - 32768)
            cnt = count([ref], lambda b, p0: b[0] >= cand)
            return jnp.where(cnt >= want, uc, u)
        return lax.fori_loop(0, 16, step, jnp.zeros((1, tq), I32)) - 32768

    a32 = jnp.maximum(bisect(hi_ref, topk), HI_NEG_INF)
    a16 = rows16(a32)
    a_blk = jnp.broadcast_to(a32, (rb, tq)).astype(I16)

    fmax_ref[...] = jnp.full(fmax_ref.shape, I16_MIN, I16)
    fmin_ref[...] = jnp.full(fmin_ref.shape, I16_MIN, I16)

    def fold_block(r, carry):
        accs, most = carry
        base = pl.multiple_of(r * rb, rb)
        hi, lo = hi_ref[pl.ds(base, rb), :], lo_ref[pl.ds(base, rb), :]
        top = jnp.full((16, tq), I16_MIN, I16)
        low = jnp.full((16, tq), I16_MAX, I16)
        members = jnp.zeros((16, tq), I16)
        accs = list(accs)
        for u in range(rb // 16):
            h, l = hi[u * 16:(u + 1) * 16, :], lo[u * 16:(u + 1) * 16, :]
            member = h == a16
            up, down = jnp.where(member, l, jnp.int16(I16_MIN)), jnp.where(member, l, jnp.int16(I16_MAX))
            top = jnp.where(up > top, up, top)
            low = jnp.where(down < low, down, low)
            members = members + jnp.where(member, jnp.int16(1), jnp.int16(0))
            accs[u % N_ACC] = accs[u % N_ACC] + jnp.where(h > a16, jnp.int16(1), jnp.int16(0))
        row = pl.multiple_of(r * 16, 16)
        fmax_ref[pl.ds(row, 16), :] = top
        fmin_ref[pl.ds(row, 16), :] = jnp.where(members >= 2, low, jnp.int16(I16_MIN))
        return tuple(accs), jnp.where(members > most, members, most)

    accs, most = lax.fori_loop(0, nblk, fold_block, (tuple(jnp.zeros((16, tq), I16) for _ in range(N_ACC)),
                                                     jnp.zeros((16, tq), I16)))
    n_hi = accs[0].astype(I32)
    for a in accs[1:]:
        n_hi = n_hi + a.astype(I32)
    n_hi = jnp.sum(n_hi, axis=0, keepdims=True)
    want_lo = topk - n_hi

    def folded_step(it, u):
        uc = u | lax.shift_left(jnp.int32(1), 15 - it)
        cand = rows16(uc - 32768)
        accs = [jnp.zeros((16, tq), I16) for _ in range(N_ACC)]
        for r in range(fmax_ref.shape[0] // 16):
            for k, ref in enumerate((fmax_ref, fmin_ref)):
                hit = ref[r * 16:(r + 1) * 16, :] >= cand
                accs[(2 * r + k) % N_ACC] = accs[(2 * r + k) % N_ACC] + jnp.where(hit, jnp.int16(1), jnp.int16(0))
        tot = accs[0].astype(I32)
        for a in accs[1:]:
            tot = tot + a.astype(I32)
        return jnp.where(jnp.sum(tot, axis=0, keepdims=True) >= want_lo, uc, u)

    blo_ref[...] = jnp.broadcast_to(lax.fori_loop(0, 16, folded_step, jnp.zeros((1, tq), I32)) - 32768, blo_ref.shape)

    @pl.when(jnp.max(most.astype(I32)) > FOLD_MAX_MEMBERS)
    def _():
        def bucket(r, carry):
            base = pl.multiple_of(r * rb, rb)
            sel_ref[pl.ds(base, rb), :] = jnp.where(hi_ref[pl.ds(base, rb), :] == a_blk, lo_ref[pl.ds(base, rb), :],
                                                    jnp.int16(I16_MIN))
            return carry
        lax.fori_loop(0, nblk, bucket, 0)
        blo_ref[...] = jnp.broadcast_to(bisect(sel_ref, want_lo), blo_ref.shape)

    b32 = blo_ref[0:1, :]
    b32 = jnp.where(a32 == HI_NEG_INF, jnp.maximum(b32, I16_MIN + 1), b32)
    b16 = rows16(b32)
    b_blk = jnp.broadcast_to(b32, (rb, tq)).astype(I16)

    def emit_selection(blks, base):
        sel = (blks[0] > a_blk) | ((blks[0] == a_blk) & (blks[1] >= b_blk))
        sel_ref[pl.ds(base, rb), :] = jnp.where(sel, jnp.int16(BF16_ONE_BITS), jnp.int16(0))

    n_ge = count([hi_ref, lo_ref], lambda b, p0: (b[0] > a16) | ((b[0] == a16) & (b[1] >= b16)),
                 emit=emit_selection)

    need = n_ge > topk

    @pl.when(jnp.max(jnp.where(need, 1, 0)) > 0)
    def _():
        nb_max = tcnt_ref.shape[0]
        none = jnp.int32(2 ** 30)

        def tied(hi, lo, a, b):
            return (hi == a) & (lo == b)

        tcnt_ref[...] = jnp.zeros(tcnt_ref.shape, I32)

        def tie_block(r, carry):
            base = pl.multiple_of(r * rb, rb)
            hi, lo = hi_ref[pl.ds(base, rb), :], lo_ref[pl.ds(base, rb), :]
            acc = jnp.zeros((16, tq), I16)
            for u in range(rb // 16):
                acc = acc + jnp.where(tied(hi[u * 16:(u + 1) * 16, :], lo[u * 16:(u + 1) * 16, :], a16, b16),
                                      jnp.int16(1), jnp.int16(0))
            tcnt_ref[r] = jnp.broadcast_to(jnp.sum(acc.astype(I32), axis=0, keepdims=True), (8, tq))
            return carry

        lax.fori_loop(0, nblk, tie_block, 0)

        per_block = [tcnt_ref[r][0:1, :] for r in range(nb_max)]
        n_tied = per_block[0]
        for c in per_block[1:]:
            n_tied = n_tied + c
        want = jnp.where(need, topk - (n_ge - n_tied), none)
        last = jnp.full((1, tq), nb_max, I32)
        before = jnp.zeros((1, tq), I32)
        prefix = jnp.zeros((1, tq), I32)
        for r, c in enumerate(per_block):
            reached = (last == nb_max) & (prefix + c >= want)
            last = jnp.where(reached, r, last)
            before = jnp.where(reached, prefix, before)
            prefix = prefix + c
        want_here = want - before

        tsel_ref[...] = jnp.zeros(tsel_ref.shape, I16)

        def pick_block(r, carry):
            base = pl.multiple_of(r * rb, rb)
            here = jnp.broadcast_to(jnp.where(last == r, 1, 0), (rb, tq)).astype(I16) != 0
            hit = tied(hi_ref[pl.ds(base, rb), :], lo_ref[pl.ds(base, rb), :], a_blk, b_blk) & here
            tsel_ref[...] = tsel_ref[...] + jnp.where(hit, jnp.int16(1), jnp.int16(0))
            return carry

        lax.fori_loop(0, nblk, pick_block, 0)

        in_block = lax.broadcasted_iota(I32, (rb, tq), 0).astype(I16)
        block_bits = (rb - 1).bit_length()

        def row_step(it, e):
            ec = e | lax.shift_left(jnp.int32(1), block_bits - 1 - it)
            hit = (tsel_ref[...] != 0) & (in_block < jnp.broadcast_to(ec, (rb, tq)).astype(I16))
            ones = jnp.where(hit, jnp.int16(1), jnp.int16(0))
            acc = ones[0:16, :]
            for u in range(1, rb // 16):
                acc = acc + ones[u * 16:(u + 1) * 16, :]
            below = jnp.sum(acc.astype(I32), axis=0, keepdims=True)
            return jnp.where(below < want_here, ec, e)

        e_last = lax.fori_loop(0, block_bits, row_step, jnp.zeros((1, tq), I32))
        x = jnp.where(last < nb_max, jnp.minimum(last * rb + e_last, I16_MAX), I16_MAX)

        def demote(r, carry):
            base = pl.multiple_of(r * rb, rb)
            pos = (base + lax.broadcasted_iota(I32, (rb, tq), 0)).astype(I16)
            drop = (tied(hi_ref[pl.ds(base, rb), :], lo_ref[pl.ds(base, rb), :], a_blk, b_blk)
                    & (pos > jnp.broadcast_to(x, (rb, tq)).astype(I16)))
            sel_ref[pl.ds(base, rb), :] = jnp.where(drop, jnp.int16(0), sel_ref[pl.ds(base, rb), :])
            return carry

        lax.fori_loop(0, nblk, demote, 0)

    shift = shift_ref[0]

    @pl.when(shift < SAFE_SHIFT_LOG2)
    def _():
        acc_ref[...] = jnp.zeros(acc_ref.shape, F32)

        def attend_chunk(off, size):
            kch = kb_ref[pl.ds(off, size), :]
            msk = lax.bitcast_convert_type(sel_ref[pl.ds(off, size), :], BF16)
            scs = [_dot(kch, qbm_ref[:, g * DSA_GROUP * tq:(g + 1) * DSA_GROUP * tq]) for g in range(DSA_KV_HEADS)]
            ps = [jnp.exp2(sc - shift).astype(BF16) * _tile_lanes(msk, DSA_GROUP) for sc in scs]
            for g, p in enumerate(ps):
                acc_ref[g] += _dot(vta_ref[g * V_AUG:(g + 1) * V_AUG, pl.ds(off, size)], p)

        for_each_chunk(attend_chunk)

    @pl.when(shift >= SAFE_SHIFT_LOG2)
    def _():
        m_ref[...] = jnp.full(m_ref.shape, NEG_BIG, F32)
        acc_ref[...] = jnp.zeros(acc_ref.shape, F32)

        def attend_block(r, carry):
            off = pl.multiple_of(r * rb, rb)
            kch = kb_ref[pl.ds(off, rb), :]
            sel = sel_ref[pl.ds(off, rb), :].astype(I32) != 0
            for j in range(DSA_Q_HEADS):
                g = j // DSA_GROUP
                cols = slice((j % DSA_GROUP) * tq, (j % DSA_GROUP + 1) * tq)
                sc = jnp.where(sel, _dot(kch, qbm_ref[:, j * tq:(j + 1) * tq]), NEG_BIG)
                m_prev = m_ref[j]
                m_new = jnp.maximum(m_prev, jnp.max(sc, axis=0, keepdims=True))
                p = jnp.where(sel, jnp.exp2(sc - m_new), 0.0).astype(BF16)
                acc_ref[g, :, cols] = jnp.exp2(m_prev - m_new) * acc_ref[g, :, cols] + _dot(
                    vta_ref[g * V_AUG:(g + 1) * V_AUG, pl.ds(off, rb)], p)
                m_ref[j] = m_new
            return carry

        lax.fori_loop(0, nblk, attend_block, 0)

    for pair in range(DSA_Q_HEADS // 2):
        halves = []
        for j in (2 * pair, 2 * pair + 1):
            a = acc_ref[j // DSA_GROUP, :, (j % DSA_GROUP) * tq:(j % DSA_GROUP + 1) * tq]
            halves.append(a[:HEAD_DIM] / a[HEAD_DIM:HEAD_DIM + 1])
        o_ref[:, pair * LANES:(pair + 1) * LANES] = jnp.concatenate(halves, axis=0).T.astype(BF16)


def _dsa(shift, qit, wit, ki, qbt, kb, vta):
    b, _, s = qbt.shape
    tq = min(256, s)
    ck = min(512, s)
    rb = 256
    topk = min(DSA_TOPK_MAX, s // 4)
    qt = lambda w: pl.BlockSpec((None, w, tq), lambda i, j: (i, 0, j))
    seq = lambda w: pl.BlockSpec((None, s, w), lambda i, j: (i, 0, 0))
    return pl.pallas_call(
        functools.partial(_dsa_kernel, tq=tq, ck=ck, rb=rb, topk=topk),
        grid=(b, s // tq),
        in_specs=[pl.BlockSpec(memory_space=pltpu.SMEM), qt(D_QI), qt(IDX_HEADS), seq(HEAD_DIM), qt(D_QB), seq(D_KVB),
                  pl.BlockSpec((None, DSA_KV_HEADS * V_AUG, s), lambda i, j: (i, 0, 0))],
        out_specs=pl.BlockSpec((None, tq, D_QB), lambda i, j: (i, j, 0)),
        out_shape=jax.ShapeDtypeStruct((b, s, D_QB), BF16),
        scratch_shapes=[
            pltpu.VMEM((s, tq), I16),
            pltpu.VMEM((s, tq), I16),
            pltpu.VMEM((s, tq), I16),
            pltpu.VMEM((s // rb * 16, tq), I16),
            pltpu.VMEM((s // rb * 16, tq), I16),
            pltpu.VMEM((8, tq), I32),
            pltpu.VMEM((s // rb, 8, tq), I32),
            pltpu.VMEM((rb, tq), I16),
            pltpu.VMEM((HEAD_DIM, IDX_HEADS * tq), BF16),
            pltpu.VMEM((2 * HEAD_DIM, DSA_Q_HEADS * tq), BF16),
            pltpu.VMEM((DSA_Q_HEADS, 1, tq), F32),
            pltpu.VMEM((DSA_KV_HEADS, V_AUG, DSA_GROUP * tq), F32),
        ],
        compiler_params=pltpu.CompilerParams(dimension_semantics=("arbitrary", "arbitrary"),
                                             vmem_limit_bytes=VMEM_LIMIT),
        name="dsa",
    )(shift, qit, wit, ki, qbt, kb, vta)


def _merge_kernel(x_ref, g_ref, wg_ref, o0_ref, l0_ref, o1_ref, l1_ref, o2_ref, l2_ref, ob_ref, qc_ref,
                  km_ref, vm_ref, wa_ref, wb_ref, wc_ref, wo_ref, out_ref, *stage_refs):
    tm, d = x_ref.shape

    def token_major(ref, stage_ref):
        dil = ref.shape[1] // LANES
        if dil == 1:
            return ref[...]
        for r in range(dil):
            stage_ref[pl.ds(r, tm // dil, stride=dil), :] = ref[:, r * LANES:(r + 1) * LANES]
        return stage_ref[...]

    lses = (l0_ref[...], token_major(l1_ref, stage_refs[0]), token_major(l2_ref, stage_refs[1]))
    outs = (o0_ref[...], token_major(o1_ref, stage_refs[2]), token_major(o2_ref, stage_refs[3]))

    def rows_part(rows):
        n = rows.stop - rows.start
        x = x_ref[rows, :]
        h = _rms_rows(x, g_ref[...]).astype(BF16)

        low = _low_half((n, LANES))
        scores = []
        for mcol in range(MEM_HEADS // 2):
            q = qc_ref[rows, mcol * LANES:(mcol + 1) * LANES]
            zero = jnp.zeros_like(q)
            for qh in (jnp.where(low, q, zero), jnp.where(low, zero, q)):
                scores.append(_dot_nt(qh, km_ref[:, mcol * LANES:(mcol + 1) * LANES]))
        logits = [_dot(h, wg_ref[:, k * d:(k + 1) * d]) for k in range(3)]

        def gate(k):
            return 1.0 / (1.0 + jnp.exp(-logits[k]))

        l0, l1, l2 = (t[rows, :] for t in lses)
        o0, o1, o2 = (t[rows, :] for t in outs)
        mx = jnp.maximum(jnp.maximum(l0, l1), l2)
        e0, e1, e2 = jnp.exp(l0 - mx), jnp.exp(l1 - mx), jnp.exp(l2 - mx)
        oa = (e0 * o0 + e1 * o1 + e2 * o2) / (e0 + e1 + e2)
        merged = gate(0) * _dot(oa.astype(BF16), wa_ref[...])

        merged = merged + gate(1) * _dot(ob_ref[rows, :], wb_ref[...])

        probs = []
        for s in scores:
            e = jnp.exp(s - jnp.max(s, axis=-1, keepdims=True))
            probs.append((e / jnp.sum(e, axis=-1, keepdims=True)).astype(BF16))
        heads = [_dot(p, vm_ref[:, (k // 2) * LANES:(k // 2 + 1) * LANES]) for k, p in enumerate(probs)]
        oc = jnp.concatenate([jnp.where(low, heads[2 * mcol], heads[2 * mcol + 1]) for mcol in range(MEM_HEADS // 2)],
                             axis=1)
        merged = merged + gate(2) * _dot(oc.astype(BF16), wc_ref[...])

        out_ref[rows, :] = x + _dot(merged.astype(BF16), wo_ref[...])

    parts = MERGE_ROW_PARTS if tm % (8 * MERGE_ROW_PARTS) == 0 else 1
    for part in range(parts):
        rows_part(slice(part * (tm // parts), (part + 1) * (tm // parts)))


def _merge(x, g_mix, w_gate, dil, ob, qc, km, vm, w_a, w_b, w_c, w_o, tm):
    b, s, d = x.shape
    mlen = km.shape[1]
    tok = lambda w: pl.BlockSpec((None, tm, w), lambda i, j: (i, j, 0))
    full = lambda *shape: pl.BlockSpec(shape, lambda i, j: (0,) * len(shape))
    memb = pl.BlockSpec((None, mlen, D_QC), lambda i, j: (i, 0, 0))
    dil_args = [t for pair in dil for t in pair]
    dil_specs = [pl.BlockSpec((None, tm // dl, dl * LANES), lambda i, j: (i, j, 0))
                 for _, dl in DIL_GROUPS for _ in range(2)]
    return pl.pallas_call(
        _merge_kernel,
        grid=(b, s // tm),
        in_specs=[tok(d), full(1, d), full(d, 3 * d)] + dil_specs + [tok(D_QB), tok(D_QC), memb, memb,
                  full(LANES, d), full(D_QB, d), full(D_QC, d), full(d, d)],
        out_specs=tok(d),
        out_shape=jax.ShapeDtypeStruct((b, s, d), F32),
        scratch_shapes=[pltpu.VMEM((tm, LANES), F32)] * 4,
        compiler_params=pltpu.CompilerParams(dimension_semantics=("arbitrary", "arbitrary"),
                                             vmem_limit_bytes=VMEM_LIMIT),
        name="merge",
    )(x, g_mix, w_gate, *dil_args, ob, qc, km, vm, w_a, w_b, w_c, w_o)


def _mlp_kernel(x_ref, g_ref, w1_ref, w2_ref, out_ref, *, fchunk):
    x = x_ref[...]
    h = _rms_rows(x, g_ref[...]).astype(BF16)
    acc = x
    for c in range(w1_ref.shape[1] // fchunk):
        u = jnp.maximum(_dot(h, w1_ref[:, c * fchunk:(c + 1) * fchunk]), 0.0)
        acc = acc + _dot((u * u).astype(BF16), w2_ref[c * fchunk:(c + 1) * fchunk, :])
    out_ref[...] = acc


def _mlp(x, g_mlp, w_1, w_2, tm):
    b, s, d = x.shape
    f = w_1.shape[1]
    tok = pl.BlockSpec((None, tm, d), lambda i, j: (i, j, 0))
    full = lambda *shape: pl.BlockSpec(shape, lambda i, j: (0,) * len(shape), pipeline_mode=pl.Buffered(1))
    return pl.pallas_call(
        functools.partial(_mlp_kernel, fchunk=min(1024, f)),
        grid=(b, s // tm),
        in_specs=[tok, full(1, d), full(d, f), full(f, d)],
        out_specs=tok,
        out_shape=jax.ShapeDtypeStruct((b, s, d), F32),
        compiler_params=pltpu.CompilerParams(dimension_semantics=("arbitrary", "arbitrary"),
                                             vmem_limit_bytes=VMEM_LIMIT),
        name="mlp",
    )(x, g_mlp, w_1, w_2)


def _rotary_tables(positions):
    inv = jnp.power(jnp.float32(ROPE_THETA), -jnp.arange(ROT_HALF, dtype=F32) / ROT_HALF)
    ang = positions.astype(F32)[:, None, :] * inv[None, :, None]
    return jnp.cos(ang), jnp.sin(ang)


def _block_diag_mean(width):
    r = jnp.arange(width) // HEAD_DIM
    return jnp.where(r[:, None] == r[None, :], 1.0 / HEAD_DIM, 0.0).astype(BF16)


def _layer(x, mem, tables, g_mix, g_mem, w_in, g_qa, g_ka, g_qb, g_kb, g_qc, g_kc,
           w_mem_kv, w_a, w_b, w_c, w_o, g_mlp, w_1, w_2):
    b, s, d = x.shape
    tm = min(TOKEN_TILE, s)
    cos_t, sin_t = tables
    bd = _block_diag_mean(D_QA)
    scale = HEAD_DIM ** -0.5

    offs, acc = [], 0
    for w in (D_QA, D_QA, D_QA, D_QB, D_KVB, D_KVB, D_QI, HEAD_DIM, IDX_HEADS, D_QC):
        offs.append((acc, acc + w))
        acc += w
    seg = lambda k: w_in[:, offs[k][0]:offs[k][1]]
    w_std = jnp.concatenate([seg(0), seg(1), seg(2), seg(4), seg(7), seg(7), seg(9)], axis=1).astype(BF16)
    w_t = jnp.concatenate([seg(3), seg(6), seg(5), seg(8), jnp.zeros((d, _R_END - _R_WI - IDX_HEADS), w_in.dtype)],
                          axis=1).T.astype(BF16)
    w_gate = w_in[:, acc:].astype(BF16)

    tile6 = lambda g: jnp.tile(g, D_QA // HEAD_DIM)
    hg = jnp.stack([tile6(g_qa) * scale, tile6(g_ka), tile6(g_kb), tile6(g_qc) * scale,
                    jnp.zeros(D_QA), jnp.zeros(D_QA), jnp.zeros(D_QA), jnp.zeros(D_QA)]).astype(F32)
    gqt = jnp.broadcast_to((g_qb * (scale * LOG2E))[:, None], (HEAD_DIM, LANES)).astype(F32)
    shift = (HEAD_DIM * scale * LOG2E * SHIFT_MARGIN) * jnp.max(jnp.abs(g_qb)) * jnp.max(jnp.abs(g_kb))
    shift = jnp.reshape(shift, (1,)).astype(F32)

    km, vm = _memkv(mem, g_mem[None, :], w_mem_kv.astype(BF16), bd[:D_QC, :D_QC],
                    jnp.tile(g_kc, MEM_HEADS)[None, :])
    (q0, q1, q2, k0, k1, k2, v0, v1, v2, kb, ki, qc, qbt, qit, vta, wit) = _inproj(
        x, g_mix[None, :], w_std, w_t, cos_t, sin_t, bd, hg, gqt, tm)
    dil = [_dilated(q, k, v, g) for g, (q, k, v) in enumerate(((q0, k0, v0), (q1, k1, v1), (q2, k2, v2)))]
    ob = _dsa(shift, qit, wit, ki, qbt, kb, vta)
    x = _merge(x, g_mix[None, :], w_gate, dil, ob, qc, km, vm, w_a.astype(BF16), w_b.astype(BF16),
               w_c.astype(BF16), w_o.astype(BF16), tm)
    return _mlp(x, g_mlp[None, :], w_1.astype(BF16), w_2.astype(BF16), tm)


def kernel(x, mem, positions, g_mix, g_mem, w_in, g_qa, g_ka, g_qb, g_kb, g_qc, g_kc, w_mem_kv, w_a, w_b, w_c, w_o, g_mlp, w_1, w_2):
    tables = _rotary_tables(positions)
    for i in range(g_mix.shape[0]):
        x = _layer(x, mem, tables, g_mix[i], g_mem[i], w_in[i], g_qa[i], g_ka[i], g_qb[i], g_kb[i], g_qc[i],
                   g_kc[i], w_mem_kv[i], w_a[i], w_b[i], w_c[i], w_o[i], g_mlp[i], w_1[i], w_2[i])
    return x
```

```python
import functools
import math

import jax
import jax.numpy as jnp
from jax import lax
from jax.experimental import pallas as pl
from jax.experimental.pallas import tpu as pltpu

F32 = jnp.float32
BF16 = jnp.bfloat16
I32 = jnp.int32
I16 = jnp.int16

LANES = 128
HEAD_DIM = 64
ROT_HALF = 8
ROPE_THETA = 500000.0
EPS = 1e-6
DIL_GROUPS = ((128, 1), (512, 4), (2048, 16))
DIL_SPAN = 128
N_DIL_HEADS = 6
DSA_Q_HEADS = 6
DSA_KV_HEADS = 2
DSA_GROUP = DSA_Q_HEADS // DSA_KV_HEADS
DSA_TOPK_MAX = 256
IDX_HEADS = 8
MEM_HEADS = 4
D_QA = N_DIL_HEADS * HEAD_DIM
D_QB = DSA_Q_HEADS * HEAD_DIM
D_KVB = DSA_KV_HEADS * HEAD_DIM
D_QI = IDX_HEADS * HEAD_DIM
D_QC = MEM_HEADS * HEAD_DIM
V_AUG = 80
LOG2E = math.log2(math.e)
NEG_BIG = -1e30
INT_MIN = -2147483648
I16_MIN = -32768
I16_MAX = 32767
FOLD_MAX_MEMBERS = 2
BF16_ONE_BITS = 0x3F80
HI_NEG_INF = -32640
SAFE_SHIFT_LOG2 = 55.0
SHIFT_MARGIN = 1.02
TOKEN_TILE = 1024
DIL_TILE = 1024
N_ACC = 4
MERGE_ROW_PARTS = 2
VMEM_LIMIT = 56 * 1024 * 1024

_CONTRACT_LAST = (((1,), (1,)), ((), ()))


def _dot(a, b):
    return jnp.dot(a, b, preferred_element_type=F32)


def _dot_nt(a, b):
    return lax.dot_general(a, b, _CONTRACT_LAST, preferred_element_type=F32)


def _tile_lanes(a, reps):
    return a if reps == 1 else jnp.concatenate([a] * reps, axis=1)


def _low_half(shape):
    return (lax.broadcasted_iota(I32, shape, 1) % LANES) < HEAD_DIM


def _rms_rows(x, g):
    ms = jnp.mean(x * x, axis=-1, keepdims=True)
    return x * lax.rsqrt(ms + EPS) * g


def _norm_heads(p, bd, gain):
    sq = p * p
    hi = sq.astype(BF16)
    lo = (sq - hi.astype(F32)).astype(BF16)
    pair = bd[:LANES, :LANES]
    cols = [_dot(hi[:, c:c + LANES], pair) + _dot(lo[:, c:c + LANES], pair) for c in range(0, p.shape[1], LANES)]
    ms = cols[0] if len(cols) == 1 else jnp.concatenate(cols, axis=1)
    return p * lax.rsqrt(ms + EPS) * gain


def _rotary(y, cos, sin_lo, sin_hi):
    w = y.shape[1]
    reps = w // LANES
    c = _tile_lanes(cos, reps)
    a = _tile_lanes(sin_lo, reps)
    b = _tile_lanes(sin_hi, reps)
    return y * c + pltpu.roll(y, w - ROT_HALF, 1) * a + pltpu.roll(y, ROT_HALF, 1) * b


def _rotary_t(blk, cos_t, sin_t):
    x1, x2 = blk[:ROT_HALF], blk[ROT_HALF:2 * ROT_HALF]
    return jnp.concatenate([x1 * cos_t - x2 * sin_t, x2 * cos_t + x1 * sin_t, blk[2 * ROT_HALF:]], axis=0)


def _memkv_kernel(mem_ref, g_ref, w_ref, bd_ref, gk_ref, k_ref, v_ref):
    h = _rms_rows(mem_ref[...], g_ref[...]).astype(BF16)
    kv = _dot(h, w_ref[...])
    k_ref[...] = _norm_heads(kv[:, :D_QC], bd_ref[...], gk_ref[...]).astype(BF16)
    v_ref[...] = kv[:, D_QC:].astype(BF16)


def _memkv(mem, g_mem, w_mem_kv, bd, gk):
    b, m, d = mem.shape
    full = lambda *shape: pl.BlockSpec(shape, lambda i: (0,) * len(shape))
    return pl.pallas_call(
        _memkv_kernel,
        grid=(b,),
        in_specs=[pl.BlockSpec((None, m, d), lambda i: (i, 0, 0)), full(1, d), full(d, 2 * D_QC),
                  full(D_QC, D_QC), full(1, D_QC)],
        out_specs=[pl.BlockSpec((None, m, D_QC), lambda i: (i, 0, 0))] * 2,
        out_shape=[jax.ShapeDtypeStruct((b, m, D_QC), BF16)] * 2,
        name="memkv",
    )(mem, g_mem, w_mem_kv, bd, gk)


_C_QA, _C_KA, _C_VA, _C_KB, _C_KI, _C_QC, _C_END = 0, 384, 768, 1152, 1280, 1408, 1664
_R_QB, _R_QI, _R_VB, _R_WI, _R_END = 0, 384, 896, 1024, 1040


def _inproj_kernel(x_ref, g_ref, w_ref, wt_ref, cost_ref, sint_ref, bd_ref, hg_ref, gqt_ref,
                   q0_ref, q1_ref, q2_ref, k0_ref, k1_ref, k2_ref, v0_ref, v1_ref, v2_ref,
                   kb_ref, ki_ref, qc_ref, qbt_ref, qit_ref, vta_ref, wit_ref, dil_ref):
    tm = x_ref.shape[0]
    h = _rms_rows(x_ref[...], g_ref[...]).astype(BF16)
    bd = bd_ref[...]

    cos_t, sin_t = cost_ref[...], sint_ref[...]
    rest = HEAD_DIM - 2 * ROT_HALF
    one_r, zero_r, zero_h = jnp.ones((rest, tm), F32), jnp.zeros((rest, tm), F32), jnp.zeros((ROT_HALF, tm), F32)
    cos = jnp.concatenate([cos_t, cos_t, one_r] * (LANES // HEAD_DIM), axis=0).T
    slo = jnp.concatenate([-sin_t, zero_h, zero_r] * (LANES // HEAD_DIM), axis=0).T
    shi = jnp.concatenate([zero_h, sin_t, zero_r] * (LANES // HEAD_DIM), axis=0).T

    def proj(a, b):
        return _dot(h, w_ref[:, a:b])

    def norm_rot(a, b, gain_row):
        w = b - a
        y = _norm_heads(proj(a, b), bd[:w, :w], hg_ref[gain_row:gain_row + 1, :w])
        return _rotary(y, cos, slo, shi)

    def store_dilated(y, out_refs):
        for g, ((_, dil), out_ref) in enumerate(zip(DIL_GROUPS, out_refs)):
            yg = y[:, g * LANES:(g + 1) * LANES]
            if dil == 1:
                out_ref[...] = yg.astype(BF16)
            else:
                dil_ref[...] = yg
                for r in range(dil):
                    out_ref[:, r * LANES:(r + 1) * LANES] = dil_ref[pl.ds(r, tm // dil, stride=dil), :].astype(BF16)

    store_dilated(norm_rot(_C_QA, _C_KA, 0), (q0_ref, q1_ref, q2_ref))
    store_dilated(norm_rot(_C_KA, _C_VA, 1), (k0_ref, k1_ref, k2_ref))
    store_dilated(proj(_C_VA, _C_KB), (v0_ref, v1_ref, v2_ref))
    kb_ref[...] = norm_rot(_C_KB, _C_KI, 2).astype(BF16)
    ki_ref[...] = _rotary(proj(_C_KI, _C_QC), cos, slo, shi)[:, :HEAD_DIM].astype(BF16)
    qc_ref[...] = _norm_heads(proj(_C_QC, _C_END), bd[:D_QC, :D_QC], hg_ref[3:4, :D_QC]).astype(BF16)

    pt = _dot_nt(wt_ref[...], h)
    gq =_tile_lanes(gqt_ref[...], tm // LANES)
    for hd in range(DSA_Q_HEADS):
        blk = pt[_R_QB + hd * HEAD_DIM:_R_QB + (hd + 1) * HEAD_DIM]
        ms = jnp.mean(blk * blk, axis=0, keepdims=True)
        qbt_ref[hd * HEAD_DIM:(hd + 1) * HEAD_DIM, :] = _rotary_t(blk * lax.rsqrt(ms + EPS) * gq, cos_t, sin_t).astype(BF16)
    for hd in range(IDX_HEADS):
        blk = pt[_R_QI + hd * HEAD_DIM:_R_QI + (hd + 1) * HEAD_DIM]
        qit_ref[hd * HEAD_DIM:(hd + 1) * HEAD_DIM, :] = _rotary_t(blk, cos_t, sin_t).astype(BF16)
    vt = pt[_R_VB:_R_WI].astype(BF16)
    pad = jnp.where(lax.broadcasted_iota(I32, (V_AUG - HEAD_DIM, tm), 0) == 0, 1.0, 0.0).astype(BF16)
    vta_ref[...] = jnp.concatenate([vt[:HEAD_DIM], pad, vt[HEAD_DIM:], pad], axis=0)
    wit_ref[...] = pt[_R_WI:_R_WI + IDX_HEADS] * ((IDX_HEADS ** -0.5) * (HEAD_DIM ** -0.5))


def _inproj(x, g_mix, w_std, w_t, cos_t, sin_t, bd, hg, gqt, tm):
    b, s, d = x.shape
    tok = lambda w: pl.BlockSpec((None, tm, w), lambda i, j: (i, j, 0))
    tok_t = lambda r: pl.BlockSpec((None, r, tm), lambda i, j: (i, 0, j))
    full = lambda *shape: pl.BlockSpec(shape, lambda i, j: (0,) * len(shape))
    dils = [dl for _, dl in DIL_GROUPS] * 3
    std = ((D_KVB, BF16), (HEAD_DIM, BF16), (D_QC, BF16))
    tr = ((D_QB, BF16), (D_QI, BF16), (DSA_KV_HEADS * V_AUG, BF16), (IDX_HEADS, F32))
    return pl.pallas_call(
        _inproj_kernel,
        grid=(b, s // tm),
        in_specs=[tok(d), full(1, d), full(d, _C_END), full(_R_END, d),
                  tok_t(ROT_HALF), tok_t(ROT_HALF), full(D_QA, D_QA), full(8, D_QA), full(HEAD_DIM, LANES)],
        out_specs=[pl.BlockSpec((None, tm // dl, dl * LANES), lambda i, j: (i, j, 0)) for dl in dils]
        + [tok(w) for w, _ in std] + [tok_t(r) for r, _ in tr],
        out_shape=[jax.ShapeDtypeStruct((b, s // dl, dl * LANES), BF16) for dl in dils]
        + [jax.ShapeDtypeStruct((b, s, w), t) for w, t in std]
        + [jax.ShapeDtypeStruct((b, r, s), t) for r, t in tr],
        scratch_shapes=[pltpu.VMEM((tm, LANES), F32)],
        compiler_params=pltpu.CompilerParams(dimension_semantics=("arbitrary", "arbitrary"),
                                             vmem_limit_bytes=VMEM_LIMIT),
        name="inproj",
    )(x, g_mix, w_std, w_t, cos_t, sin_t, bd, hg, gqt)


def _dilated_kernel(q_ref, k_ref, v_ref, kp_ref, vp_ref, o_ref, l_ref, *, nsub):
    n = pl.program_id(2)
    sp = DIL_SPAN
    low = _low_half((sp, LANES))
    rq = lax.broadcasted_iota(I32, (2 * sp, 2 * sp), 0) % sp
    kj = lax.broadcasted_iota(I32, (2 * sp, 2 * sp), 1)
    dist = sp + rq - kj
    band = (dist >= 0) & (dist <= sp)
    kmin = jnp.where(n > 0, 0, sp)
    scores = []
    for i in range(nsub):
        q = q_ref[i * sp:(i + 1) * sp, :]
        kprev = kp_ref[...] if i == 0 else k_ref[(i - 1) * sp:i * sp, :]
        mask = band & (kj >= kmin) if i == 0 else band
        kk = jnp.concatenate([kprev, k_ref[i * sp:(i + 1) * sp, :]], axis=0)
        zero = jnp.zeros_like(q)
        qs = jnp.concatenate([jnp.where(low, q, zero), jnp.where(low, zero, q)], axis=0)
        scores.append(jnp.where(mask, _dot_nt(qs, kk), -jnp.inf))
    probs = []
    for i, s in enumerate(scores):
        m = jnp.max(s, axis=-1, keepdims=True)
        e = jnp.exp(s - m)
        den = jnp.sum(e, axis=-1, keepdims=True)
        probs.append((e / den).astype(BF16))
        lse = jnp.broadcast_to(m + jnp.log(den), (2 * sp, LANES))
        l_ref[i * sp:(i + 1) * sp, :] = jnp.where(low, lse[:sp], lse[sp:])
    for i, p in enumerate(probs):
        vprev = vp_ref[...] if i == 0 else v_ref[(i - 1) * sp:i * sp, :]
        o2 = _dot(p, jnp.concatenate([vprev, v_ref[i * sp:(i + 1) * sp, :]], axis=0))
        o_ref[i * sp:(i + 1) * sp, :] = jnp.where(low, o2[:sp], o2[sp:])


def _dilated(q, k, v, group):
    b, m, width = q.shape
    dilation = width // LANES
    tb = min(DIL_TILE, m)
    nsub = tb // DIL_SPAN
    cur = pl.BlockSpec((None, tb, LANES), lambda i, r, n: (i, n, r))
    prev = pl.BlockSpec((None, DIL_SPAN, LANES), lambda i, r, n: (i, jnp.maximum(n * nsub - 1, 0), r))
    return pl.pallas_call(
        functools.partial(_dilated_kernel, nsub=nsub),
        grid=(b, dilation, m // tb),
        in_specs=[cur, cur, cur, prev, prev],
        out_specs=[cur, cur],
        out_shape=[jax.ShapeDtypeStruct((b, m, width), F32)] * 2,
        compiler_params=pltpu.CompilerParams(dimension_semantics=("arbitrary",) * 3),
        name=f"dilated_g{group}",
    )(q, k, v, k, v)


def _dsa_kernel(shift_ref, qit_ref, wit_ref, ki_ref, qbt_ref, kb_ref, vta_ref, o_ref,
                hi_ref, lo_ref, sel_ref, fmax_ref, fmin_ref, blo_ref, tcnt_ref, tsel_ref, qic_ref, qbm_ref, m_ref, acc_ref,
                *, tq, ck, rb, topk):
    i = pl.program_id(1)
    nrows = (i + 1) * tq
    nfull = lax.div(nrows, ck)
    nblk = lax.div(nrows, rb)

    rows = lax.broadcasted_iota(I32, (2 * HEAD_DIM, tq), 0)
    for j in range(DSA_Q_HEADS):
        g = j // DSA_GROUP
        q = qbt_ref[j * HEAD_DIM:(j + 1) * HEAD_DIM, :]
        q2 = jnp.concatenate([q, q], axis=0)
        own = (rows >= g * HEAD_DIM) & (rows < (g + 1) * HEAD_DIM)
        qbm_ref[:, j * tq:(j + 1) * tq] = jnp.where(own, q2, jnp.zeros_like(q2))
    for h in range(IDX_HEADS):
        qic_ref[:, h * tq:(h + 1) * tq] = qit_ref[h * HEAD_DIM:(h + 1) * HEAD_DIM, :]

    def index_chunk(off, size):
        sc = _dot(ki_ref[pl.ds(off, size), :], qic_ref[...])
        acc = None
        for h in range(IDX_HEADS):
            t = jnp.maximum(sc[:, h * tq:(h + 1) * tq], 0.0) * wit_ref[h:h + 1, :]
            acc = t if acc is None else acc + t
        kpos = off + lax.broadcasted_iota(I32, (size, tq), 0)
        qpos = i * tq + lax.broadcasted_iota(I32, (size, tq), 1)
        acc = jnp.where(kpos <= qpos, acc, -jnp.inf)
        bits = lax.bitcast_convert_type(acc, I32)
        key = jnp.where(bits < 0, INT_MIN - bits, bits)
        hi_ref[pl.ds(off, size), :] = lax.shift_right_arithmetic(key, 16).astype(I16)
        lo_ref[pl.ds(off, size), :] = (key ^ 0x8000).astype(I16)

    def for_each_chunk(fn):
        def pair(t, carry):
            fn(pl.multiple_of(2 * t * ck, ck), ck)
            fn(pl.multiple_of((2 * t + 1) * ck, ck), ck)
            return carry
        lax.fori_loop(0, lax.div(nfull, 2), pair, 0)

        @pl.when(lax.rem(nfull, 2) == 1)
        def _():
            fn(pl.multiple_of((nfull - 1) * ck, ck), ck)

        @pl.when(nfull * ck < nrows)
        def _():
            fn(pl.multiple_of(nfull * ck, ck), ck // 2)

    for_each_chunk(index_chunk)


    def count(refs, pred, nb=nblk, emit=None):
        def body(r, accs):
            base = pl.multiple_of(r * rb, rb)
            blks = [ref[pl.ds(base, rb), :] for ref in refs]
            if emit is not None:
                emit(blks, base)
            accs = list(accs)
            for u in range(rb // 16):
                hit = pred([blk[u * 16:(u + 1) * 16, :] for blk in blks], base + u * 16)
                accs[u % N_ACC] = accs[u % N_ACC] + jnp.where(hit, jnp.int16(1), jnp.int16(0))
            return tuple(accs)
        accs = lax.fori_loop(0, nb, body, tuple(jnp.zeros((16, tq), I16) for _ in range(N_ACC)))
        tot = accs[0].astype(I32)
        for a in accs[1:]:
            tot = tot + a.astype(I32)
        return jnp.sum(tot, axis=0, keepdims=True)

    def rows16(v):
        return jnp.broadcast_to(v, (16, tq)).astype(I16)

    def bisect(ref, want):
        def step(it, u):
            uc = u | lax.shift_left(jnp.int32(1), 15 - it)
            cand = rows16(uc - 32768)
            cnt = count([ref], lambda b, p0: b[0] >= cand)
            return jnp.where(cnt >= want, uc, u)
        return lax.fori_loop(0, 16, step, jnp.zeros((1, tq), I32)) - 32768

    a32 = jnp.maximum(bisect(hi_ref, topk), HI_NEG_INF)
    a16 = rows16(a32)
    a_blk = jnp.broadcast_to(a32, (rb, tq)).astype(I16)

    fmax_ref[...] = jnp.full(fmax_ref.shape, I16_MIN, I16)
    fmin_ref[...] = jnp.full(fmin_ref.shape, I16_MIN, I16)

    def fold_block(r, carry):
        accs, most = carry
        base = pl.multiple_of(r * rb, rb)
        hi, lo = hi_ref[pl.ds(base, rb), :], lo_ref[pl.ds(base, rb), :]
        top = jnp.full((16, tq), I16_MIN, I16)
        low = jnp.full((16, tq), I16_MAX, I16)
        members = jnp.zeros((16, tq), I16)
        accs = list(accs)
        for u in range(rb // 16):
            h, l = hi[u * 16:(u + 1) * 16, :], lo[u * 16:(u + 1) * 16, :]
            member = h == a16
            up, down = jnp.where(member, l, jnp.int16(I16_MIN)), jnp.where(member, l, jnp.int16(I16_MAX))
            top = jnp.where(up > top, up, top)
            low = jnp.where(down < low, down, low)
            members = members + jnp.where(member, jnp.int16(1), jnp.int16(0))
            accs[u % N_ACC] = accs[u % N_ACC] + jnp.where(h > a16, jnp.int16(1), jnp.int16(0))
        row = pl.multiple_of(r * 16, 16)
        fmax_ref[pl.ds(row, 16), :] = top
        fmin_ref[pl.ds(row, 16), :] = jnp.where(members >= 2, low, jnp.int16(I16_MIN))
        return tuple(accs), jnp.where(members > most, members, most)

    accs, most = lax.fori_loop(0, nblk, fold_block, (tuple(jnp.zeros((16, tq), I16) for _ in range(N_ACC)),
                                                     jnp.zeros((16, tq), I16)))
    n_hi = accs[0].astype(I32)
    for a in accs[1:]:
        n_hi = n_hi + a.astype(I32)
    n_hi = jnp.sum(n_hi, axis=0, keepdims=True)
    want_lo = topk - n_hi

    def folded_step(it, u):
        uc = u | lax.shift_left(jnp.int32(1), 15 - it)
        cand = rows16(uc - 32768)
        accs = [jnp.zeros((16, tq), I16) for _ in range(N_ACC)]
        for r in range(fmax_ref.shape[0] // 16):
            for k, ref in enumerate((fmax_ref, fmin_ref)):
                hit = ref[r * 16:(r + 1) * 16, :] >= cand
                accs[(2 * r + k) % N_ACC] = accs[(2 * r + k) % N_ACC] + jnp.where(hit, jnp.int16(1), jnp.int16(0))
        tot = accs[0].astype(I32)
        for a in accs[1:]:
            tot = tot + a.astype(I32)
        return jnp.where(jnp.sum(tot, axis=0, keepdims=True) >= want_lo, uc, u)

    blo_ref[...] = jnp.broadcast_to(lax.fori_loop(0, 16, folded_step, jnp.zeros((1, tq), I32)) - 32768, blo_ref.shape)

    @pl.when(jnp.max(most.astype(I32)) > FOLD_MAX_MEMBERS)
    def _():
        def bucket(r, carry):
            base = pl.multiple_of(r * rb, rb)
            sel_ref[pl.ds(base, rb), :] = jnp.where(hi_ref[pl.ds(base, rb), :] == a_blk, lo_ref[pl.ds(base, rb), :],
                                                    jnp.int16(I16_MIN))
            return carry
        lax.fori_loop(0, nblk, bucket, 0)
        blo_ref[...] = jnp.broadcast_to(bisect(sel_ref, want_lo), blo_ref.shape)

    b32 = blo_ref[0:1, :]
    b32 = jnp.where(a32 == HI_NEG_INF, jnp.maximum(b32, I16_MIN + 1), b32)
    b16 = rows16(b32)
    b_blk = jnp.broadcast_to(b32, (rb, tq)).astype(I16)

    def emit_selection(blks, base):
        sel = (blks[0] > a_blk) | ((blks[0] == a_blk) & (blks[1] >= b_blk))
        sel_ref[pl.ds(base, rb), :] = jnp.where(sel, jnp.int16(BF16_ONE_BITS), jnp.int16(0))

    n_ge = count([hi_ref, lo_ref], lambda b, p0: (b[0] > a16) | ((b[0] == a16) & (b[1] >= b16)),
                 emit=emit_selection)

    need = n_ge > topk

    @pl.when(jnp.max(jnp.where(need, 1, 0)) > 0)
    def _():
        nb_max = tcnt_ref.shape[0]
        none = jnp.int32(2 ** 30)

        def tied(hi, lo, a, b):
            return (hi == a) & (lo == b)

        tcnt_ref[...] = jnp.zeros(tcnt_ref.shape, I32)

        def tie_block(r, carry):
            base = pl.multiple_of(r * rb, rb)
            hi, lo = hi_ref[pl.ds(base, rb), :], lo_ref[pl.ds(base, rb), :]
            acc = jnp.zeros((16, tq), I16)
            for u in range(rb // 16):
                acc = acc + jnp.where(tied(hi[u * 16:(u + 1) * 16, :], lo[u * 16:(u + 1) * 16, :], a16, b16),
                                      jnp.int16(1), jnp.int16(0))
            tcnt_ref[r] = jnp.broadcast_to(jnp.sum(acc.astype(I32), axis=0, keepdims=True), (8, tq))
            return carry

        lax.fori_loop(0, nblk, tie_block, 0)

        per_block = [tcnt_ref[r][0:1, :] for r in range(nb_max)]
        n_tied = per_block[0]
        for c in per_block[1:]:
            n_tied = n_tied + c
        want = jnp.where(need, topk - (n_ge - n_tied), none)
        last = jnp.full((1, tq), nb_max, I32)
        before = jnp.zeros((1, tq), I32)
        prefix = jnp.zeros((1, tq), I32)
        for r, c in enumerate(per_block):
            reached = (last == nb_max) & (prefix + c >= want)
            last = jnp.where(reached, r, last)
            before = jnp.where(reached, prefix, before)
            prefix = prefix + c
        want_here = want - before

        tsel_ref[...] = jnp.zeros(tsel_ref.shape, I16)

        def pick_block(r, carry):
            base = pl.multiple_of(r * rb, rb)
            here = jnp.broadcast_to(jnp.where(last == r, 1, 0), (rb, tq)).astype(I16) != 0
            hit = tied(hi_ref[pl.ds(base, rb), :], lo_ref[pl.ds(base, rb), :], a_blk, b_blk) & here
            tsel_ref[...] = tsel_ref[...] + jnp.where(hit, jnp.int16(1), jnp.int16(0))
            return carry

        lax.fori_loop(0, nblk, pick_block, 0)

        in_block = lax.broadcasted_iota(I32, (rb, tq), 0).astype(I16)
        block_bits = (rb - 1).bit_length()

        def row_step(it, e):
            ec = e | lax.shift_left(jnp.int32(1), block_bits - 1 - it)
            hit = (tsel_ref[...] != 0) & (in_block < jnp.broadcast_to(ec, (rb, tq)).astype(I16))
            ones = jnp.where(hit, jnp.int16(1), jnp.int16(0))
            acc = ones[0:16, :]
            for u in range(1, rb // 16):
                acc = acc + ones[u * 16:(u + 1) * 16, :]
            below = jnp.sum(acc.astype(I32), axis=0, keepdims=True)
            return jnp.where(below < want_here, ec, e)

        e_last = lax.fori_loop(0, block_bits, row_step, jnp.zeros((1, tq), I32))
        x = jnp.where(last < nb_max, jnp.minimum(last * rb + e_last, I16_MAX), I16_MAX)

        def demote(r, carry):
            base = pl.multiple_of(r * rb, rb)
            pos = (base + lax.broadcasted_iota(I32, (rb, tq), 0)).astype(I16)
            drop = (tied(hi_ref[pl.ds(base, rb), :], lo_ref[pl.ds(base, rb), :], a_blk, b_blk)
                    & (pos > jnp.broadcast_to(x, (rb, tq)).astype(I16)))
            sel_ref[pl.ds(base, rb), :] = jnp.where(drop, jnp.int16(0), sel_ref[pl.ds(base, rb), :])
            return carry

        lax.fori_loop(0, nblk, demote, 0)

    shift = shift_ref[0]

    @pl.when(shift < SAFE_SHIFT_LOG2)
    def _():
        acc_ref[...] = jnp.zeros(acc_ref.shape, F32)

        def attend_chunk(off, size):
            kch = kb_ref[pl.ds(off, size), :]
            msk = lax.bitcast_convert_type(sel_ref[pl.ds(off, size), :], BF16)
            scs = [_dot(kch, qbm_ref[:, g * DSA_GROUP * tq:(g + 1) * DSA_GROUP * tq]) for g in range(DSA_KV_HEADS)]
            ps = [jnp.exp2(sc - shift).astype(BF16) * _tile_lanes(msk, DSA_GROUP) for sc in scs]
            for g, p in enumerate(ps):
                acc_ref[g] += _dot(vta_ref[g * V_AUG:(g + 1) * V_AUG, pl.ds(off, size)], p)

        for_each_chunk(attend_chunk)

    @pl.when(shift >= SAFE_SHIFT_LOG2)
    def _():
        m_ref[...] = jnp.full(m_ref.shape, NEG_BIG, F32)
        acc_ref[...] = jnp.zeros(acc_ref.shape, F32)

        def attend_block(r, carry):
            off = pl.multiple_of(r * rb, rb)
            kch = kb_ref[pl.ds(off, rb), :]
            sel = sel_ref[pl.ds(off, rb), :].astype(I32) != 0
            for j in range(DSA_Q_HEADS):
                g = j // DSA_GROUP
                cols = slice((j % DSA_GROUP) * tq, (j % DSA_GROUP + 1) * tq)
                sc = jnp.where(sel, _dot(kch, qbm_ref[:, j * tq:(j + 1) * tq]), NEG_BIG)
                m_prev = m_ref[j]
                m_new = jnp.maximum(m_prev, jnp.max(sc, axis=0, keepdims=True))
                p = jnp.where(sel, jnp.exp2(sc - m_new), 0.0).astype(BF16)
                acc_ref[g, :, cols] = jnp.exp2(m_prev - m_new) * acc_ref[g, :, cols] + _dot(
                    vta_ref[g * V_AUG:(g + 1) * V_AUG, pl.ds(off, rb)], p)
                m_ref[j] = m_new
            return carry

        lax.fori_loop(0, nblk, attend_block, 0)

    for pair in range(DSA_Q_HEADS // 2):
        halves = []
        for j in (2 * pair, 2 * pair + 1):
            a = acc_ref[j // DSA_GROUP, :, (j % DSA_GROUP) * tq:(j % DSA_GROUP + 1) * tq]
            halves.append(a[:HEAD_DIM] / a[HEAD_DIM:HEAD_DIM + 1])
        o_ref[:, pair * LANES:(pair + 1) * LANES] = jnp.concatenate(halves, axis=0).T.astype(BF16)


def _dsa(shift, qit, wit, ki, qbt, kb, vta):
    b, _, s = qbt.shape
    tq = min(256, s)
    ck = min(512, s)
    rb = 256
    topk = min(DSA_TOPK_MAX, s // 4)
    qt = lambda w: pl.BlockSpec((None, w, tq), lambda i, j: (i, 0, j))
    seq = lambda w: pl.BlockSpec((None, s, w), lambda i, j: (i, 0, 0))
    return pl.pallas_call(
        functools.partial(_dsa_kernel, tq=tq, ck=ck, rb=rb, topk=topk),
        grid=(b, s // tq),
        in_specs=[pl.BlockSpec(memory_space=pltpu.SMEM), qt(D_QI), qt(IDX_HEADS), seq(HEAD_DIM), qt(D_QB), seq(D_KVB),
                  pl.BlockSpec((None, DSA_KV_HEADS * V_AUG, s), lambda i, j: (i, 0, 0))],
        out_specs=pl.BlockSpec((None, tq, D_QB), lambda i, j: (i, j, 0)),
        out_shape=jax.ShapeDtypeStruct((b, s, D_QB), BF16),
        scratch_shapes=[
            pltpu.VMEM((s, tq), I16),
            pltpu.VMEM((s, tq), I16),
            pltpu.VMEM((s, tq), I16),
            pltpu.VMEM((s // rb * 16, tq), I16),
            pltpu.VMEM((s // rb * 16, tq), I16),
            pltpu.VMEM((8, tq), I32),
            pltpu.VMEM((s // rb, 8, tq), I32),
            pltpu.VMEM((rb, tq), I16),
            pltpu.VMEM((HEAD_DIM, IDX_HEADS * tq), BF16),
            pltpu.VMEM((2 * HEAD_DIM, DSA_Q_HEADS * tq), BF16),
            pltpu.VMEM((DSA_Q_HEADS, 1, tq), F32),
            pltpu.VMEM((DSA_KV_HEADS, V_AUG, DSA_GROUP * tq), F32),
        ],
        compiler_params=pltpu.CompilerParams(dimension_semantics=("arbitrary", "arbitrary"),
                                             vmem_limit_bytes=VMEM_LIMIT),
        name="dsa",
    )(shift, qit, wit, ki, qbt, kb, vta)


def _merge_kernel(x_ref, g_ref, wg_ref, o0_ref, l0_ref, o1_ref, l1_ref, o2_ref, l2_ref, ob_ref, qc_ref,
                  km_ref, vm_ref, wa_ref, wb_ref, wc_ref, wo_ref, out_ref, *stage_refs):
    tm, d = x_ref.shape

    def token_major(ref, stage_ref):
        dil = ref.shape[1] // LANES
        if dil == 1:
            return ref[...]
        for r in range(dil):
            stage_ref[pl.ds(r, tm // dil, stride=dil), :] = ref[:, r * LANES:(r + 1) * LANES]
        return stage_ref[...]

    lses = (l0_ref[...], token_major(l1_ref, stage_refs[0]), token_major(l2_ref, stage_refs[1]))
    outs = (o0_ref[...], token_major(o1_ref, stage_refs[2]), token_major(o2_ref, stage_refs[3]))

    def rows_part(rows):
        n = rows.stop - rows.start
        x = x_ref[rows, :]
        h = _rms_rows(x, g_ref[...]).astype(BF16)

        low = _low_half((n, LANES))
        scores = []
        for mcol in range(MEM_HEADS // 2):
            q = qc_ref[rows, mcol * LANES:(mcol + 1) * LANES]
            zero = jnp.zeros_like(q)
            for qh in (jnp.where(low, q, zero), jnp.where(low, zero, q)):
                scores.append(_dot_nt(qh, km_ref[:, mcol * LANES:(mcol + 1) * LANES]))
        logits = [_dot(h, wg_ref[:, k * d:(k + 1) * d]) for k in range(3)]

        def gate(k):
            return 1.0 / (1.0 + jnp.exp(-logits[k]))

        l0, l1, l2 = (t[rows, :] for t in lses)
        o0, o1, o2 = (t[rows, :] for t in outs)
        mx = jnp.maximum(jnp.maximum(l0, l1), l2)
        e0, e1, e2 = jnp.exp(l0 - mx), jnp.exp(l1 - mx), jnp.exp(l2 - mx)
        oa = (e0 * o0 + e1 * o1 + e2 * o2) / (e0 + e1 + e2)
        merged = gate(0) * _dot(oa.astype(BF16), wa_ref[...])

        merged = merged + gate(1) * _dot(ob_ref[rows, :], wb_ref[...])

        probs = []
        for s in scores:
            e = jnp.exp(s - jnp.max(s, axis=-1, keepdims=True))
            probs.append((e / jnp.sum(e, axis=-1, keepdims=True)).astype(BF16))
        heads = [_dot(p, vm_ref[:, (k // 2) * LANES:(k // 2 + 1) * LANES]) for k, p in enumerate(probs)]
        oc = jnp.concatenate([jnp.where(low, heads[2 * mcol], heads[2 * mcol + 1]) for mcol in range(MEM_HEADS // 2)],
                             axis=1)
        merged = merged + gate(2) * _dot(oc.astype(BF16), wc_ref[...])

        out_ref[rows, :] = x + _dot(merged.astype(BF16), wo_ref[...])

    parts = MERGE_ROW_PARTS if tm % (8 * MERGE_ROW_PARTS) == 0 else 1
    for part in range(parts):
        rows_part(slice(part * (tm // parts), (part + 1) * (tm // parts)))


def _merge(x, g_mix, w_gate, dil, ob, qc, km, vm, w_a, w_b, w_c, w_o, tm):
    b, s, d = x.shape
    mlen = km.shape[1]
    tok = lambda w: pl.BlockSpec((None, tm, w), lambda i, j: (i, j, 0))
    full = lambda *shape: pl.BlockSpec(shape, lambda i, j: (0,) * len(shape))
    memb = pl.BlockSpec((None, mlen, D_QC), lambda i, j: (i, 0, 0))
    dil_args = [t for pair in dil for t in pair]
    dil_specs = [pl.BlockSpec((None, tm // dl, dl * LANES), lambda i, j: (i, j, 0))
                 for _, dl in DIL_GROUPS for _ in range(2)]
    return pl.pallas_call(
        _merge_kernel,
        grid=(b, s // tm),
        in_specs=[tok(d), full(1, d), full(d, 3 * d)] + dil_specs + [tok(D_QB), tok(D_QC), memb, memb,
                  full(LANES, d), full(D_QB, d), full(D_QC, d), full(d, d)],
        out_specs=tok(d),
        out_shape=jax.ShapeDtypeStruct((b, s, d), F32),
        scratch_shapes=[pltpu.VMEM((tm, LANES), F32)] * 4,
        compiler_params=pltpu.CompilerParams(dimension_semantics=("arbitrary", "arbitrary"),
                                             vmem_limit_bytes=VMEM_LIMIT),
        name="merge",
    )(x, g_mix, w_gate, *dil_args, ob, qc, km, vm, w_a, w_b, w_c, w_o)


def _mlp_kernel(x_ref, g_ref, w1_ref, w2_ref, out_ref, *, fchunk):
    x = x_ref[...]
    h = _rms_rows(x, g_ref[...]).astype(BF16)
    acc = x
    for c in range(w1_ref.shape[1] // fchunk):
        u = jnp.maximum(_dot(h, w1_ref[:, c * fchunk:(c + 1) * fchunk]), 0.0)
        acc = acc + _dot((u * u).astype(BF16), w2_ref[c * fchunk:(c + 1) * fchunk, :])
    out_ref[...] = acc


def _mlp(x, g_mlp, w_1, w_2, tm):
    b, s, d = x.shape
    f = w_1.shape[1]
    tok = pl.BlockSpec((None, tm, d), lambda i, j: (i, j, 0))
    full = lambda *shape: pl.BlockSpec(shape, lambda i, j: (0,) * len(shape), pipeline_mode=pl.Buffered(1))
    return pl.pallas_call(
        functools.partial(_mlp_kernel, fchunk=min(1024, f)),
        grid=(b, s // tm),
        in_specs=[tok, full(1, d), full(d, f), full(f, d)],
        out_specs=tok,
        out_shape=jax.ShapeDtypeStruct((b, s, d), F32),
        compiler_params=pltpu.CompilerParams(dimension_semantics=("arbitrary", "arbitrary"),
                                             vmem_limit_bytes=VMEM_LIMIT),
        name="mlp",
    )(x, g_mlp, w_1, w_2)


def _rotary_tables(positions):
    inv = jnp.power(jnp.float32(ROPE_THETA), -jnp.arange(ROT_HALF, dtype=F32) / ROT_HALF)
    ang = positions.astype(F32)[:, None, :] * inv[None, :, None]
    return jnp.cos(ang), jnp.sin(ang)


def _block_diag_mean(width):
    r = jnp.arange(width) // HEAD_DIM
    return jnp.where(r[:, None] == r[None, :], 1.0 / HEAD_DIM, 0.0).astype(BF16)


def _layer(x, mem, tables, g_mix, g_mem, w_in, g_qa, g_ka, g_qb, g_kb, g_qc, g_kc,
           w_mem_kv, w_a, w_b, w_c, w_o, g_mlp, w_1, w_2):
    b, s, d = x.shape
    tm = min(TOKEN_TILE, s)
    cos_t, sin_t = tables
    bd = _block_diag_mean(D_QA)
    scale = HEAD_DIM ** -0.5

    offs, acc = [], 0
    for w in (D_QA, D_QA, D_QA, D_QB, D_KVB, D_KVB, D_QI, HEAD_DIM, IDX_HEADS, D_QC):
        offs.append((acc, acc + w))
        acc += w
    seg = lambda k: w_in[:, offs[k][0]:offs[k][1]]
    w_std = jnp.concatenate([seg(0), seg(1), seg(2), seg(4), seg(7), seg(7), seg(9)], axis=1).astype(BF16)
    w_t = jnp.concatenate([seg(3), seg(6), seg(5), seg(8), jnp.zeros((d, _R_END - _R_WI - IDX_HEADS), w_in.dtype)],
                          axis=1).T.astype(BF16)
    w_gate = w_in[:, acc:].astype(BF16)

    tile6 = lambda g: jnp.tile(g, D_QA // HEAD_DIM)
    hg = jnp.stack([tile6(g_qa) * scale, tile6(g_ka), tile6(g_kb), tile6(g_qc) * scale,
                    jnp.zeros(D_QA), jnp.zeros(D_QA), jnp.zeros(D_QA), jnp.zeros(D_QA)]).astype(F32)
    gqt = jnp.broadcast_to((g_qb * (scale * LOG2E))[:, None], (HEAD_DIM, LANES)).astype(F32)
    shift = (HEAD_DIM * scale * LOG2E * SHIFT_MARGIN) * jnp.max(jnp.abs(g_qb)) * jnp.max(jnp.abs(g_kb))
    shift = jnp.reshape(shift, (1,)).astype(F32)

    km, vm = _memkv(mem, g_mem[None, :], w_mem_kv.astype(BF16), bd[:D_QC, :D_QC],
                    jnp.tile(g_kc, MEM_HEADS)[None, :])
    (q0, q1, q2, k0, k1, k2, v0, v1, v2, kb, ki, qc, qbt, qit, vta, wit) = _inproj(
        x, g_mix[None, :], w_std, w_t, cos_t, sin_t, bd, hg, gqt, tm)
    dil = [_dilated(q, k, v, g) for g, (q, k, v) in enumerate(((q0, k0, v0), (q1, k1, v1), (q2, k2, v2)))]
    ob = _dsa(shift, qit, wit, ki, qbt, kb, vta)
    x = _merge(x, g_mix[None, :], w_gate, dil, ob, qc, km, vm, w_a.astype(BF16), w_b.astype(BF16),
               w_c.astype(BF16), w_o.astype(BF16), tm)
    return _mlp(x, g_mlp[None, :], w_1.astype(BF16), w_2.astype(BF16), tm)


def kernel(x, mem, positions, g_mix, g_mem, w_in, g_qa, g_ka, g_qb, g_kb, g_qc, g_kc, w_mem_kv, w_a, w_b, w_c, w_o, g_mlp, w_1, w_2):
    tables = _rotary_tables(positions)
    for i in range(g_mix.shape[0]):
        x = _layer(x, mem, tables, g_mix[i], g_mem[i], w_in[i], g_qa[i], g_ka[i], g_qb[i], g_kb[i], g_qc[i],
                   g_kc[i], w_mem_kv[i], w_a[i], w_b[i], w_c[i], w_o[i], g_mlp[i], w_1[i], w_2[i])
    return x
```
